```python
import math
import jax
import jax.numpy as jnp
from jax import lax
import numpy as np

D_MODEL = 1024
BATCH = 4
SEQ = 4096
DEPTH = 1
DEC_BATCH = 128
DEC_SEQ = 4
PAST_LEN = 2048
PAGE_SIZE = 128

MIX_WIDTH = D_MODEL
GDN_WIDTH = MIX_WIDTH // 2
GDN_HEADS = 4
GDN_DK = GDN_WIDTH // GDN_HEADS
GDN_DV = GDN_WIDTH // GDN_HEADS
GDN_CONV = 4
GDN_CHUNK = 64
DIFF_WIDTH = MIX_WIDTH - GDN_WIDTH
DIFF_HEADS = 4
DIFF_DV = DIFF_WIDTH // DIFF_HEADS
DIFF_DQK = DIFF_DV // 2
Q_BLOCK = 128
D_FF = 2816
FFN_CONV = 3
EPS = 1e-6

COL_GDN_Z = 3 * GDN_WIDTH
COL_GDN_B = 4 * GDN_WIDTH
COL_GDN_A = COL_GDN_B + GDN_HEADS
COL_DIFF = COL_GDN_A + GDN_HEADS
IN_COLS = COL_DIFF + 3 * DIFF_WIDTH

kernel_name = 'hybrid_gdn_diffattn_step'


def _rmsnorm(x, g):
    xf = x.astype(jnp.float32)
    xf = xf * lax.rsqrt(jnp.mean(xf * xf, axis=-1, keepdims=True) + EPS)
    return xf.astype(x.dtype) * g.astype(x.dtype)


def _l2norm(x):
    xf = x.astype(jnp.float32)
    return (xf * lax.rsqrt(jnp.sum(xf * xf, axis=-1, keepdims=True) + EPS)).astype(x.dtype)


def _causal_dwconv(x, buf, w, b=None):
    width = w.shape[0]
    L = x.shape[1]
    xp = jnp.concatenate([buf.astype(x.dtype), x], axis=1)
    y = xp[:, 0:L] * w[0]
    for j in range(1, width):
        y = y + xp[:, j:j + L] * w[j]
    if b is not None:
        y = y + b
    return y, xp[:, L:]


def _to_chunks(t, n, c):
    t = t.reshape(t.shape[0], n, c, *t.shape[2:])
    return jnp.moveaxis(t, (1, 3), (0, 2))


def _gated_delta_chunked(q, k, v, g, beta, S0):
    B, L, H, DK = q.shape
    DV = v.shape[-1]
    f32 = jnp.float32
    C = min(GDN_CHUNK, L)
    pad = (-L) % C
    n = (L + pad) // C

    def prep(t):
        t = t.astype(f32)
        return _to_chunks(jnp.pad(t, [(0, 0), (0, pad)] + [(0, 0)] * (t.ndim - 2)), n, C)

    qc, kc, vc, gc, bc = prep(q), prep(k), prep(v), prep(g), prep(beta)
    gcum = jnp.cumsum(gc, axis=-1)
    idx = jnp.arange(C)
    causal = idx[:, None] >= idx[None, :]
    decay = jnp.exp(jnp.where(causal, gcum[..., :, None] - gcum[..., None, :], -jnp.inf))
    kb = kc * bc[..., None]
    m = jnp.einsum('nbhik,nbhjk->nbhij', kb, kc) * decay
    a = jnp.where(idx[:, None] > idx[None, :], m, 0.0) + jnp.eye(C, dtype=f32)
    rhs = jnp.concatenate([vc * bc[..., None], kb * jnp.exp(gcum)[..., None]], axis=-1)
    sol = lax.linalg.triangular_solve(a, rhs, left_side=True, lower=True, unit_diagonal=True)
    u, w = sol[..., :DV], sol[..., DV:]

    def step(S, inp):
        q_i, k_i, u_i, w_i, g_i, d_i = inp
        v_new = u_i - jnp.einsum('bhck,bhkv->bhcv', w_i, S)
        intra = jnp.einsum('bhik,bhjk->bhij', q_i, k_i) * d_i
        o_i = (jnp.einsum('bhck,bhkv->bhcv', q_i * jnp.exp(g_i)[..., None], S)
               + jnp.einsum('bhij,bhjv->bhiv', intra, v_new))
        g_last = g_i[..., -1]
        S = (S * jnp.exp(g_last)[..., None, None]
             + jnp.einsum('bhck,bhcv->bhkv', k_i * jnp.exp(g_last[..., None] - g_i)[..., None], v_new))
        return S, o_i

    S_final, o = lax.scan(step, S0.astype(f32), (qc, kc, u, w, gcum, decay))
    o = jnp.transpose(o, (1, 0, 3, 2, 4)).reshape(B, n * C, H, DV)[:, :L]
    return o, S_final


def _diff_attention(q, k, v, lam, past_len):
    B, L = q.shape[0], q.shape[1]
    T = k.shape[1]
    H, DV = v.shape[2], v.shape[3]
    k_pos = jnp.arange(T)
    scale = DIFF_DQK ** -0.5

    def attend(qb, q_pos):
        s = jnp.einsum('bqhcd,bthcd->bhcqt', qb, k).astype(jnp.float32) * scale
        s = jnp.where(k_pos[None, :] <= q_pos[:, None], s, -jnp.inf)
        p = jax.nn.softmax(s, axis=-1)
        amap = p[:, :, 0] - lam * p[:, :, 1]
        return jnp.einsum('bhqt,bthv->bqhv', amap.astype(v.dtype), v)

    q_pos = past_len + jnp.arange(L)
    if L % Q_BLOCK == 0 and L > Q_BLOCK:
        nb = L // Q_BLOCK
        qb = jnp.moveaxis(q.reshape(B, nb, Q_BLOCK, H, 2, DIFF_DQK), 1, 0)
        pb = q_pos.reshape(nb, Q_BLOCK)
        o = lax.map(lambda xs: attend(xs[0], xs[1]), (qb, pb))
        return jnp.moveaxis(o, 0, 1).reshape(B, L, H, DV)
    return attend(q, q_pos)


def _hybrid_layer(x, gdn_conv_buf, gdn_S0, past_k, past_v, ffn_buf, wl, lam_init):
    (attn_norm_g, w_in, gdn_conv_w, gdn_A_log, gdn_dt_bias, gdn_out_norm_g,
     diff_q_norm_g, diff_k_norm_g, lq1, lk1, lq2, lk2, diff_subln_g, w_out,
     ffn_norm_g, w_up, ffn_conv_w, ffn_conv_b, w_down) = wl
    f32 = jnp.float32
    B, L, _ = x.shape
    P = past_k.shape[1]
    xn = _rmsnorm(x, attn_norm_g)
    proj = xn @ w_in

    qkv, new_gdn_conv = _causal_dwconv(proj[..., :COL_GDN_Z], gdn_conv_buf, gdn_conv_w)
    qkv = jax.nn.silu(qkv)
    gq = _l2norm(qkv[..., :GDN_WIDTH].reshape(B, L, GDN_HEADS, GDN_DK)) * (GDN_DK ** -0.5)
    gk = _l2norm(qkv[..., GDN_WIDTH:2 * GDN_WIDTH].reshape(B, L, GDN_HEADS, GDN_DK))
    gv = qkv[..., 2 * GDN_WIDTH:].reshape(B, L, GDN_HEADS, GDN_DV)
    z = proj[..., COL_GDN_Z:COL_GDN_B].reshape(B, L, GDN_HEADS, GDN_DV)
    beta = jax.nn.sigmoid(proj[..., COL_GDN_B:COL_GDN_A].astype(f32))
    g = -jnp.exp(gdn_A_log.astype(f32)) * jax.nn.softplus(
        proj[..., COL_GDN_A:COL_DIFF].astype(f32) + gdn_dt_bias.astype(f32))
    o_g, new_S = _gated_delta_chunked(gq, gk, gv, g, beta, gdn_S0)
    o_g = (_rmsnorm(o_g.astype(x.dtype), gdn_out_norm_g) * jax.nn.silu(z)).reshape(B, L, GDN_WIDTH)

    d = proj[..., COL_DIFF:]
    dq = _rmsnorm(d[..., :DIFF_WIDTH].reshape(B, L, DIFF_HEADS, 2, DIFF_DQK), diff_q_norm_g)
    dk = _rmsnorm(d[..., DIFF_WIDTH:2 * DIFF_WIDTH].reshape(B, L, DIFF_HEADS, 2, DIFF_DQK), diff_k_norm_g)
    dv = d[..., 2 * DIFF_WIDTH:].reshape(B, L, DIFF_HEADS, DIFF_DV)
    new_k = dk.reshape(B, L, DIFF_HEADS, 2 * DIFF_DQK)
    keys = jnp.concatenate([past_k.astype(x.dtype), new_k], axis=1).reshape(B, P + L, DIFF_HEADS, 2, DIFF_DQK)
    vals = jnp.concatenate([past_v.astype(x.dtype), dv], axis=1)
    lam = (jnp.exp(jnp.sum(lq1.astype(f32) * lk1.astype(f32)))
           - jnp.exp(jnp.sum(lq2.astype(f32) * lk2.astype(f32))) + lam_init)
    o_d = _diff_attention(dq, keys, vals, lam, P)
    o_d = (_rmsnorm(o_d, diff_subln_g) * (1.0 - lam_init)).reshape(B, L, DIFF_WIDTH)

    h = x + jnp.concatenate([o_g, o_d], axis=-1) @ w_out

    hn = _rmsnorm(h, ffn_norm_g)
    u, new_ffn = _causal_dwconv(hn @ w_up, ffn_buf, ffn_conv_w, ffn_conv_b)
    y = h + (jax.nn.silu(u[..., :D_FF]) * u[..., D_FF:]) @ w_down
    return y, new_gdn_conv, new_S, new_k, dv, new_ffn


def setup_inputs(seed: int = 0) -> dict:
    key = jax.random.key(seed)
    ks = jax.random.split(key, 32)
    f32 = jnp.float32
    n_pages = PAST_LEN // PAGE_SIZE
    n_pool = (DEC_BATCH * n_pages * 5) // 4

    def nrm(k, shape, scale):
        return jax.random.normal(k, shape, f32) * scale

    dt = jnp.exp(jax.random.uniform(ks[12], (DEPTH, GDN_HEADS), f32, math.log(1e-3), math.log(1e-1)))
    return {
        'x_prompt': nrm(ks[0], (BATCH, SEQ, D_MODEL), 1.0),
        'x_sample': nrm(ks[1], (DEC_BATCH, DEC_SEQ, D_MODEL), 1.0),
        'state_gdn_conv': nrm(ks[2], (DEPTH, DEC_BATCH, GDN_CONV - 1, 3 * GDN_WIDTH), 1.0),
        'state_gdn_S': nrm(ks[3], (DEPTH, DEC_BATCH, GDN_HEADS, GDN_DK, GDN_DV), 0.1),
        'cache_k': nrm(ks[4], (DEPTH, n_pool, PAGE_SIZE, DIFF_HEADS, 2 * DIFF_DQK), 1.0),
        'cache_v': nrm(ks[5], (DEPTH, n_pool, PAGE_SIZE, DIFF_HEADS, DIFF_DV), 1.0),
        'page_table': jax.random.permutation(ks[6], n_pool)[:DEC_BATCH * n_pages]
                      .reshape(DEC_BATCH, n_pages).astype(jnp.int32),
        'state_ffn_conv': nrm(ks[7], (DEPTH, DEC_BATCH, FFN_CONV - 1, 2 * D_FF), 1.0),
        'attn_norm_g': 1.0 + nrm(ks[8], (DEPTH, D_MODEL), 0.02),
        'w_in': nrm(ks[9], (DEPTH, D_MODEL, IN_COLS), D_MODEL ** -0.5),
        'gdn_conv_w': nrm(ks[10], (DEPTH, GDN_CONV, 3 * GDN_WIDTH), GDN_CONV ** -0.5),
        'gdn_A_log': jnp.log(jax.random.uniform(ks[11], (DEPTH, GDN_HEADS), f32, 1.0, 16.0)),
        'gdn_dt_bias': dt + jnp.log(-jnp.expm1(-dt)),
        'gdn_out_norm_g': 1.0 + nrm(ks[13], (DEPTH, GDN_DV), 0.02),
        'diff_q_norm_g': 1.0 + nrm(ks[14], (DEPTH, DIFF_DQK), 0.02),
        'diff_k_norm_g': 1.0 + nrm(ks[15], (DEPTH, DIFF_DQK), 0.02),
        'diff_lambda_q1': nrm(ks[16], (DEPTH, DIFF_DQK), 0.1),
        'diff_lambda_k1': nrm(ks[17], (DEPTH, DIFF_DQK), 0.1),
        'diff_lambda_q2': nrm(ks[18], (DEPTH, DIFF_DQK), 0.1),
        'diff_lambda_k2': nrm(ks[19], (DEPTH, DIFF_DQK), 0.1),
        'diff_subln_g': 1.0 + nrm(ks[20], (DEPTH, DIFF_DV), 0.02),
        'w_out': nrm(ks[21], (DEPTH, MIX_WIDTH, D_MODEL), MIX_WIDTH ** -0.5),
        'ffn_norm_g': 1.0 + nrm(ks[22], (DEPTH, D_MODEL), 0.02),
        'w_up': nrm(ks[23], (DEPTH, D_MODEL, 2 * D_FF), D_MODEL ** -0.5),
        'ffn_conv_w': nrm(ks[24], (DEPTH, FFN_CONV, 2 * D_FF), FFN_CONV ** -0.5),
        'ffn_conv_b': nrm(ks[25], (DEPTH, 2 * D_FF), 0.01),
        'w_down': nrm(ks[26], (DEPTH, D_FF, D_MODEL), D_FF ** -0.5),
    }


def reference(x_prompt, x_sample, state_gdn_conv, state_gdn_S, cache_k, cache_v, page_table,
              state_ffn_conv, attn_norm_g, w_in, gdn_conv_w, gdn_A_log, gdn_dt_bias,
              gdn_out_norm_g, diff_q_norm_g, diff_k_norm_g, diff_lambda_q1, diff_lambda_k1,
              diff_lambda_q2, diff_lambda_k2, diff_subln_g, w_out, ffn_norm_g, w_up,
              ffn_conv_w, ffn_conv_b, w_down):
    B = x_prompt.shape[0]
    Bd = x_sample.shape[0]
    dt = x_prompt.dtype
    n_pages = page_table.shape[1]
    past_len = n_pages * cache_k.shape[2]
    hp, hs = x_prompt, x_sample
    pc, pS, pk, pv, pf = [], [], [], [], []
    sc, sS, sk, sv, sf = [], [], [], [], []
    for l in range(DEPTH):
        wl = (attn_norm_g[l], w_in[l], gdn_conv_w[l], gdn_A_log[l], gdn_dt_bias[l],
              gdn_out_norm_g[l], diff_q_norm_g[l], diff_k_norm_g[l], diff_lambda_q1[l],
              diff_lambda_k1[l], diff_lambda_q2[l], diff_lambda_k2[l], diff_subln_g[l],
              w_out[l], ffn_norm_g[l], w_up[l], ffn_conv_w[l], ffn_conv_b[l], w_down[l])
        lam_init = 0.8 - 0.6 * math.exp(-0.3 * l)
        hp, c1, S1, k1, v1, f1 = _hybrid_layer(
            hp,
            jnp.zeros((B, GDN_CONV - 1, 3 * GDN_WIDTH), dt),
            jnp.zeros((B, GDN_HEADS, GDN_DK, GDN_DV), jnp.float32),
            jnp.zeros((B, 0, DIFF_HEADS, 2 * DIFF_DQK), dt),
            jnp.zeros((B, 0, DIFF_HEADS, DIFF_DV), dt),
            jnp.zeros((B, FFN_CONV - 1, 2 * D_FF), dt),
            wl, lam_init)
        past_k = cache_k[l][page_table].reshape(Bd, past_len, DIFF_HEADS, 2 * DIFF_DQK)
        past_v = cache_v[l][page_table].reshape(Bd, past_len, DIFF_HEADS, DIFF_DV)
        hs, c2, S2, k2, v2, f2 = _hybrid_layer(
            hs, state_gdn_conv[l], state_gdn_S[l], past_k, past_v, state_ffn_conv[l], wl, lam_init)
        pc.append(c1); pS.append(S1); pk.append(k1); pv.append(v1); pf.append(f1)
        sc.append(c2); sS.append(S2); sk.append(k2); sv.append(v2); sf.append(f2)
    return (hp, hs, jnp.stack(pc), jnp.stack(pS), jnp.stack(pk), jnp.stack(pv), jnp.stack(pf),
            jnp.stack(sc), jnp.stack(sS), jnp.stack(sk), jnp.stack(sv), jnp.stack(sf))
```

```python
import functools
import math

import jax
import jax.numpy as jnp
from jax import lax
from jax.experimental import pallas as pl
from jax.experimental.pallas import tpu as pltpu

F32 = jnp.float32
BF16 = jnp.bfloat16
EPS = 1e-6
NEG = -1e30

GDN_HEADS = 4
GDN_DK = 128
GDN_CONV = 4
GDN_CHUNK = 64
DIFF_HEADS = 4
DIFF_DV = 128
DIFF_DQK = 64
FFN_CONV = 3
LANES = 128
SUBLANES = 8
VMEM_LIMIT = 56 * 1024 * 1024


def _cparams(sem):
    return pltpu.CompilerParams(dimension_semantics=sem, vmem_limit_bytes=VMEM_LIMIT)


def _const_spec(shape):
    nd = len(shape)
    return pl.BlockSpec(shape, lambda *_: (0,) * nd, pipeline_mode=pl.Buffered(1))


def _dot(a, b):
    return jnp.dot(a, b, preferred_element_type=F32)


def _dot_nt(a, b):
    return lax.dot_general(a, b, (((1,), (1,)), ((), ())), preferred_element_type=F32)


def _dot_tn(a, b):
    return lax.dot_general(a, b, (((0,), (0,)), ((), ())), preferred_element_type=F32)


def _softplus(x):
    return jnp.maximum(x, 0.0) + jnp.log1p(jnp.exp(-jnp.abs(x)))


def _silu(x):
    return x * jax.nn.sigmoid(x)


def _split3(x):
    hi = x.astype(BF16)
    r = x - hi.astype(F32)
    mid = r.astype(BF16)
    lo = (r - mid.astype(F32)).astype(BF16)
    return hi, mid, lo


def _inproj_kernel(x_ref, g_ref, wm_ref, wbt_ref, gq_ref, gk_ref, grp_ref,
                   qkv_ref, z_ref, ba_ref, bat_ref, knew_ref, vnew_ref, qa_ref, ka_ref, va_ref,
                   *, gw, dw):
    x = x_ref[...]
    xn = x * lax.rsqrt(jnp.mean(x * x, axis=-1, keepdims=True) + EPS) * g_ref[...]
    xb = xn.astype(BF16)
    proj = _dot(xb, wm_ref[...])
    qkv_ref[...] = proj[:, :3 * gw]
    z_ref[...] = proj[:, 3 * gw:4 * gw]
    c0 = 4 * gw
    ba_ref[...] = proj[:, c0 + 3 * dw:c0 + 3 * dw + LANES]
    bat_ref[...] = _dot_nt(wbt_ref[...], xb)
    dq = proj[:, c0:c0 + dw]
    dk = proj[:, c0 + dw:c0 + 2 * dw]
    dv = proj[:, c0 + 2 * dw:c0 + 3 * dw]
    grp = grp_ref[...]
    msq = _dot((dq * dq).astype(BF16), grp)
    msk = _dot((dk * dk).astype(BF16), grp)
    dqn = dq * lax.rsqrt(msq + EPS) * gq_ref[...]
    dkn = dk * lax.rsqrt(msk + EPS) * gk_ref[...]
    knew_ref[...] = dkn
    vnew_ref[...] = dv
    qa_ref[...] = (dqn * (DIFF_DQK ** -0.5)).astype(BF16)
    ka_ref[...] = dkn.astype(BF16)
    va_ref[...] = dv.astype(BF16)


def _inproj(x2d, g_attn, w_main, w_ba_t, gq_t, gk_t, grp, *, tm, gw, dw):
    n, d = x2d.shape
    ncols = w_main.shape[1]
    row = lambda i: (i, 0)
    out_shape = (
        jax.ShapeDtypeStruct((n, 3 * gw), F32),
        jax.ShapeDtypeStruct((n, gw), F32),
        jax.ShapeDtypeStruct((n, LANES), F32),
        jax.ShapeDtypeStruct((SUBLANES, n), F32),
        jax.ShapeDtypeStruct((n, dw), F32),
        jax.ShapeDtypeStruct((n, dw), F32),
        jax.ShapeDtypeStruct((n, dw), BF16),
        jax.ShapeDtypeStruct((n, dw), BF16),
        jax.ShapeDtypeStruct((n, dw), BF16),
    )
    out_specs = (
        pl.BlockSpec((tm, 3 * gw), row), pl.BlockSpec((tm, gw), row), pl.BlockSpec((tm, LANES), row),
        pl.BlockSpec((SUBLANES, tm), lambda i: (0, i)),
        pl.BlockSpec((tm, dw), row), pl.BlockSpec((tm, dw), row),
        pl.BlockSpec((tm, dw), row), pl.BlockSpec((tm, dw), row), pl.BlockSpec((tm, dw), row),
    )
    in_specs = [
        pl.BlockSpec((tm, d), row), _const_spec((1, d)), _const_spec((d, ncols)),
        _const_spec((SUBLANES, d)), _const_spec((1, dw)), _const_spec((1, dw)), _const_spec((dw, dw)),
    ]
    return pl.pallas_call(
        functools.partial(_inproj_kernel, gw=gw, dw=dw),
        grid=(n // tm,), in_specs=in_specs, out_specs=out_specs, out_shape=out_shape,
        compiler_params=_cparams(("parallel",)), name="inproj",
    )(x2d, g_attn, w_main, w_ba_t, gq_t, gk_t, grp)


def _gdn_prompt_kernel(qkv_ref, z_ref, ba_ref, bat_ref, cw_ref, alr_ref, dtr_ref, alc_ref, dtc_ref, gn_ref,
                       og_ref, sout_ref, xbuf, s_scr, *, tl):
    i = pl.program_id(1)
    nh, dk, c = GDN_HEADS, GDN_DK, GDN_CHUNK
    gw = nh * dk
    hr = SUBLANES

    @pl.when(i == 0)
    def _():
        xbuf[0:hr, :] = jnp.zeros((hr, 3 * gw), F32)
        s_scr[...] = jnp.zeros_like(s_scr)

    xbuf[hr:hr + tl, :] = qkv_ref[...]
    cw = cw_ref[...]
    y = cw[GDN_CONV - 1:GDN_CONV, :] * xbuf[hr:hr + tl, :]
    for j in range(1, GDN_CONV):
        y = y + cw[GDN_CONV - 1 - j:GDN_CONV - j, :] * xbuf[hr - j:hr - j + tl, :]
    xbuf[0:hr, :] = xbuf[tl:tl + hr, :]
    y = _silu(y)

    ba = ba_ref[...]
    beta_c = jax.nn.sigmoid(ba)
    g_c = -jnp.exp(alr_ref[...]) * _softplus(ba + dtr_ref[...])
    beta_rep = [jnp.broadcast_to(beta_c[:, h:h + 1], (tl, dk)) for h in range(nh)]
    g_rep = jnp.concatenate([jnp.broadcast_to(g_c[:, nh + h:nh + h + 1], (tl, dk)) for h in range(nh)], axis=1)
    bat = bat_ref[...]
    g_r = -jnp.exp(alc_ref[...]) * _softplus(bat + dtc_ref[...])

    ii = lax.broadcasted_iota(jnp.int32, (tl, tl), 0)
    jj = lax.broadcasted_iota(jnp.int32, (tl, tl), 1)
    same = lax.shift_right_logical(ii, 6) == lax.shift_right_logical(jj, 6)
    tri = jnp.where(same & (ii >= jj), 1.0, 0.0).astype(BF16)
    triu = jnp.where(same & (ii <= jj), 1.0, 0.0).astype(BF16)
    gc_rep = sum(_dot(tri, p) for p in _split3(g_rep))
    gc_r = sum(_dot(p, triu) for p in _split3(g_r))

    pr = 2 * c
    pi = lax.broadcasted_iota(jnp.int32, (pr, pr), 0)
    pj = lax.broadcasted_iota(jnp.int32, (pr, pr), 1)
    psame = lax.shift_right_logical(pi, 6) == lax.shift_right_logical(pj, 6)
    lower = psame & (pi >= pj)
    strict = psame & (pi > pj)
    eye = jnp.where(pi == pj, 1.0, 0.0).astype(F32)
    zeros_c = jnp.zeros((c, dk), F32)

    for h in range(nh):
        q = y[:, h * dk:(h + 1) * dk]
        k = y[:, gw + h * dk:gw + (h + 1) * dk]
        v = y[:, 2 * gw + h * dk:2 * gw + (h + 1) * dk]
        q = q * lax.rsqrt(jnp.sum(q * q, axis=-1, keepdims=True) + EPS) * (dk ** -0.5)
        k = k * lax.rsqrt(jnp.sum(k * k, axis=-1, keepdims=True) + EPS)
        gc_h = gc_rep[:, h * dk:(h + 1) * dk]
        kb = k * beta_rep[h]
        vb = v * beta_rep[h]
        o_rows = []
        for p in range(tl // pr):
            r0 = p * pr
            kp, qp = k[r0:r0 + pr], q[r0:r0 + pr]
            gcp = gc_h[r0:r0 + pr]
            grow = gc_r[nh + h:nh + h + 1, r0:r0 + pr]
            dec = jnp.exp(jnp.where(lower, gcp - grow, NEG))
            kbp = kb[r0:r0 + pr]
            kpb = kp.astype(BF16)
            nmat = jnp.where(strict, _dot_nt(kbp.astype(BF16), kpb) * dec, 0.0)
            m = -nmat
            a = eye + m
            pw = _dot(m.astype(BF16), m.astype(BF16))
            for it in range(1, 6):
                pwb = pw.astype(BF16)
                if it < 5:
                    res = _dot(pwb, jnp.concatenate([a.astype(BF16), pwb], axis=1))
                    a = a + res[:, :pr]
                    pw = res[:, pr:]
                else:
                    a = a + _dot(pwb, a.astype(BF16))
            rhs = jnp.concatenate([vb[r0:r0 + pr], kbp * jnp.exp(gcp)], axis=1)
            sol = _dot(a.astype(BF16), rhs.astype(BF16))
            u, w = sol[:, :dk], sol[:, dk:]
            qk = _dot_nt(qp.astype(BF16), kpb) * dec
            qg = qp * jnp.exp(gcp)
            for cc in range(2):
                c0 = cc * c
                glast = gcp[c0 + c - 1:c0 + c, :]
                kd = kp[c0:c0 + c] * jnp.exp(glast - gcp[c0:c0 + c])
                s_h = s_scr[h]
                wq = jnp.concatenate([w[c0:c0 + c], qg[c0:c0 + c]], axis=0)
                ws = _dot(wq.astype(BF16), s_h.astype(BF16))
                v_new = u[c0:c0 + c] - ws[:c]
                vn_pad = jnp.concatenate([v_new, zeros_c] if cc == 0 else [zeros_c, v_new], axis=0)
                o_rows.append(ws[c:] + _dot(qk[c0:c0 + c].astype(BF16), vn_pad.astype(BF16)))
                s_scr[h] = s_h * jnp.exp(glast) + _dot_tn(kd.astype(BF16), v_new.astype(BF16))
        o = jnp.concatenate(o_rows, axis=0)
        o = o * lax.rsqrt(jnp.mean(o * o, axis=-1, keepdims=True) + EPS) * gn_ref[...]
        og_ref[:, h * dk:(h + 1) * dk] = (o * _silu(z_ref[:, h * dk:(h + 1) * dk])).astype(og_ref.dtype)

    @pl.when(i == pl.num_programs(1) - 1)
    def _():
        sout_ref[...] = s_scr[...]


def _gdn_prompt(qkv, z, ba, bat, cw, alr, dtr, alc, dtc, gn, *, nb, seq, tl):
    nh, dk = GDN_HEADS, GDN_DK
    gw = nh * dk
    nt = seq // tl
    row = lambda b, i: (b * nt + i, 0)
    in_specs = [
        pl.BlockSpec((tl, 3 * gw), row), pl.BlockSpec((tl, gw), row), pl.BlockSpec((tl, LANES), row),
        pl.BlockSpec((SUBLANES, tl), lambda b, i: (0, b * nt + i)),
        _const_spec((GDN_CONV, 3 * gw)), _const_spec((1, LANES)), _const_spec((1, LANES)),
        _const_spec((SUBLANES, 1)), _const_spec((SUBLANES, 1)), _const_spec((1, dk)),
    ]
    out_shape = (jax.ShapeDtypeStruct((nb * seq, gw), BF16),
                 jax.ShapeDtypeStruct((nb, nh, dk, dk), F32))
    out_specs = (pl.BlockSpec((tl, gw), row),
                 pl.BlockSpec((None, nh, dk, dk), lambda b, i: (b, 0, 0, 0)))
    return pl.pallas_call(
        functools.partial(_gdn_prompt_kernel, tl=tl),
        grid=(nb, nt), in_specs=in_specs, out_specs=out_specs, out_shape=out_shape,
        scratch_shapes=[pltpu.VMEM((tl + SUBLANES, 3 * gw), F32), pltpu.VMEM((nh, dk, dk), F32)],
        compiler_params=_cparams(("parallel", "arbitrary")), name="gdn_prompt",
    )(qkv, z, ba, bat, cw, alr, dtr, alc, dtc, gn)


def _gdn_sample_pre_kernel(qkv_ref, hist_ref, ba_ref, cw_ref, alr_ref, dtr_ref, r_ref, intra_ref, *, nt):
    nh, dk = GDN_HEADS, GDN_DK
    gw = nh * dk
    nb = qkv_ref.shape[1]
    cw = cw_ref[...]
    nhist = GDN_CONV - 1
    xp = [hist_ref[j] for j in range(nhist)] + [qkv_ref[t] for t in range(nt)]
    ys = []
    for t in range(nt):
        y = cw[0:1, :] * xp[t]
        for j in range(1, GDN_CONV):
            y = y + cw[j:j + 1, :] * xp[t + j]
        ys.append(_silu(y))
    beta, gc = [], []
    for t in range(nt):
        ba = ba_ref[t]
        beta_c = jax.nn.sigmoid(ba)
        g_c = -jnp.exp(alr_ref[...]) * _softplus(ba + dtr_ref[...])
        beta.append([jnp.broadcast_to(beta_c[:, h:h + 1], (nb, dk)) for h in range(nh)])
        g_t = [jnp.broadcast_to(g_c[:, nh + h:nh + h + 1], (nb, dk)) for h in range(nh)]
        gc.append(g_t if t == 0 else [gc[t - 1][h] + g_t[h] for h in range(nh)])
    zrow = jnp.zeros((nb, dk), F32)
    for h in range(nh):
        q, k, v = [], [], []
        for t in range(nt):
            qt = ys[t][:, h * dk:(h + 1) * dk]
            kt = ys[t][:, gw + h * dk:gw + (h + 1) * dk]
            q.append(qt * lax.rsqrt(jnp.sum(qt * qt, axis=-1, keepdims=True) + EPS) * (dk ** -0.5))
            k.append(kt * lax.rsqrt(jnp.sum(kt * kt, axis=-1, keepdims=True) + EPS))
            v.append(ys[t][:, 2 * gw + h * dk:2 * gw + (h + 1) * dk])
        g = [gc[t][h] for t in range(nt)]
        us, ws = [], []
        for t in range(nt):
            b_t = beta[t][h]
            u_t = v[t] * b_t
            w_t = k[t] * b_t * jnp.exp(g[t])
            for s in range(t):
                m_ts = b_t * jnp.sum(k[t] * k[s], axis=-1, keepdims=True) * jnp.exp(g[t] - g[s])
                u_t = u_t - m_ts * us[s]
                w_t = w_t - m_ts * ws[s]
            us.append(u_t)
            ws.append(w_t)
        lane = slice(h * dk, (h + 1) * dk)
        for t in range(nt):
            r_ref[t, :, lane] = ws[t]
            r_ref[nt + t, :, lane] = q[t] * jnp.exp(g[t])
            r_ref[2 * nt + t, :, lane] = us[t]
            r_ref[3 * nt + t, :, lane] = k[t] * jnp.exp(g[nt - 1] - g[t])
            for s in range(nt):
                if s <= t:
                    intra_ref[t * nt + s, :, lane] = (jnp.sum(q[t] * k[s], axis=-1, keepdims=True)
                                                       * jnp.exp(g[t] - g[s]))
                else:
                    intra_ref[t * nt + s, :, lane] = zrow
        r_ref[4 * nt, :, lane] = jnp.exp(g[nt - 1])
        for r in range(4 * nt + 1, r_ref.shape[0]):
            r_ref[r, :, lane] = zrow


def _gdn_sample_state_kernel(r_ref, s_ref, ws_ref, sout_ref, *, nt, bb):
    nh = GDN_HEADS
    rows = lax.broadcasted_iota(jnp.int32, (2 * nt, GDN_DK), 0)
    for bi in range(bb):
        for h in range(nh):
            r = r_ref[bi, h]
            s_h = s_ref[bi, h]
            ws = _dot(r[0:2 * nt].astype(BF16), s_h.astype(BF16))
            vn = jnp.where(rows < nt, r[2 * nt:4 * nt] - ws, 0.0)
            kd = jnp.where(rows < nt, pltpu.roll(r[2 * nt:4 * nt], nt, axis=0), 0.0)
            ws_ref[bi, h] = ws
            sout_ref[bi, h] = s_h * r[4 * nt:4 * nt + 1] + _dot_tn(kd.astype(BF16), vn.astype(BF16))


def _gdn_sample_post_kernel(ws_ref, r_ref, intra_ref, z_ref, gn_ref, og_ref, *, nt):
    nh, dk = GDN_HEADS, GDN_DK
    for h in range(nh):
        lane = slice(h * dk, (h + 1) * dk)
        vn = [r_ref[2 * nt + t, :, lane] - ws_ref[t, :, lane] for t in range(nt)]
        for t in range(nt):
            o = ws_ref[nt + t, :, lane]
            for s in range(t + 1):
                o = o + intra_ref[t * nt + s, :, lane] * vn[s]
            o = o * lax.rsqrt(jnp.mean(o * o, axis=-1, keepdims=True) + EPS) * gn_ref[...]
            og_ref[t, :, lane] = (o * _silu(z_ref[t, :, lane])).astype(og_ref.dtype)


def _gdn_sample(qkv_tm, hist_tm, ba_tm, z_tm, state_s, cw, alr, dtr, gn, *, bb):
    nt, nb, _ = qkv_tm.shape
    nh, dk = GDN_HEADS, GDN_DK
    gw = nh * dk
    nr = 6 * nt
    full = lambda shape: pl.BlockSpec(shape, lambda *_: (0,) * len(shape))
    r_tm, intra = pl.pallas_call(
        functools.partial(_gdn_sample_pre_kernel, nt=nt),
        grid=(1,),
        in_specs=[full(qkv_tm.shape), full(hist_tm.shape), full(ba_tm.shape), full(cw.shape),
                  full(alr.shape), full(dtr.shape)],
        out_specs=(full((nr, nb, gw)), full((nt * nt, nb, gw))),
        out_shape=(jax.ShapeDtypeStruct((nr, nb, gw), F32), jax.ShapeDtypeStruct((nt * nt, nb, gw), F32)),
        compiler_params=_cparams(("arbitrary",)), name="gdn_sample_pre",
    )(qkv_tm, hist_tm, ba_tm, cw, alr, dtr)
    r_bm = jnp.transpose(r_tm.reshape(nr, nb, nh, dk), (1, 2, 0, 3))
    blk = lambda b: (b, 0, 0, 0)
    ws_bm, s_new = pl.pallas_call(
        functools.partial(_gdn_sample_state_kernel, nt=nt, bb=bb),
        grid=(nb // bb,),
        in_specs=[pl.BlockSpec((bb, nh, nr, dk), blk), pl.BlockSpec((bb, nh, dk, dk), blk)],
        out_specs=(pl.BlockSpec((bb, nh, 2 * nt, dk), blk), pl.BlockSpec((bb, nh, dk, dk), blk)),
        out_shape=(jax.ShapeDtypeStruct((nb, nh, 2 * nt, dk), F32), jax.ShapeDtypeStruct((nb, nh, dk, dk), F32)),
        compiler_params=_cparams(("parallel",)), name="gdn_sample_state",
    )(r_bm, state_s)
    ws_tm = jnp.transpose(ws_bm, (2, 0, 1, 3)).reshape(2 * nt, nb, gw)
    og = pl.pallas_call(
        functools.partial(_gdn_sample_post_kernel, nt=nt),
        grid=(1,),
        in_specs=[full(ws_tm.shape), full(r_tm.shape), full(intra.shape), full(z_tm.shape), full(gn.shape)],
        out_specs=full((nt, nb, gw)),
        out_shape=jax.ShapeDtypeStruct((nt, nb, gw), BF16),
        compiler_params=_cparams(("arbitrary",)), name="gdn_sample_post",
    )(ws_tm, r_tm, intra, z_tm, gn)
    return og, s_new


def _lambda(lq1_ref, lk1_ref, lq2_ref, lk2_ref, lam_init):
    s1 = jnp.sum(lq1_ref[...] * lk1_ref[...], axis=-1, keepdims=True)
    s2 = jnp.sum(lq2_ref[...] * lk2_ref[...], axis=-1, keepdims=True)
    return jnp.exp(s1) - jnp.exp(s2) + lam_init


def _subln(o, g_ref, lam_init):
    return o * lax.rsqrt(jnp.mean(o * o, axis=-1, keepdims=True) + EPS) * g_ref[...] * (1.0 - lam_init)


def _attn_prompt_kernel(q_ref, k_ref, v_ref, lq1_ref, lk1_ref, lq2_ref, lk2_ref, sg_ref, o_ref, *, tq, lam_init):
    qi = pl.program_id(2)
    dv, dqk = DIFF_DV, DIFF_DQK
    q = q_ref[...]
    lane = lax.broadcasted_iota(jnp.int32, (tq, dv), 1)
    zero = jnp.zeros_like(q)
    qs = (jnp.where(lane < dqk, q, zero), jnp.where(lane >= dqk, q, zero))

    def block(j, carry, masked):
        kb = k_ref[pl.ds(pl.multiple_of(j * tq, tq), tq), :]
        vb = v_ref[pl.ds(pl.multiple_of(j * tq, tq), tq), :]
        out = []
        for c in range(2):
            m_i, l_i, acc = carry[c]
            s = _dot_nt(qs[c], kb)
            if masked:
                ri = lax.broadcasted_iota(jnp.int32, (tq, tq), 0)
                ci = lax.broadcasted_iota(jnp.int32, (tq, tq), 1)
                s = jnp.where(ci <= ri, s, NEG)
            m_new = jnp.maximum(m_i, jnp.max(s, axis=-1, keepdims=True))
            alpha = jnp.exp(m_i - m_new)
            p = jnp.exp(s - m_new)
            l_new = alpha * l_i + jnp.sum(p, axis=-1, keepdims=True)
            acc_new = alpha * acc + _dot(p.astype(BF16), vb)
            out.append((m_new, l_new, acc_new))
        return tuple(out)

    init = tuple((jnp.full((tq, 1), NEG, F32), jnp.zeros((tq, 1), F32), jnp.zeros((tq, dv), F32)) for _ in range(2))
    carry = lax.fori_loop(0, qi, lambda j, cr: block(j, cr, False), init)
    carry = block(qi, carry, True)
    lam = _lambda(lq1_ref, lk1_ref, lq2_ref, lk2_ref, lam_init)
    o = carry[0][2] / carry[0][1] - lam * (carry[1][2] / carry[1][1])
    o_ref[...] = _subln(o, sg_ref, lam_init).astype(o_ref.dtype)


def _attn_prompt(qa, ka, va, lq1, lk1, lq2, lk2, sg, *, nb, seq, tq, lam_init):
    nh, dv = DIFF_HEADS, DIFF_DV
    nq = seq // tq
    qa3, ka3, va3 = (a.reshape(nb, seq, nh * dv) for a in (qa, ka, va))
    vec = _const_spec((1, DIFF_DQK))
    out = pl.pallas_call(
        functools.partial(_attn_prompt_kernel, tq=tq, lam_init=lam_init),
        grid=(nb, nh, nq),
        in_specs=[pl.BlockSpec((None, tq, dv), lambda b, h, i: (b, i, h)),
                  pl.BlockSpec((None, seq, dv), lambda b, h, i: (b, 0, h)),
                  pl.BlockSpec((None, seq, dv), lambda b, h, i: (b, 0, h)),
                  vec, vec, vec, vec, _const_spec((1, dv))],
        out_specs=pl.BlockSpec((None, tq, dv), lambda b, h, i: (b, i, h)),
        out_shape=jax.ShapeDtypeStruct((nb, seq, nh * dv), BF16),
        compiler_params=_cparams(("parallel", "parallel", "arbitrary")), name="attn_prompt",
    )(qa3, ka3, va3, lq1, lk1, lq2, lk2, sg)
    return out.reshape(nb * seq, nh * dv)


def _attn_sample_kernel(pt_ref, q_ref, kn_ref, vn_ref, *rest, n_pages, nt, lam_init):
    k_refs = rest[:n_pages]
    v_refs = rest[n_pages:2 * n_pages]
    lq1_ref, lk1_ref, lq2_ref, lk2_ref, sg_ref, o_ref = rest[2 * n_pages:]
    del pt_ref
    nh, dv, dqk = DIFF_HEADS, DIFF_DV, DIFF_DQK
    page = k_refs[0].shape[0]
    nr = 2 * nt
    lam = _lambda(lq1_ref, lk1_ref, lq2_ref, lk2_ref, lam_init)
    row = lax.broadcasted_iota(jnp.int32, (nr, dv), 0)
    lane = lax.broadcasted_iota(jnp.int32, (nr, dv), 1)
    first = row < nt
    keep = jnp.logical_xor(lane >= dqk, first)
    tpos = jnp.where(first, row, row - nt)
    new_ok = (lane < nt) & (lane <= tpos)
    zpad = jnp.zeros((page - nr, dv), F32)
    for h in range(nh):
        ls = slice(h * dv, (h + 1) * dv)
        qh = q_ref[:, ls]
        qz = jnp.where(keep, qh, jnp.zeros_like(qh))
        s_list = [_dot_nt(qz, k_refs[j][:, ls].astype(BF16)) for j in range(n_pages)]
        knew = jnp.concatenate([kn_ref[:, ls], zpad], axis=0)
        s_new = jnp.where(new_ok, _dot_nt(qz, knew.astype(BF16)), NEG)
        s_list.append(s_new)
        m = s_list[0]
        for s in s_list[1:]:
            m = jnp.maximum(m, s)
        m = jnp.max(m, axis=-1, keepdims=True)
        l = jnp.zeros((nr, 1), F32)
        acc = jnp.zeros((nr, dv), F32)
        for j, s in enumerate(s_list):
            p = jnp.exp(s - m)
            l = l + jnp.sum(p, axis=-1, keepdims=True)
            if j < n_pages:
                vj = v_refs[j][:, ls]
            else:
                vj = jnp.concatenate([vn_ref[:, ls], zpad], axis=0)
            acc = acc + _dot(p.astype(BF16), vj.astype(BF16))
        o2 = acc / l
        o = o2 - lam * pltpu.roll(o2, nt, axis=0)
        o_ref[:, ls] = _subln(o, sg_ref, lam_init)


def _attn_sample(page_table, q2, kn8, vn8, cache_k, cache_v, lq1, lk1, lq2, lk2, sg, *, nt, lam_init):
    nb, n_pages = page_table.shape
    nh, dv = DIFF_HEADS, DIFF_DV
    page = cache_k.shape[1]
    wd = nh * dv
    ck = cache_k.reshape(cache_k.shape[0], page, wd)
    cv = cache_v.reshape(cache_v.shape[0], page, wd)
    nr = 2 * nt
    small = pl.BlockSpec((None, nr, wd), lambda b, pt: (b, 0, 0))

    def page_spec(j):
        return pl.BlockSpec((None, page, wd), lambda b, pt: (pt[b, j], 0, 0))

    vec = pl.BlockSpec((1, DIFF_DQK), lambda b, pt: (0, 0))
    grid_spec = pltpu.PrefetchScalarGridSpec(
        num_scalar_prefetch=1, grid=(nb,),
        in_specs=[small, small, small] + [page_spec(j) for j in range(n_pages)] * 2
        + [vec, vec, vec, vec, pl.BlockSpec((1, dv), lambda b, pt: (0, 0))],
        out_specs=small)
    return pl.pallas_call(
        functools.partial(_attn_sample_kernel, n_pages=n_pages, nt=nt, lam_init=lam_init),
        grid_spec=grid_spec, out_shape=jax.ShapeDtypeStruct((nb, nr, wd), F32),
        compiler_params=_cparams(("parallel",)), name="attn_sample",
    )(page_table, q2, kn8, vn8, *([ck] * n_pages), *([cv] * n_pages), lq1, lk1, lq2, lk2, sg)


def _post_kernel(x_ref, og_ref, od_ref, hist_ref, wo_ref, gf_ref, wu_ref, fcw_ref, fcb_ref, wd_ref,
                 y_ref, tail_ref, ubuf, *, tm, shift, hr, dff):
    i = pl.program_id(1)
    gw = og_ref.shape[1]

    @pl.when(i == 0)
    def _():
        ubuf[0:hr, :] = hist_ref[...]

    hres = x_ref[...] + _dot(og_ref[...], wo_ref[0:gw, :]) + _dot(od_ref[...], wo_ref[gw:, :])
    hn = hres * lax.rsqrt(jnp.mean(hres * hres, axis=-1, keepdims=True) + EPS) * gf_ref[...]
    ubuf[hr:hr + tm, :] = _dot(hn.astype(BF16), wu_ref[...])
    fcw = fcw_ref[...]
    u = fcw[FFN_CONV - 1:FFN_CONV, :] * ubuf[hr:hr + tm, :] + fcb_ref[...]
    for j in range(1, FFN_CONV):
        u = u + fcw[FFN_CONV - 1 - j:FFN_CONV - j, :] * ubuf[hr - j * shift:hr - j * shift + tm, :]
    tail = ubuf[tm:tm + hr, :]
    tail_ref[...] = tail
    ubuf[0:hr, :] = tail
    act = (_silu(u[:, :dff]) * u[:, dff:]).astype(BF16)
    y_ref[...] = hres + _dot(act, wd_ref[...])


def _post(x2d, og, od, hist, w_out, g_ffn, w_up, fcw, fcb, w_down, *, nseq, tm, shift, hr):
    n, d = x2d.shape
    gw = og.shape[1]
    dff2 = w_up.shape[1]
    nt = n // (nseq * tm)
    row = lambda b, i: (b * nt + i, 0)
    in_specs = [
        pl.BlockSpec((tm, d), row), pl.BlockSpec((tm, gw), row), pl.BlockSpec((tm, od.shape[1]), row),
        pl.BlockSpec((None, hr, dff2), lambda b, i: (b, 0, 0)),
        _const_spec(w_out.shape), _const_spec((1, d)), _const_spec(w_up.shape),
        _const_spec((FFN_CONV, dff2)), _const_spec((1, dff2)), _const_spec(w_down.shape),
    ]
    out_shape = (jax.ShapeDtypeStruct((n, d), F32), jax.ShapeDtypeStruct((nseq, hr, dff2), F32))
    out_specs = (pl.BlockSpec((tm, d), row), pl.BlockSpec((None, hr, dff2), lambda b, i: (b, 0, 0)))
    return pl.pallas_call(
        functools.partial(_post_kernel, tm=tm, shift=shift, hr=hr, dff=dff2 // 2),
        grid=(nseq, nt), in_specs=in_specs, out_specs=out_specs, out_shape=out_shape,
        scratch_shapes=[pltpu.VMEM((hr + tm, dff2), F32)],
        compiler_params=_cparams(("parallel", "arbitrary")), name="post",
    )(x2d, og, od, hist, w_out, g_ffn, w_up, fcw, fcb, w_down)


def _lane_pad(vec, offset):
    out = jnp.zeros((1, LANES), F32)
    return lax.dynamic_update_slice(out, vec.reshape(1, -1).astype(F32), (0, offset))


def _layer(l, x_prompt, x_sample, state_gdn_conv, state_gdn_s, cache_k, cache_v, page_table, state_ffn_conv, wl):
    (attn_norm_g, w_in, gdn_conv_w, gdn_a_log, gdn_dt_bias, gdn_out_norm_g, diff_q_norm_g, diff_k_norm_g,
     lq1, lk1, lq2, lk2, diff_subln_g, w_out, ffn_norm_g, w_up, ffn_conv_w, ffn_conv_b, w_down) = wl
    nbp, seq, d = x_prompt.shape
    nbs, nts, _ = x_sample.shape
    nh, dk = GDN_HEADS, GDN_DK
    gw = nh * dk
    dw = DIFF_HEADS * DIFF_DV
    lam_init = 0.8 - 0.6 * math.exp(-0.3 * l)

    c_b = 4 * gw
    c_d = c_b + 2 * nh
    w_ba = w_in[:, c_b:c_d]
    w_main = jnp.concatenate([w_in[:, :c_b], w_in[:, c_d:], w_ba,
                              jnp.zeros((d, LANES - 2 * nh), w_in.dtype)], axis=1).astype(BF16)
    w_ba_t = jnp.transpose(w_ba).astype(BF16)
    reps = dw // DIFF_DQK
    gq_t = jnp.tile(diff_q_norm_g.reshape(1, -1), (1, reps))
    gk_t = jnp.tile(diff_k_norm_g.reshape(1, -1), (1, reps))
    gid = jnp.arange(dw) // DIFF_DQK
    grp = jnp.where(gid[:, None] == gid[None, :], 1.0 / DIFF_DQK, 0.0).astype(BF16)
    g_attn = attn_norm_g.reshape(1, d)
    alr, dtr = _lane_pad(gdn_a_log, nh), _lane_pad(gdn_dt_bias, nh)
    alc = jnp.transpose(alr[:, :SUBLANES])
    dtc = jnp.transpose(dtr[:, :SUBLANES])
    gn = gdn_out_norm_g.reshape(1, dk)
    vecs = [v.reshape(1, -1) for v in (lq1, lk1, lq2, lk2)]
    sg = diff_subln_g.reshape(1, -1)
    w_out_b, w_up_b, w_down_b = w_out.astype(BF16), w_up.astype(BF16), w_down.astype(BF16)
    g_ffn = ffn_norm_g.reshape(1, d)
    fcb = ffn_conv_b.reshape(1, -1)
    dff2 = w_up.shape[1]

    xp2 = x_prompt.reshape(nbp * seq, d)
    qkv, z, ba, bat, knew, vnew, qa, ka, va = _inproj(xp2, g_attn, w_main, w_ba_t, gq_t, gk_t, grp,
                                                      tm=256, gw=gw, dw=dw)
    og, s_prompt = _gdn_prompt(qkv, z, ba, bat, gdn_conv_w, alr, dtr, alc, dtc, gn, nb=nbp, seq=seq, tl=256)
    od = _attn_prompt(qa, ka, va, *vecs, sg, nb=nbp, seq=seq, tq=512, lam_init=lam_init)
    hr_p = SUBLANES
    y_p, tail_p = _post(xp2, og, od, jnp.zeros((nbp, hr_p, dff2), F32), w_out_b, g_ffn, w_up_b,
                        ffn_conv_w, fcb, w_down_b, nseq=nbp, tm=256, shift=1, hr=hr_p)
    out_p = (y_p.reshape(nbp, seq, d),
             qkv.reshape(nbp, seq, 3 * gw)[:, seq - (GDN_CONV - 1):, :],
             s_prompt,
             knew.reshape(nbp, seq, DIFF_HEADS, DIFF_DV),
             vnew.reshape(nbp, seq, DIFF_HEADS, DIFF_DV),
             tail_p[:, hr_p - (FFN_CONV - 1):, :])

    xs2 = jnp.transpose(x_sample, (1, 0, 2)).reshape(nts * nbs, d)
    qkv, z, ba, bat, knew, vnew, qa, ka, va = _inproj(xs2, g_attn, w_main, w_ba_t, gq_t, gk_t, grp,
                                                      tm=min(256, nts * nbs), gw=gw, dw=dw)
    del bat
    qkv_tm = qkv.reshape(nts, nbs, 3 * gw)
    hist_tm = jnp.transpose(state_gdn_conv, (1, 0, 2))
    og_tm, s_sample = _gdn_sample(qkv_tm, hist_tm, ba.reshape(nts, nbs, LANES), z.reshape(nts, nbs, gw),
                                  state_gdn_s, gdn_conv_w, alr, dtr, gn, bb=8)
    conv_all = jnp.concatenate([hist_tm, qkv_tm], axis=0)
    conv_s = jnp.transpose(conv_all[nts:], (1, 0, 2))
    to_bm = lambda a: jnp.transpose(a.reshape(nts, nbs, -1), (1, 0, 2))
    q_bm = to_bm(qa)
    q2 = jnp.concatenate([q_bm, q_bm], axis=1)
    pad = jnp.zeros((nbs, nts, dw), F32)
    kn_bm, vn_bm = to_bm(knew), to_bm(vnew)
    od_bm = _attn_sample(page_table, q2, jnp.concatenate([kn_bm, pad], axis=1),
                         jnp.concatenate([vn_bm, pad], axis=1), cache_k, cache_v, *vecs, sg,
                         nt=nts, lam_init=lam_init)
    od_tm = jnp.transpose(od_bm[:, :nts, :], (1, 0, 2)).reshape(nts * nbs, dw).astype(BF16)
    hr_s = (FFN_CONV - 1) * nbs
    hist_f = jnp.transpose(state_ffn_conv, (1, 0, 2)).reshape(1, hr_s, dff2)
    y_s, tail_s = _post(xs2, og_tm.reshape(nts * nbs, gw), od_tm, hist_f, w_out_b, g_ffn, w_up_b,
                        ffn_conv_w, fcb, w_down_b, nseq=1, tm=nbs, shift=nbs, hr=hr_s)
    out_s = (jnp.transpose(y_s.reshape(nts, nbs, d), (1, 0, 2)),
             conv_s,
             s_sample,
             kn_bm.reshape(nbs, nts, DIFF_HEADS, DIFF_DV),
             vn_bm.reshape(nbs, nts, DIFF_HEADS, DIFF_DV),
             jnp.transpose(tail_s.reshape(FFN_CONV - 1, nbs, dff2), (1, 0, 2)))
    return out_p, out_s


def kernel(x_prompt, x_sample, state_gdn_conv, state_gdn_S, cache_k, cache_v, page_table, state_ffn_conv, attn_norm_g, w_in, gdn_conv_w, gdn_A_log, gdn_dt_bias, gdn_out_norm_g, diff_q_norm_g, diff_k_norm_g, diff_lambda_q1, diff_lambda_k1, diff_lambda_q2, diff_lambda_k2, diff_subln_g, w_out, ffn_norm_g, w_up, ffn_conv_w, ffn_conv_b, w_down):
    depth = w_in.shape[0]
    hp, hs = x_prompt, x_sample
    outs_p, outs_s = [], []
    for l in range(depth):
        wl = (attn_norm_g[l], w_in[l], gdn_conv_w[l], gdn_A_log[l], gdn_dt_bias[l], gdn_out_norm_g[l],
              diff_q_norm_g[l], diff_k_norm_g[l], diff_lambda_q1[l], diff_lambda_k1[l], diff_lambda_q2[l],
              diff_lambda_k2[l], diff_subln_g[l], w_out[l], ffn_norm_g[l], w_up[l], ffn_conv_w[l],
              ffn_conv_b[l], w_down[l])
        out_p, out_s = _layer(l, hp, hs, state_gdn_conv[l], state_gdn_S[l], cache_k[l], cache_v[l],
                              page_table, state_ffn_conv[l], wl)
        hp, hs = out_p[0], out_s[0]
        outs_p.append(out_p[1:])
        outs_s.append(out_s[1:])
    stack = lambda outs, i: jnp.stack([o[i] for o in outs])
    return (hp, hs) + tuple(stack(outs_p, i) for i in range(5)) + tuple(stack(outs_s, i) for i in range(5))
```

```python
import functools
import math

import jax
import jax.numpy as jnp
from jax import lax
from jax.experimental import pallas as pl
from jax.experimental.pallas import tpu as pltpu

F32 = jnp.float32
BF16 = jnp.bfloat16
EPS = 1e-6
NEG = -1e30

GDN_HEADS = 4
GDN_DK = 128
GDN_CONV = 4
GDN_CHUNK = 64
DIFF_HEADS = 4
DIFF_DV = 128
DIFF_DQK = 64
FFN_CONV = 3
LANES = 128
SUBLANES = 8
VMEM_LIMIT = 56 * 1024 * 1024


def _cparams(sem):
    return pltpu.CompilerParams(dimension_semantics=sem, vmem_limit_bytes=VMEM_LIMIT)


def _const_spec(shape):
    nd = len(shape)
    return pl.BlockSpec(shape, lambda *_: (0,) * nd, pipeline_mode=pl.Buffered(1))


def _dot(a, b):
    return jnp.dot(a, b, preferred_element_type=F32)


def _dot_nt(a, b):
    return lax.dot_general(a, b, (((1,), (1,)), ((), ())), preferred_element_type=F32)


def _dot_tn(a, b):
    return lax.dot_general(a, b, (((0,), (0,)), ((), ())), preferred_element_type=F32)


def _softplus(x):
    return jnp.maximum(x, 0.0) + jnp.log1p(jnp.exp(-jnp.abs(x)))


def _silu(x):
    return x * jax.nn.sigmoid(x)


def _split3(x):
    hi = x.astype(BF16)
    r = x - hi.astype(F32)
    mid = r.astype(BF16)
    lo = (r - mid.astype(F32)).astype(BF16)
    return hi, mid, lo


def _inproj_kernel(x_ref, g_ref, wm_ref, wt_ref, gq_ref, gk_ref, grp_ref,
                   qkv_ref, z_ref, ba_ref, bat_ref, knew_ref, vnew_ref, qa_ref, ka_ref, vt_ref,
                   *, gw, dw):
    x = x_ref[...]
    xn = x * lax.rsqrt(jnp.mean(x * x, axis=-1, keepdims=True) + EPS) * g_ref[...]
    xb = xn.astype(BF16)
    proj = _dot(xb, wm_ref[...])
    qkv_ref[...] = proj[:, :3 * gw]
    z_ref[...] = proj[:, 3 * gw:4 * gw]
    c0 = 4 * gw
    ba_ref[...] = proj[:, c0 + 3 * dw:c0 + 3 * dw + LANES]
    tr = _dot_nt(wt_ref[...], xb)
    bat_ref[...] = tr[:SUBLANES]
    vt_ref[...] = tr[SUBLANES:].astype(BF16)
    dq = proj[:, c0:c0 + dw]
    dk = proj[:, c0 + dw:c0 + 2 * dw]
    dv = proj[:, c0 + 2 * dw:c0 + 3 * dw]
    grp = grp_ref[...]
    msq = _dot((dq * dq).astype(BF16), grp)
    msk = _dot((dk * dk).astype(BF16), grp)
    dqn = dq * lax.rsqrt(msq + EPS) * gq_ref[...]
    dkn = dk * lax.rsqrt(msk + EPS) * gk_ref[...]
    knew_ref[...] = dkn
    vnew_ref[...] = dv
    qa_ref[...] = (dqn * (DIFF_DQK ** -0.5 * math.log2(math.e))).astype(BF16)
    ka_ref[...] = dkn.astype(BF16)


def _inproj(x2d, g_attn, w_main, w_t, gq_t, gk_t, grp, *, tm, gw, dw):
    n, d = x2d.shape
    ncols = w_main.shape[1]
    row = lambda i: (i, 0)
    out_shape = (
        jax.ShapeDtypeStruct((n, 3 * gw), F32),
        jax.ShapeDtypeStruct((n, gw), F32),
        jax.ShapeDtypeStruct((n, LANES), F32),
        jax.ShapeDtypeStruct((SUBLANES, n), F32),
        jax.ShapeDtypeStruct((n, dw), F32),
        jax.ShapeDtypeStruct((n, dw), F32),
        jax.ShapeDtypeStruct((n, dw), BF16),
        jax.ShapeDtypeStruct((n, dw), BF16),
        jax.ShapeDtypeStruct((n // tm, dw, tm), BF16),
    )
    out_specs = (
        pl.BlockSpec((tm, 3 * gw), row), pl.BlockSpec((tm, gw), row), pl.BlockSpec((tm, LANES), row),
        pl.BlockSpec((SUBLANES, tm), lambda i: (0, i)),
        pl.BlockSpec((tm, dw), row), pl.BlockSpec((tm, dw), row),
        pl.BlockSpec((tm, dw), row), pl.BlockSpec((tm, dw), row),
        pl.BlockSpec((None, dw, tm), lambda i: (i, 0, 0)),
    )
    in_specs = [
        pl.BlockSpec((tm, d), row), _const_spec((1, d)), _const_spec((d, ncols)),
        _const_spec(w_t.shape), _const_spec((1, dw)), _const_spec((1, dw)), _const_spec((dw, dw)),
    ]
    return pl.pallas_call(
        functools.partial(_inproj_kernel, gw=gw, dw=dw),
        grid=(n // tm,), in_specs=in_specs, out_specs=out_specs, out_shape=out_shape,
        compiler_params=_cparams(("parallel",)), name="inproj",
    )(x2d, g_attn, w_main, w_t, gq_t, gk_t, grp)


def _gdn_prompt_kernel(qkv_ref, z_ref, ba_ref, bat_ref, cw_ref, alr_ref, dtr_ref, alc_ref, dtc_ref, gn_ref,
                       og_ref, sout_ref, xbuf, s_scr, *, tl):
    i = pl.program_id(1)
    nh, dk, c = GDN_HEADS, GDN_DK, GDN_CHUNK
    gw = nh * dk
    hr = SUBLANES

    @pl.when(i == 0)
    def _():
        xbuf[0:hr, :] = jnp.zeros((hr, 3 * gw), F32)
        s_scr[...] = jnp.zeros_like(s_scr)

    xbuf[hr:hr + tl, :] = qkv_ref[...]
    cw = cw_ref[...]
    y = cw[GDN_CONV - 1:GDN_CONV, :] * xbuf[hr:hr + tl, :]
    for j in range(1, GDN_CONV):
        y = y + cw[GDN_CONV - 1 - j:GDN_CONV - j, :] * xbuf[hr - j:hr - j + tl, :]
    xbuf[0:hr, :] = xbuf[tl:tl + hr, :]
    y = _silu(y)

    ba = ba_ref[...]
    beta_c = jax.nn.sigmoid(ba)
    g_c = -jnp.exp(alr_ref[...]) * _softplus(ba + dtr_ref[...])
    beta_rep = [jnp.broadcast_to(beta_c[:, h:h + 1], (tl, dk)) for h in range(nh)]
    g_rep = jnp.concatenate([jnp.broadcast_to(g_c[:, nh + h:nh + h + 1], (tl, dk)) for h in range(nh)], axis=1)
    bat = bat_ref[...]
    g_r = -jnp.exp(alc_ref[...]) * _softplus(bat + dtc_ref[...])

    ii = lax.broadcasted_iota(jnp.int32, (tl, tl), 0)
    jj = lax.broadcasted_iota(jnp.int32, (tl, tl), 1)
    same = lax.shift_right_logical(ii, 6) == lax.shift_right_logical(jj, 6)
    tri = jnp.where(same & (ii >= jj), 1.0, 0.0).astype(BF16)
    triu = jnp.where(same & (ii <= jj), 1.0, 0.0).astype(BF16)
    gc_rep = sum(_dot(tri, p) for p in _split3(g_rep))
    gc_r = sum(_dot(p, triu) for p in _split3(g_r))

    pr = 2 * c
    pi = lax.broadcasted_iota(jnp.int32, (pr, pr), 0)
    pj = lax.broadcasted_iota(jnp.int32, (pr, pr), 1)
    psame = lax.shift_right_logical(pi, 6) == lax.shift_right_logical(pj, 6)
    lower = psame & (pi >= pj)
    strict = psame & (pi > pj)
    eye = jnp.where(pi == pj, 1.0, 0.0).astype(F32)
    zeros_c = jnp.zeros((c, dk), F32)

    npair = tl // pr
    hp = [(h, p) for h in range(nh) for p in range(npair)]
    qn, kn, kb, vb = [], [], [], []
    for h in range(nh):
        q = y[:, h * dk:(h + 1) * dk]
        k = y[:, gw + h * dk:gw + (h + 1) * dk]
        v = y[:, 2 * gw + h * dk:2 * gw + (h + 1) * dk]
        qn.append(q * lax.rsqrt(jnp.sum(q * q, axis=-1, keepdims=True) + EPS) * (dk ** -0.5))
        kn.append(k * lax.rsqrt(jnp.sum(k * k, axis=-1, keepdims=True) + EPS))
        kb.append(kn[h] * beta_rep[h])
        vb.append(v * beta_rep[h])
    rows = lambda arr, p: arr[p * pr:(p + 1) * pr]
    gcp = {(h, p): rows(gc_rep[:, h * dk:(h + 1) * dk], p) for h, p in hp}
    dec = {(h, p): jnp.exp(jnp.where(lower, gcp[h, p] - gc_r[nh + h:nh + h + 1, p * pr:(p + 1) * pr], NEG))
           for h, p in hp}
    kpb = {(h, p): rows(kn[h], p).astype(BF16) for h, p in hp}
    m = {x: -jnp.where(strict, _dot_nt(rows(kb[x[0]], x[1]).astype(BF16), kpb[x]) * dec[x], 0.0) for x in hp}
    a = {x: eye + m[x] for x in hp}
    pw = {x: _dot(m[x].astype(BF16), m[x].astype(BF16)) for x in hp}
    for it in range(1, 6):
        for x in hp:
            pwb = pw[x].astype(BF16)
            if it < 5:
                res = _dot(pwb, jnp.concatenate([a[x].astype(BF16), pwb], axis=1))
                a[x] = a[x] + res[:, :pr]
                pw[x] = res[:, pr:]
            else:
                a[x] = a[x] + _dot(pwb, a[x].astype(BF16))
    sol = {(h, p): _dot(a[h, p].astype(BF16),
                        jnp.concatenate([rows(vb[h], p), rows(kb[h], p) * jnp.exp(gcp[h, p])], axis=1).astype(BF16))
           for h, p in hp}
    qk = {(h, p): (_dot_nt(rows(qn[h], p).astype(BF16), kpb[h, p]) * dec[h, p]).astype(BF16) for h, p in hp}
    qg = {(h, p): rows(qn[h], p) * jnp.exp(gcp[h, p]) for h, p in hp}

    s_cur = [s_scr[h] for h in range(nh)]
    o_rows = [[] for _ in range(nh)]
    for p in range(npair):
        for cc in range(2):
            c0 = cc * c
            for h in range(nh):
                g_c0 = gcp[h, p][c0:c0 + c]
                glast = g_c0[c - 1:c, :]
                kd = rows(kn[h], p)[c0:c0 + c] * jnp.exp(glast - g_c0)
                u, w = sol[h, p][c0:c0 + c, :dk], sol[h, p][c0:c0 + c, dk:]
                wq = jnp.concatenate([w, qg[h, p][c0:c0 + c]], axis=0)
                ws = _dot(wq.astype(BF16), s_cur[h].astype(BF16))
                v_new = u - ws[:c]
                vn_pad = jnp.concatenate([v_new, zeros_c] if cc == 0 else [zeros_c, v_new], axis=0)
                o_rows[h].append(ws[c:] + _dot(qk[h, p][c0:c0 + c], vn_pad.astype(BF16)))
                s_cur[h] = s_cur[h] * jnp.exp(glast) + _dot_tn(kd.astype(BF16), v_new.astype(BF16))
    for h in range(nh):
        s_scr[h] = s_cur[h]
        o = jnp.concatenate(o_rows[h], axis=0)
        o = o * lax.rsqrt(jnp.mean(o * o, axis=-1, keepdims=True) + EPS) * gn_ref[...]
        og_ref[:, h * dk:(h + 1) * dk] = (o * _silu(z_ref[:, h * dk:(h + 1) * dk])).astype(og_ref.dtype)

    @pl.when(i == pl.num_programs(1) - 1)
    def _():
        sout_ref[...] = s_scr[...]


def _gdn_prompt(qkv, z, ba, bat, cw, alr, dtr, alc, dtc, gn, *, nb, seq, tl):
    nh, dk = GDN_HEADS, GDN_DK
    gw = nh * dk
    nt = seq // tl
    row = lambda b, i: (b * nt + i, 0)
    in_specs = [
        pl.BlockSpec((tl, 3 * gw), row), pl.BlockSpec((tl, gw), row), pl.BlockSpec((tl, LANES), row),
        pl.BlockSpec((SUBLANES, tl), lambda b, i: (0, b * nt + i)),
        _const_spec((GDN_CONV, 3 * gw)), _const_spec((1, LANES)), _const_spec((1, LANES)),
        _const_spec((SUBLANES, 1)), _const_spec((SUBLANES, 1)), _const_spec((1, dk)),
    ]
    out_shape = (jax.ShapeDtypeStruct((nb * seq, gw), BF16),
                 jax.ShapeDtypeStruct((nb, nh, dk, dk), F32))
    out_specs = (pl.BlockSpec((tl, gw), row),
                 pl.BlockSpec((None, nh, dk, dk), lambda b, i: (b, 0, 0, 0)))
    return pl.pallas_call(
        functools.partial(_gdn_prompt_kernel, tl=tl),
        grid=(nb, nt), in_specs=in_specs, out_specs=out_specs, out_shape=out_shape,
        scratch_shapes=[pltpu.VMEM((tl + SUBLANES, 3 * gw), F32), pltpu.VMEM((nh, dk, dk), F32)],
        compiler_params=_cparams(("parallel", "arbitrary")), name="gdn_prompt",
    )(qkv, z, ba, bat, cw, alr, dtr, alc, dtc, gn)


def _gdn_sample_pre_kernel(qkv_ref, hist_ref, ba_ref, cw_ref, alr_ref, dtr_ref, r_ref, intra_ref, *, nt):
    nh, dk = GDN_HEADS, GDN_DK
    gw = nh * dk
    nb = qkv_ref.shape[1]
    cw = cw_ref[...]
    nhist = GDN_CONV - 1
    xp = [hist_ref[j] for j in range(nhist)] + [qkv_ref[t] for t in range(nt)]
    ys = []
    for t in range(nt):
        y = cw[0:1, :] * xp[t]
        for j in range(1, GDN_CONV):
            y = y + cw[j:j + 1, :] * xp[t + j]
        ys.append(_silu(y))
    beta, gc = [], []
    for t in range(nt):
        ba = ba_ref[t]
        beta_c = jax.nn.sigmoid(ba)
        g_c = -jnp.exp(alr_ref[...]) * _softplus(ba + dtr_ref[...])
        beta.append([jnp.broadcast_to(beta_c[:, h:h + 1], (nb, dk)) for h in range(nh)])
        g_t = [jnp.broadcast_to(g_c[:, nh + h:nh + h + 1], (nb, dk)) for h in range(nh)]
        gc.append(g_t if t == 0 else [gc[t - 1][h] + g_t[h] for h in range(nh)])
    zrow = jnp.zeros((nb, dk), F32)
    for h in range(nh):
        q, k, v = [], [], []
        for t in range(nt):
            qt = ys[t][:, h * dk:(h + 1) * dk]
            kt = ys[t][:, gw + h * dk:gw + (h + 1) * dk]
            q.append(qt * lax.rsqrt(jnp.sum(qt * qt, axis=-1, keepdims=True) + EPS) * (dk ** -0.5))
            k.append(kt * lax.rsqrt(jnp.sum(kt * kt, axis=-1, keepdims=True) + EPS))
            v.append(ys[t][:, 2 * gw + h * dk:2 * gw + (h + 1) * dk])
        g = [gc[t][h] for t in range(nt)]
        us, ws = [], []
        for t in range(nt):
            b_t = beta[t][h]
            u_t = v[t] * b_t
            w_t = k[t] * b_t * jnp.exp(g[t])
            for s in range(t):
                m_ts = b_t * jnp.sum(k[t] * k[s], axis=-1, keepdims=True) * jnp.exp(g[t] - g[s])
                u_t = u_t - m_ts * us[s]
                w_t = w_t - m_ts * ws[s]
            us.append(u_t)
            ws.append(w_t)
        lane = slice(h * dk, (h + 1) * dk)
        for t in range(nt):
            r_ref[t, :, lane] = ws[t]
            r_ref[nt + t, :, lane] = q[t] * jnp.exp(g[t])
            r_ref[2 * nt + t, :, lane] = us[t]
            r_ref[3 * nt + t, :, lane] = k[t] * jnp.exp(g[nt - 1] - g[t])
            for s in range(nt):
                if s <= t:
                    intra_ref[t * nt + s, :, lane] = (jnp.sum(q[t] * k[s], axis=-1, keepdims=True)
                                                       * jnp.exp(g[t] - g[s]))
                else:
                    intra_ref[t * nt + s, :, lane] = zrow
        r_ref[4 * nt, :, lane] = jnp.exp(g[nt - 1])
        for r in range(4 * nt + 1, r_ref.shape[0]):
            r_ref[r, :, lane] = zrow


def _gdn_sample_state_kernel(r_ref, s_ref, ws_ref, sout_ref, *, nt, bb):
    nh = GDN_HEADS
    rows = lax.broadcasted_iota(jnp.int32, (2 * nt, GDN_DK), 0)
    for bi in range(bb):
        for h in range(nh):
            r = r_ref[bi, h]
            s_h = s_ref[bi, h]
            ws = _dot(r[0:2 * nt].astype(BF16), s_h.astype(BF16))
            vn = jnp.where(rows < nt, r[2 * nt:4 * nt] - ws, 0.0)
            kd = jnp.where(rows < nt, pltpu.roll(r[2 * nt:4 * nt], nt, axis=0), 0.0)
            ws_ref[bi, h] = ws
            sout_ref[bi, h] = s_h * r[4 * nt:4 * nt + 1] + _dot_tn(kd.astype(BF16), vn.astype(BF16))


def _gdn_sample_post_kernel(ws_ref, r_ref, intra_ref, z_ref, gn_ref, og_ref, *, nt):
    nh, dk = GDN_HEADS, GDN_DK
    for h in range(nh):
        lane = slice(h * dk, (h + 1) * dk)
        vn = [r_ref[2 * nt + t, :, lane] - ws_ref[t, :, lane] for t in range(nt)]
        for t in range(nt):
            o = ws_ref[nt + t, :, lane]
            for s in range(t + 1):
                o = o + intra_ref[t * nt + s, :, lane] * vn[s]
            o = o * lax.rsqrt(jnp.mean(o * o, axis=-1, keepdims=True) + EPS) * gn_ref[...]
            og_ref[t, :, lane] = (o * _silu(z_ref[t, :, lane])).astype(og_ref.dtype)


def _gdn_sample(qkv_tm, hist_tm, ba_tm, z_tm, state_s, cw, alr, dtr, gn, *, bb):
    nt, nb, _ = qkv_tm.shape
    nh, dk = GDN_HEADS, GDN_DK
    gw = nh * dk
    nr = 6 * nt
    full = lambda shape: pl.BlockSpec(shape, lambda *_: (0,) * len(shape))
    r_tm, intra = pl.pallas_call(
        functools.partial(_gdn_sample_pre_kernel, nt=nt),
        grid=(1,),
        in_specs=[full(qkv_tm.shape), full(hist_tm.shape), full(ba_tm.shape), full(cw.shape),
                  full(alr.shape), full(dtr.shape)],
        out_specs=(full((nr, nb, gw)), full((nt * nt, nb, gw))),
        out_shape=(jax.ShapeDtypeStruct((nr, nb, gw), F32), jax.ShapeDtypeStruct((nt * nt, nb, gw), F32)),
        compiler_params=_cparams(("arbitrary",)), name="gdn_sample_pre",
    )(qkv_tm, hist_tm, ba_tm, cw, alr, dtr)
    r_bm = jnp.transpose(r_tm.reshape(nr, nb, nh, dk), (1, 2, 0, 3))
    blk = lambda b: (b, 0, 0, 0)
    ws_bm, s_new = pl.pallas_call(
        functools.partial(_gdn_sample_state_kernel, nt=nt, bb=bb),
        grid=(nb // bb,),
        in_specs=[pl.BlockSpec((bb, nh, nr, dk), blk), pl.BlockSpec((bb, nh, dk, dk), blk)],
        out_specs=(pl.BlockSpec((bb, nh, 2 * nt, dk), blk), pl.BlockSpec((bb, nh, dk, dk), blk)),
        out_shape=(jax.ShapeDtypeStruct((nb, nh, 2 * nt, dk), F32), jax.ShapeDtypeStruct((nb, nh, dk, dk), F32)),
        compiler_params=_cparams(("parallel",)), name="gdn_sample_state",
    )(r_bm, state_s)
    ws_tm = jnp.transpose(ws_bm, (2, 0, 1, 3)).reshape(2 * nt, nb, gw)
    og = pl.pallas_call(
        functools.partial(_gdn_sample_post_kernel, nt=nt),
        grid=(1,),
        in_specs=[full(ws_tm.shape), full(r_tm.shape), full(intra.shape), full(z_tm.shape), full(gn.shape)],
        out_specs=full((nt, nb, gw)),
        out_shape=jax.ShapeDtypeStruct((nt, nb, gw), BF16),
        compiler_params=_cparams(("arbitrary",)), name="gdn_sample_post",
    )(ws_tm, r_tm, intra, z_tm, gn)
    return og, s_new


def _lambda(lq1_ref, lk1_ref, lq2_ref, lk2_ref, lam_init):
    s1 = jnp.sum(lq1_ref[...] * lk1_ref[...], axis=-1, keepdims=True)
    s2 = jnp.sum(lq2_ref[...] * lk2_ref[...], axis=-1, keepdims=True)
    return jnp.exp(s1) - jnp.exp(s2) + lam_init


def _subln(o, g_ref, lam_init):
    return o * lax.rsqrt(jnp.mean(o * o, axis=-1, keepdims=True) + EPS) * g_ref[...] * (1.0 - lam_init)


def _attn_prompt_kernel(q_ref, k_ref, vt_ref, lq1_ref, lk1_ref, lq2_ref, lk2_ref, sg_ref, o_ref, *, tq, lam_init):
    qi = pl.program_id(1)
    nh, dv, dqk = DIFF_HEADS, DIFF_DV, DIFF_DQK
    lam = _lambda(lq1_ref, lk1_ref, lq2_ref, lk2_ref, lam_init)
    lane = lax.broadcasted_iota(jnp.int32, (tq, dv), 1)
    kidx = lax.broadcasted_iota(jnp.int32, (tq, tq), 0)
    qidx = lax.broadcasted_iota(jnp.int32, (tq, tq), 1)
    causal = kidx <= qidx
    nsum = 2 * SUBLANES
    ones_rows = jnp.ones((nsum, tq), BF16)
    lanes = [slice(h * dv, (h + 1) * dv) for h in range(nh)]
    qs = []
    for h in range(nh):
        q = q_ref[:, lanes[h]]
        zero = jnp.zeros_like(q)
        qs.append((jnp.where(lane < dqk, q, zero), jnp.where(lane >= dqk, q, zero)))

    def block(j, carry, masked):
        st, out = [], []
        for h in range(nh):
            kb = k_ref[pl.ds(pl.multiple_of(j * tq, tq), tq), lanes[h]]
            for c in range(2):
                s_hc = _dot_nt(kb, qs[h][c])
                st.append(jnp.where(causal, s_hc, NEG) if masked else s_hc)
        ps, ms, alphas = [], [], []
        for x in range(2 * nh):
            m_i = carry[x][0]
            m_new = jnp.maximum(m_i, jnp.max(st[x], axis=0, keepdims=True))
            alphas.append(jnp.exp2(m_i - m_new))
            ps.append(jnp.exp2(st[x] - m_new).astype(BF16))
            ms.append(m_new)
        for h in range(nh):
            vt = jnp.concatenate([vt_ref[j, lanes[h], :], ones_rows], axis=0)
            for c in range(2):
                x = 2 * h + c
                out.append((ms[x], alphas[x] * carry[x][1] + _dot(vt, ps[x])))
        return tuple(out)

    init = tuple((jnp.full((1, tq), NEG, F32), jnp.zeros((dv + nsum, tq), F32)) for _ in range(2 * nh))
    carry = lax.fori_loop(0, qi, lambda j, cr: block(j, cr, False), init)
    carry = block(qi, carry, True)
    for h in range(nh):
        ls = lanes[h]
        a1, a2 = carry[2 * h][1], carry[2 * h + 1][1]
        ot = a1[:dv] / a1[dv:dv + 1] - lam * (a2[:dv] / a2[dv:dv + 1])
        ot = ot * lax.rsqrt(jnp.mean(ot * ot, axis=0, keepdims=True) + EPS) * (1.0 - lam_init)
        o_ref[:, ls] = (jnp.transpose(ot) * sg_ref[...]).astype(o_ref.dtype)


def _attn_prompt(qa, ka, vt, lq1, lk1, lq2, lk2, sg, *, nb, seq, tq, lam_init):
    nh, dv = DIFF_HEADS, DIFF_DV
    wd = nh * dv
    nq = seq // tq
    qa3, ka3 = qa.reshape(nb, seq, wd), ka.reshape(nb, seq, wd)
    vt4 = vt.reshape(nb, nq, wd, tq)
    vec = _const_spec((1, DIFF_DQK))
    out = pl.pallas_call(
        functools.partial(_attn_prompt_kernel, tq=tq, lam_init=lam_init),
        grid=(nb, nq),
        in_specs=[pl.BlockSpec((None, tq, wd), lambda b, i: (b, i, 0)),
                  pl.BlockSpec((None, seq, wd), lambda b, i: (b, 0, 0)),
                  pl.BlockSpec((None, nq, wd, tq), lambda b, i: (b, 0, 0, 0)),
                  vec, vec, vec, vec, _const_spec((1, dv))],
        out_specs=pl.BlockSpec((None, tq, wd), lambda b, i: (b, i, 0)),
        out_shape=jax.ShapeDtypeStruct((nb, seq, wd), BF16),
        compiler_params=_cparams(("parallel", "arbitrary")), name="attn_prompt",
    )(qa3, ka3, vt4, lq1, lk1, lq2, lk2, sg)
    return out.reshape(nb * seq, wd)


def _attn_sample_kernel(pt_ref, q_ref, kn_ref, vn_ref, *rest, n_pages, nt, lam_init):
    k_refs = rest[:n_pages]
    v_refs = rest[n_pages:2 * n_pages]
    lq1_ref, lk1_ref, lq2_ref, lk2_ref, sg_ref, o_ref = rest[2 * n_pages:]
    del pt_ref
    nh, dv, dqk = DIFF_HEADS, DIFF_DV, DIFF_DQK
    page = k_refs[0].shape[0] // nh
    nr = 2 * nt
    lam = _lambda(lq1_ref, lk1_ref, lq2_ref, lk2_ref, lam_init)
    row = lax.broadcasted_iota(jnp.int32, (nr, dv), 0)
    lane = lax.broadcasted_iota(jnp.int32, (nr, dv), 1)
    first = row < nt
    keep = jnp.logical_xor(lane >= dqk, first)
    tpos = jnp.where(first, row, row - nt)
    new_ok = (lane < nt) & (lane <= tpos)
    zpad = jnp.zeros((page - nr, dv), F32)
    lanes = [slice(h * dv, (h + 1) * dv) for h in range(nh)]
    head_rows = [pl.ds(h, page, stride=nh) for h in range(nh)]
    groups = [list(range(j, min(j + 2, n_pages))) for j in range(0, n_pages, 2)]

    def past(refs, h, grp):
        return jnp.concatenate([refs[j][head_rows[h], :] for j in grp], axis=0).astype(BF16)

    s_all = []
    for h in range(nh):
        qh = q_ref[:, lanes[h]]
        qz = jnp.where(keep, qh, jnp.zeros_like(qh))
        s_h = [_dot_nt(qz, past(k_refs, h, grp)) for grp in groups]
        knew = jnp.concatenate([kn_ref[:, lanes[h]], zpad], axis=0)
        s_h.append(jnp.where(new_ok, _dot_nt(qz, knew.astype(BF16)), NEG))
        s_all.append(s_h)
    p_all, l_all = [], []
    for h in range(nh):
        m = jnp.max(s_all[h][0], axis=-1, keepdims=True)
        for s in s_all[h][1:]:
            m = jnp.maximum(m, jnp.max(s, axis=-1, keepdims=True))
        ps = [jnp.exp2(s - m) for s in s_all[h]]
        l = jnp.sum(ps[0], axis=-1, keepdims=True)
        for p in ps[1:]:
            l = l + jnp.sum(p, axis=-1, keepdims=True)
        p_all.append([p.astype(BF16) for p in ps])
        l_all.append(l)
    for h in range(nh):
        acc = _dot(p_all[h][-1], jnp.concatenate([vn_ref[:, lanes[h]], zpad], axis=0).astype(BF16))
        for gi, grp in enumerate(groups):
            acc = acc + _dot(p_all[h][gi], past(v_refs, h, grp))
        o2 = acc / l_all[h]
        o = o2 - lam * pltpu.roll(o2, nt, axis=0)
        o_ref[:, lanes[h]] = _subln(o, sg_ref, lam_init)


def _attn_sample(page_table, q2, kn8, vn8, cache_k, cache_v, lq1, lk1, lq2, lk2, sg, *, nt, lam_init):
    nb, n_pages = page_table.shape
    nh, dv = DIFF_HEADS, DIFF_DV
    page = cache_k.shape[1]
    wd = nh * dv
    ck = cache_k.reshape(cache_k.shape[0], page * nh, dv)
    cv = cache_v.reshape(cache_v.shape[0], page * nh, dv)
    nr = 2 * nt
    small = pl.BlockSpec((None, nr, wd), lambda b, pt: (b, 0, 0))

    def page_spec(j):
        return pl.BlockSpec((None, page * nh, dv), lambda b, pt: (pt[b, j], 0, 0))

    vec = pl.BlockSpec((1, DIFF_DQK), lambda b, pt: (0, 0))
    grid_spec = pltpu.PrefetchScalarGridSpec(
        num_scalar_prefetch=1, grid=(nb,),
        in_specs=[small, small, small] + [page_spec(j) for j in range(n_pages)] * 2
        + [vec, vec, vec, vec, pl.BlockSpec((1, dv), lambda b, pt: (0, 0))],
        out_specs=small)
    return pl.pallas_call(
        functools.partial(_attn_sample_kernel, n_pages=n_pages, nt=nt, lam_init=lam_init),
        grid_spec=grid_spec, out_shape=jax.ShapeDtypeStruct((nb, nr, wd), F32),
        compiler_params=_cparams(("parallel",)), name="attn_sample",
    )(page_table, q2, kn8, vn8, *([ck] * n_pages), *([cv] * n_pages), lq1, lk1, lq2, lk2, sg)


def _post_kernel(x_ref, og_ref, od_ref, hist_ref, wo_ref, gf_ref, wu_ref, fcw_ref, fcb_ref, wd_ref,
                 y_ref, tail_ref, ubuf, *, tm, shift, hr, dff):
    i = pl.program_id(1)
    gw = og_ref.shape[1]

    @pl.when(i == 0)
    def _():
        ubuf[0:hr, :] = hist_ref[...]

    hres = x_ref[...] + _dot(og_ref[...], wo_ref[0:gw, :]) + _dot(od_ref[...], wo_ref[gw:, :])
    hn = hres * lax.rsqrt(jnp.mean(hres * hres, axis=-1, keepdims=True) + EPS) * gf_ref[...]
    ubuf[hr:hr + tm, :] = _dot(hn.astype(BF16), wu_ref[...])
    fcw = fcw_ref[...]
    u = fcw[FFN_CONV - 1:FFN_CONV, :] * ubuf[hr:hr + tm, :] + fcb_ref[...]
    for j in range(1, FFN_CONV):
        u = u + fcw[FFN_CONV - 1 - j:FFN_CONV - j, :] * ubuf[hr - j * shift:hr - j * shift + tm, :]
    tail = ubuf[tm:tm + hr, :]
    tail_ref[...] = tail
    ubuf[0:hr, :] = tail
    act = (_silu(u[:, :dff]) * u[:, dff:]).astype(BF16)
    y_ref[...] = hres + _dot(act, wd_ref[...])


def _post(x2d, og, od, hist, w_out, g_ffn, w_up, fcw, fcb, w_down, *, nseq, tm, shift, hr):
    n, d = x2d.shape
    gw = og.shape[1]
    dff2 = w_up.shape[1]
    nt = n // (nseq * tm)
    row = lambda b, i: (b * nt + i, 0)
    in_specs = [
        pl.BlockSpec((tm, d), row), pl.BlockSpec((tm, gw), row), pl.BlockSpec((tm, od.shape[1]), row),
        pl.BlockSpec((None, hr, dff2), lambda b, i: (b, 0, 0)),
        _const_spec(w_out.shape), _const_spec((1, d)), _const_spec(w_up.shape),
        _const_spec((FFN_CONV, dff2)), _const_spec((1, dff2)), _const_spec(w_down.shape),
    ]
    out_shape = (jax.ShapeDtypeStruct((n, d), F32), jax.ShapeDtypeStruct((nseq, hr, dff2), F32))
    out_specs = (pl.BlockSpec((tm, d), row), pl.BlockSpec((None, hr, dff2), lambda b, i: (b, 0, 0)))
    return pl.pallas_call(
        functools.partial(_post_kernel, tm=tm, shift=shift, hr=hr, dff=dff2 // 2),
        grid=(nseq, nt), in_specs=in_specs, out_specs=out_specs, out_shape=out_shape,
        scratch_shapes=[pltpu.VMEM((hr + tm, dff2), F32)],
        compiler_params=_cparams(("parallel", "arbitrary")), name="post",
    )(x2d, og, od, hist, w_out, g_ffn, w_up, fcw, fcb, w_down)


def _lane_pad(vec, offset):
    out = jnp.zeros((1, LANES), F32)
    return lax.dynamic_update_slice(out, vec.reshape(1, -1).astype(F32), (0, offset))


def _layer(l, x_prompt, x_sample, state_gdn_conv, state_gdn_s, cache_k, cache_v, page_table, state_ffn_conv, wl):
    (attn_norm_g, w_in, gdn_conv_w, gdn_a_log, gdn_dt_bias, gdn_out_norm_g, diff_q_norm_g, diff_k_norm_g,
     lq1, lk1, lq2, lk2, diff_subln_g, w_out, ffn_norm_g, w_up, ffn_conv_w, ffn_conv_b, w_down) = wl
    nbp, seq, d = x_prompt.shape
    nbs, nts, _ = x_sample.shape
    nh, dk = GDN_HEADS, GDN_DK
    gw = nh * dk
    dw = DIFF_HEADS * DIFF_DV
    lam_init = 0.8 - 0.6 * math.exp(-0.3 * l)

    c_b = 4 * gw
    c_d = c_b + 2 * nh
    w_ba = w_in[:, c_b:c_d]
    w_main = jnp.concatenate([w_in[:, :c_b], w_in[:, c_d:], w_ba,
                              jnp.zeros((d, LANES - 2 * nh), w_in.dtype)], axis=1).astype(BF16)
    w_t = jnp.transpose(jnp.concatenate([w_ba, w_in[:, c_d + 2 * dw:]], axis=1)).astype(BF16)
    reps = dw // DIFF_DQK
    gq_t = jnp.tile(diff_q_norm_g.reshape(1, -1), (1, reps))
    gk_t = jnp.tile(diff_k_norm_g.reshape(1, -1), (1, reps))
    gid = jnp.arange(dw) // DIFF_DQK
    grp = jnp.where(gid[:, None] == gid[None, :], 1.0 / DIFF_DQK, 0.0).astype(BF16)
    g_attn = attn_norm_g.reshape(1, d)
    alr, dtr = _lane_pad(gdn_a_log, nh), _lane_pad(gdn_dt_bias, nh)
    alc = jnp.transpose(alr[:, :SUBLANES])
    dtc = jnp.transpose(dtr[:, :SUBLANES])
    gn = gdn_out_norm_g.reshape(1, dk)
    vecs = [v.reshape(1, -1) for v in (lq1, lk1, lq2, lk2)]
    sg = diff_subln_g.reshape(1, -1)
    w_out_b, w_up_b, w_down_b = w_out.astype(BF16), w_up.astype(BF16), w_down.astype(BF16)
    g_ffn = ffn_norm_g.reshape(1, d)
    fcb = ffn_conv_b.reshape(1, -1)
    dff2 = w_up.shape[1]

    xp2 = x_prompt.reshape(nbp * seq, d)
    tm_p = 256
    qkv, z, ba, bat, knew, vnew, qa, ka, vt = _inproj(xp2, g_attn, w_main, w_t, gq_t, gk_t, grp,
                                                      tm=tm_p, gw=gw, dw=dw)
    og, s_prompt = _gdn_prompt(qkv, z, ba, bat, gdn_conv_w, alr, dtr, alc, dtc, gn, nb=nbp, seq=seq, tl=256)
    od = _attn_prompt(qa, ka, vt, *vecs, sg, nb=nbp, seq=seq, tq=tm_p, lam_init=lam_init)
    hr_p = SUBLANES
    y_p, tail_p = _post(xp2, og, od, jnp.zeros((nbp, hr_p, dff2), F32), w_out_b, g_ffn, w_up_b,
                        ffn_conv_w, fcb, w_down_b, nseq=nbp, tm=256, shift=1, hr=hr_p)
    out_p = (y_p.reshape(nbp, seq, d),
             qkv.reshape(nbp, seq, 3 * gw)[:, seq - (GDN_CONV - 1):, :],
             s_prompt,
             knew.reshape(nbp, seq, DIFF_HEADS, DIFF_DV),
             vnew.reshape(nbp, seq, DIFF_HEADS, DIFF_DV),
             tail_p[:, hr_p - (FFN_CONV - 1):, :])

    xs2 = jnp.transpose(x_sample, (1, 0, 2)).reshape(nts * nbs, d)
    qkv, z, ba, _, knew, vnew, qa, _, _ = _inproj(xs2, g_attn, w_main, w_t, gq_t, gk_t, grp,
                                                  tm=min(256, nts * nbs), gw=gw, dw=dw)
    qkv_tm = qkv.reshape(nts, nbs, 3 * gw)
    hist_tm = jnp.transpose(state_gdn_conv, (1, 0, 2))
    og_tm, s_sample = _gdn_sample(qkv_tm, hist_tm, ba.reshape(nts, nbs, LANES), z.reshape(nts, nbs, gw),
                                  state_gdn_s, gdn_conv_w, alr, dtr, gn, bb=8)
    conv_all = jnp.concatenate([hist_tm, qkv_tm], axis=0)
    conv_s = jnp.transpose(conv_all[nts:], (1, 0, 2))
    to_bm = lambda a: jnp.transpose(a.reshape(nts, nbs, -1), (1, 0, 2))
    q_bm = to_bm(qa)
    q2 = jnp.concatenate([q_bm, q_bm], axis=1)
    pad = jnp.zeros((nbs, nts, dw), F32)
    kn_bm, vn_bm = to_bm(knew), to_bm(vnew)
    od_bm = _attn_sample(page_table, q2, jnp.concatenate([kn_bm, pad], axis=1),
                         jnp.concatenate([vn_bm, pad], axis=1), cache_k, cache_v, *vecs, sg,
                         nt=nts, lam_init=lam_init)
    od_tm = jnp.transpose(od_bm[:, :nts, :], (1, 0, 2)).reshape(nts * nbs, dw).astype(BF16)
    hr_s = (FFN_CONV - 1) * nbs
    hist_f = jnp.transpose(state_ffn_conv, (1, 0, 2)).reshape(1, hr_s, dff2)
    y_s, tail_s = _post(xs2, og_tm.reshape(nts * nbs, gw), od_tm, hist_f, w_out_b, g_ffn, w_up_b,
                        ffn_conv_w, fcb, w_down_b, nseq=1, tm=nbs, shift=nbs, hr=hr_s)
    out_s = (jnp.transpose(y_s.reshape(nts, nbs, d), (1, 0, 2)),
             conv_s,
             s_sample,
             kn_bm.reshape(nbs, nts, DIFF_HEADS, DIFF_DV),
             vn_bm.reshape(nbs, nts, DIFF_HEADS, DIFF_DV),
             jnp.transpose(tail_s.reshape(FFN_CONV - 1, nbs, dff2), (1, 0, 2)))
    return out_p, out_s


def kernel(x_prompt, x_sample, state_gdn_conv, state_gdn_S, cache_k, cache_v, page_table, state_ffn_conv, attn_norm_g, w_in, gdn_conv_w, gdn_A_log, gdn_dt_bias, gdn_out_norm_g, diff_q_norm_g, diff_k_norm_g, diff_lambda_q1, diff_lambda_k1, diff_lambda_q2, diff_lambda_k2, diff_subln_g, w_out, ffn_norm_g, w_up, ffn_conv_w, ffn_conv_b, w_down):
    depth = w_in.shape[0]
    hp, hs = x_prompt, x_sample
    outs_p, outs_s = [], []
    for l in range(depth):
        wl = (attn_norm_g[l], w_in[l], gdn_conv_w[l], gdn_A_log[l], gdn_dt_bias[l], gdn_out_norm_g[l],
              diff_q_norm_g[l], diff_k_norm_g[l], diff_lambda_q1[l], diff_lambda_k1[l], diff_lambda_q2[l],
              diff_lambda_k2[l], diff_subln_g[l], w_out[l], ffn_norm_g[l], w_up[l], ffn_conv_w[l],
              ffn_conv_b[l], w_down[l])
        out_p, out_s = _layer(l, hp, hs, state_gdn_conv[l], state_gdn_S[l], cache_k[l], cache_v[l],
                              page_table, state_ffn_conv[l], wl)
        hp, hs = out_p[0], out_s[0]
        outs_p.append(out_p[1:])
        outs_s.append(out_s[1:])
    stack = lambda outs, i: jnp.stack([o[i] for o in outs])
    return (hp, hs) + tuple(stack(outs_p, i) for i in range(5)) + tuple(stack(outs_s, i) for i in range(5))
```

```python
import functools
import math

import jax
import jax.numpy as jnp
from jax import lax
from jax.experimental import pallas as pl
from jax.experimental.pallas import tpu as pltpu

F32 = jnp.float32
BF16 = jnp.bfloat16
EPS = 1e-6
NEG = -1e30

GDN_HEADS = 4
GDN_DK = 128
GDN_CONV = 4
GDN_CHUNK = 64
DIFF_HEADS = 4
DIFF_DV = 128
DIFF_DQK = 64
FFN_CONV = 3
LANES = 128
SUBLANES = 8
VMEM_LIMIT = 56 * 1024 * 1024


def _cparams(sem):
    return pltpu.CompilerParams(dimension_semantics=sem, vmem_limit_bytes=VMEM_LIMIT)


def _const_spec(shape):
    nd = len(shape)
    return pl.BlockSpec(shape, lambda *_: (0,) * nd, pipeline_mode=pl.Buffered(1))


def _dot(a, b):
    return jnp.dot(a, b, preferred_element_type=F32)


def _dot_nt(a, b):
    return lax.dot_general(a, b, (((1,), (1,)), ((), ())), preferred_element_type=F32)


def _dot_tn(a, b):
    return lax.dot_general(a, b, (((0,), (0,)), ((), ())), preferred_element_type=F32)


def _softplus(x):
    return jnp.maximum(x, 0.0) + jnp.log1p(jnp.exp(-jnp.abs(x)))


def _silu(x):
    return x * jax.nn.sigmoid(x)


def _split3(x):
    hi = x.astype(BF16)
    r = x - hi.astype(F32)
    mid = r.astype(BF16)
    lo = (r - mid.astype(F32)).astype(BF16)
    return hi, mid, lo


def _inproj_core(x_ref, g_ref, wm_ref, wt_ref, gq_ref, gk_ref, grp_ref,
                 knew_ref, vnew_ref, qa_ref, ka_ref, vt_ref, gw, dw):
    tm = x_ref.shape[0]
    x = x_ref[...]
    xn = x * lax.rsqrt(jnp.mean(x * x, axis=-1, keepdims=True) + EPS) * g_ref[...]
    xb = xn.astype(BF16)
    proj = _dot(xb, wm_ref[...])
    tr = _dot_nt(wt_ref[...], xb)
    vt_ref[...] = tr[SUBLANES:].astype(BF16)
    c0 = 4 * gw
    dq = proj[:, c0:c0 + dw]
    dk = proj[:, c0 + dw:c0 + 2 * dw]
    dv = proj[:, c0 + 2 * dw:c0 + 3 * dw]
    grp = grp_ref[...]
    msq = _dot((dq * dq).astype(BF16), grp)
    msk = _dot((dk * dk).astype(BF16), grp)
    dqn = dq * lax.rsqrt(msq + EPS) * gq_ref[...]
    dkn = dk * lax.rsqrt(msk + EPS) * gk_ref[...]
    for h in range(DIFF_HEADS):
        head_rows = pl.ds(h, tm, stride=DIFF_HEADS)
        knew_ref[head_rows, :] = dkn[:, h * DIFF_DV:(h + 1) * DIFF_DV]
        vnew_ref[head_rows, :] = dv[:, h * DIFF_DV:(h + 1) * DIFF_DV]
    qa_ref[...] = (dqn * (DIFF_DQK ** -0.5 * math.log2(math.e))).astype(BF16)
    ka_ref[...] = dkn.astype(BF16)
    return proj, tr[:SUBLANES]


def _inproj_kernel(x_ref, g_ref, wm_ref, wt_ref, gq_ref, gk_ref, grp_ref,
                   qkv_ref, z_ref, ba_ref, knew_ref, vnew_ref, qa_ref, ka_ref, vt_ref, *, gw, dw):
    proj, _ = _inproj_core(x_ref, g_ref, wm_ref, wt_ref, gq_ref, gk_ref, grp_ref,
                           knew_ref, vnew_ref, qa_ref, ka_ref, vt_ref, gw, dw)
    qkv_ref[...] = proj[:, :3 * gw]
    z_ref[...] = proj[:, 3 * gw:4 * gw]
    ba_ref[...] = proj[:, 4 * gw + 3 * dw:4 * gw + 3 * dw + LANES]


def _inproj_gdn_kernel(x_ref, g_ref, wm_ref, wt_ref, gq_ref, gk_ref, grp_ref,
                       cw_ref, alr_ref, dtr_ref, alc_ref, dtc_ref,
                       z_ref, gcol_ref, grow_ref, qn_ref, kn_ref, kb_ref, vb_ref, tail_ref,
                       knew_ref, vnew_ref, qa_ref, ka_ref, vt_ref, xbuf, *, gw, dw):
    i = pl.program_id(1)
    tm = x_ref.shape[0]
    nh, dk = GDN_HEADS, GDN_DK
    hr = SUBLANES

    @pl.when(i == 0)
    def _():
        xbuf[0:hr, :] = jnp.zeros((hr, 3 * gw), F32)

    proj, trow = _inproj_core(x_ref, g_ref, wm_ref, wt_ref, gq_ref, gk_ref, grp_ref,
                              knew_ref, vnew_ref, qa_ref, ka_ref, vt_ref, gw, dw)
    z_ref[...] = proj[:, 3 * gw:4 * gw]
    xbuf[hr:hr + tm, :] = proj[:, :3 * gw]
    cw = cw_ref[...]
    y = cw[GDN_CONV - 1:GDN_CONV, :] * xbuf[hr:hr + tm, :]
    for j in range(1, GDN_CONV):
        y = y + cw[GDN_CONV - 1 - j:GDN_CONV - j, :] * xbuf[hr - j:hr - j + tm, :]
    tail = xbuf[tm:tm + hr, :]
    tail_ref[...] = tail
    xbuf[0:hr, :] = tail
    y = _silu(y)
    ba = proj[:, 4 * gw + 3 * dw:4 * gw + 3 * dw + LANES]
    beta_c = jax.nn.sigmoid(ba)
    gcol_ref[...] = -jnp.exp(alr_ref[...]) * _softplus(ba + dtr_ref[...])
    grow_ref[...] = -jnp.exp(alc_ref[...]) * _softplus(trow + dtc_ref[...])
    for h in range(nh):
        ls = slice(h * dk, (h + 1) * dk)
        q = y[:, h * dk:(h + 1) * dk]
        k = y[:, gw + h * dk:gw + (h + 1) * dk]
        v = y[:, 2 * gw + h * dk:2 * gw + (h + 1) * dk]
        qn = q * lax.rsqrt(jnp.sum(q * q, axis=-1, keepdims=True) + EPS) * (dk ** -0.5)
        kn = k * lax.rsqrt(jnp.sum(k * k, axis=-1, keepdims=True) + EPS)
        beta = jnp.broadcast_to(beta_c[:, h:h + 1], (tm, dk))
        qn_ref[:, ls] = qn.astype(BF16)
        kn_ref[:, ls] = kn.astype(BF16)
        kb_ref[:, ls] = (kn * beta).astype(BF16)
        vb_ref[:, ls] = (v * beta).astype(BF16)


def _attn_out(n, tm, dw):
    return (
        jax.ShapeDtypeStruct((n * DIFF_HEADS, DIFF_DV), F32),
        jax.ShapeDtypeStruct((n * DIFF_HEADS, DIFF_DV), F32),
        jax.ShapeDtypeStruct((n, dw), BF16),
        jax.ShapeDtypeStruct((n, dw), BF16),
        jax.ShapeDtypeStruct((n // tm, dw, tm), BF16),
    )


def _inproj(x2d, g_attn, w_main, w_t, gq_t, gk_t, grp, *, tm, gw, dw):
    n, d = x2d.shape
    row = lambda i: (i, 0)
    out_shape = (
        jax.ShapeDtypeStruct((n, 3 * gw), F32),
        jax.ShapeDtypeStruct((n, gw), F32),
        jax.ShapeDtypeStruct((n, LANES), F32),
    ) + _attn_out(n, tm, dw)
    out_specs = (
        pl.BlockSpec((tm, 3 * gw), row), pl.BlockSpec((tm, gw), row), pl.BlockSpec((tm, LANES), row),
        pl.BlockSpec((tm * DIFF_HEADS, DIFF_DV), row), pl.BlockSpec((tm * DIFF_HEADS, DIFF_DV), row),
        pl.BlockSpec((tm, dw), row), pl.BlockSpec((tm, dw), row),
        pl.BlockSpec((None, dw, tm), lambda i: (i, 0, 0)),
    )
    in_specs = [
        pl.BlockSpec((tm, d), row), _const_spec((1, d)), _const_spec(w_main.shape),
        _const_spec(w_t.shape), _const_spec((1, dw)), _const_spec((1, dw)), _const_spec((dw, dw)),
    ]
    return pl.pallas_call(
        functools.partial(_inproj_kernel, gw=gw, dw=dw),
        grid=(n // tm,), in_specs=in_specs, out_specs=out_specs, out_shape=out_shape,
        compiler_params=_cparams(("parallel",)), name="inproj",
    )(x2d, g_attn, w_main, w_t, gq_t, gk_t, grp)


def _inproj_gdn(x2d, g_attn, w_main, w_t, gq_t, gk_t, grp, cw, alr, dtr, alc, dtc, *, nb, tm, gw, dw):
    n, d = x2d.shape
    nt = n // (nb * tm)
    row = lambda b, i: (b * nt + i, 0)
    out_shape = (
        jax.ShapeDtypeStruct((n, gw), F32),
        jax.ShapeDtypeStruct((n, LANES), F32),
        jax.ShapeDtypeStruct((SUBLANES, n), F32),
        jax.ShapeDtypeStruct((n, gw), BF16),
        jax.ShapeDtypeStruct((n, gw), BF16),
        jax.ShapeDtypeStruct((n, gw), BF16),
        jax.ShapeDtypeStruct((n, gw), BF16),
        jax.ShapeDtypeStruct((nb, SUBLANES, 3 * gw), F32),
    ) + _attn_out(n, tm, dw)
    out_specs = (
        pl.BlockSpec((tm, gw), row), pl.BlockSpec((tm, LANES), row),
        pl.BlockSpec((SUBLANES, tm), lambda b, i: (0, b * nt + i)),
        pl.BlockSpec((tm, gw), row), pl.BlockSpec((tm, gw), row), pl.BlockSpec((tm, gw), row),
        pl.BlockSpec((tm, gw), row),
        pl.BlockSpec((None, SUBLANES, 3 * gw), lambda b, i: (b, 0, 0)),
        pl.BlockSpec((tm * DIFF_HEADS, DIFF_DV), row), pl.BlockSpec((tm * DIFF_HEADS, DIFF_DV), row),
        pl.BlockSpec((tm, dw), row), pl.BlockSpec((tm, dw), row),
        pl.BlockSpec((None, dw, tm), lambda b, i: (b * nt + i, 0, 0)),
    )
    in_specs = [
        pl.BlockSpec((tm, d), row), _const_spec((1, d)), _const_spec(w_main.shape),
        _const_spec(w_t.shape), _const_spec((1, dw)), _const_spec((1, dw)), _const_spec((dw, dw)),
        _const_spec(cw.shape), _const_spec((1, LANES)), _const_spec((1, LANES)),
        _const_spec((SUBLANES, 1)), _const_spec((SUBLANES, 1)),
    ]
    return pl.pallas_call(
        functools.partial(_inproj_gdn_kernel, gw=gw, dw=dw),
        grid=(nb, nt), in_specs=in_specs, out_specs=out_specs, out_shape=out_shape,
        scratch_shapes=[pltpu.VMEM((tm + SUBLANES, 3 * gw), F32)],
        compiler_params=_cparams(("parallel", "arbitrary")), name="inproj_gdn",
    )(x2d, g_attn, w_main, w_t, gq_t, gk_t, grp, cw, alr, dtr, alc, dtc)


def _gdn_prompt_kernel(qn_ref, kn_ref, kb_ref, vb_ref, gcol_ref, grow_ref, z_ref, gn_ref,
                       og_ref, sout_ref, s_scr, *, tl):
    i = pl.program_id(1)
    nh, dk, c = GDN_HEADS, GDN_DK, GDN_CHUNK

    @pl.when(i == 0)
    def _():
        s_scr[...] = jnp.zeros_like(s_scr)

    g_c = gcol_ref[...]
    g_rep = jnp.concatenate([jnp.broadcast_to(g_c[:, nh + h:nh + h + 1], (tl, dk)) for h in range(nh)], axis=1)
    g_r = grow_ref[...]

    ii = lax.broadcasted_iota(jnp.int32, (tl, tl), 0)
    jj = lax.broadcasted_iota(jnp.int32, (tl, tl), 1)
    same = lax.shift_right_logical(ii, 6) == lax.shift_right_logical(jj, 6)
    tri = jnp.where(same & (ii >= jj), 1.0, 0.0).astype(BF16)
    triu = jnp.where(same & (ii <= jj), 1.0, 0.0).astype(BF16)
    gc_rep = sum(_dot(tri, p) for p in _split3(g_rep))
    gc_r = sum(_dot(p, triu) for p in _split3(g_r))

    pr = 2 * c
    pi = lax.broadcasted_iota(jnp.int32, (pr, pr), 0)
    pj = lax.broadcasted_iota(jnp.int32, (pr, pr), 1)
    psame = lax.shift_right_logical(pi, 6) == lax.shift_right_logical(pj, 6)
    lower = psame & (pi >= pj)
    strict = psame & (pi > pj)
    eye = jnp.where(pi == pj, 1.0, 0.0).astype(F32)
    zeros_c = jnp.zeros((c, dk), F32)

    npair = tl // pr
    hp = [(h, p) for h in range(nh) for p in range(npair)]
    def tile(ref, h, p):
        return ref[p * pr:(p + 1) * pr, h * dk:(h + 1) * dk]

    rows = lambda arr, p: arr[p * pr:(p + 1) * pr]
    gcp = {(h, p): rows(gc_rep[:, h * dk:(h + 1) * dk], p) for h, p in hp}
    dec = {(h, p): jnp.exp(jnp.where(lower, gcp[h, p] - gc_r[nh + h:nh + h + 1, p * pr:(p + 1) * pr], NEG))
           for h, p in hp}
    kpb = {(h, p): tile(kn_ref, h, p) for h, p in hp}
    kbp = {(h, p): tile(kb_ref, h, p) for h, p in hp}
    m = {x: -jnp.where(strict, _dot_nt(kbp[x], kpb[x]) * dec[x], 0.0) for x in hp}
    a = {x: eye + m[x] for x in hp}
    pw = {x: _dot(m[x].astype(BF16), m[x].astype(BF16)) for x in hp}
    for it in range(1, 6):
        for x in hp:
            pwb = pw[x].astype(BF16)
            if it < 5:
                res = _dot(pwb, jnp.concatenate([a[x].astype(BF16), pwb], axis=1))
                a[x] = a[x] + res[:, :pr]
                pw[x] = res[:, pr:]
            else:
                a[x] = a[x] + _dot(pwb, a[x].astype(BF16))
    egc = {x: jnp.exp(gcp[x]) for x in hp}
    sol = {x: _dot(a[x].astype(BF16),
                   jnp.concatenate([tile(vb_ref, *x), (kbp[x].astype(F32) * egc[x]).astype(BF16)], axis=1))
           for x in hp}
    qk = {x: (_dot_nt(tile(qn_ref, *x), kpb[x]) * dec[x]).astype(BF16) for x in hp}
    qg = {x: tile(qn_ref, *x).astype(F32) * egc[x] for x in hp}

    s_cur = [s_scr[h] for h in range(nh)]
    o_rows = [[] for _ in range(nh)]
    for p in range(npair):
        for cc in range(2):
            c0 = cc * c
            ws = []
            for h in range(nh):
                wq = jnp.concatenate([sol[h, p][c0:c0 + c, dk:], qg[h, p][c0:c0 + c]], axis=0)
                ws.append(_dot(wq.astype(BF16), s_cur[h].astype(BF16)))
            for h in range(nh):
                g_c0 = gcp[h, p][c0:c0 + c]
                glast = g_c0[c - 1:c, :]
                kd = kpb[h, p][c0:c0 + c].astype(F32) * jnp.exp(glast - g_c0)
                v_new = (sol[h, p][c0:c0 + c, :dk] - ws[h][:c]).astype(BF16)
                s_cur[h] = s_cur[h] * jnp.exp(glast) + _dot_tn(kd.astype(BF16), v_new)
                zc = jnp.zeros_like(v_new)
                vn_pad = jnp.concatenate([v_new, zc] if cc == 0 else [zc, v_new], axis=0)
                o_rows[h].append(ws[h][c:] + _dot(qk[h, p][c0:c0 + c], vn_pad))
    for h in range(nh):
        s_scr[h] = s_cur[h]
        o = jnp.concatenate(o_rows[h], axis=0)
        o = o * lax.rsqrt(jnp.mean(o * o, axis=-1, keepdims=True) + EPS) * gn_ref[...]
        og_ref[:, h * dk:(h + 1) * dk] = (o * _silu(z_ref[:, h * dk:(h + 1) * dk])).astype(og_ref.dtype)

    @pl.when(i == pl.num_programs(1) - 1)
    def _():
        sout_ref[...] = s_scr[...]


def _gdn_prompt(qn, kn, kb, vb, gcol, grow, z, gn, *, nb, seq, tl):
    nh, dk = GDN_HEADS, GDN_DK
    gw = nh * dk
    nt = seq // tl
    row = lambda b, i: (b * nt + i, 0)
    wide = pl.BlockSpec((tl, gw), row)
    in_specs = [
        wide, wide, wide, wide, pl.BlockSpec((tl, LANES), row),
        pl.BlockSpec((SUBLANES, tl), lambda b, i: (0, b * nt + i)),
        wide, _const_spec((1, dk)),
    ]
    out_shape = (jax.ShapeDtypeStruct((nb * seq, gw), BF16),
                 jax.ShapeDtypeStruct((nb, nh, dk, dk), F32))
    out_specs = (pl.BlockSpec((tl, gw), row),
                 pl.BlockSpec((None, nh, dk, dk), lambda b, i: (b, 0, 0, 0)))
    return pl.pallas_call(
        functools.partial(_gdn_prompt_kernel, tl=tl),
        grid=(nb, nt), in_specs=in_specs, out_specs=out_specs, out_shape=out_shape,
        scratch_shapes=[pltpu.VMEM((nh, dk, dk), F32)],
        compiler_params=_cparams(("parallel", "arbitrary")), name="gdn_prompt",
    )(qn, kn, kb, vb, gcol, grow, z, gn)


def _gdn_sample_pre_kernel(qkv_ref, hist_ref, ba_ref, cw_ref, alr_ref, dtr_ref, r_ref, intra_ref, *, nt):
    nh, dk = GDN_HEADS, GDN_DK
    gw = nh * dk
    nb = qkv_ref.shape[1]
    cw = cw_ref[...]
    nhist = GDN_CONV - 1
    xp = [hist_ref[j] for j in range(nhist)] + [qkv_ref[t] for t in range(nt)]
    ys = []
    for t in range(nt):
        y = cw[0:1, :] * xp[t]
        for j in range(1, GDN_CONV):
            y = y + cw[j:j + 1, :] * xp[t + j]
        ys.append(_silu(y))
    beta, gc = [], []
    for t in range(nt):
        ba = ba_ref[t]
        beta_c = jax.nn.sigmoid(ba)
        g_c = -jnp.exp(alr_ref[...]) * _softplus(ba + dtr_ref[...])
        beta.append([jnp.broadcast_to(beta_c[:, h:h + 1], (nb, dk)) for h in range(nh)])
        g_t = [jnp.broadcast_to(g_c[:, nh + h:nh + h + 1], (nb, dk)) for h in range(nh)]
        gc.append(g_t if t == 0 else [gc[t - 1][h] + g_t[h] for h in range(nh)])
    zrow = jnp.zeros((nb, dk), F32)
    for h in range(nh):
        q, k, v = [], [], []
        for t in range(nt):
            qt = ys[t][:, h * dk:(h + 1) * dk]
            kt = ys[t][:, gw + h * dk:gw + (h + 1) * dk]
            q.append(qt * lax.rsqrt(jnp.sum(qt * qt, axis=-1, keepdims=True) + EPS) * (dk ** -0.5))
            k.append(kt * lax.rsqrt(jnp.sum(kt * kt, axis=-1, keepdims=True) + EPS))
            v.append(ys[t][:, 2 * gw + h * dk:2 * gw + (h + 1) * dk])
        g = [gc[t][h] for t in range(nt)]
        us, ws = [], []
        for t in range(nt):
            b_t = beta[t][h]
            u_t = v[t] * b_t
            w_t = k[t] * b_t * jnp.exp(g[t])
            for s in range(t):
                m_ts = b_t * jnp.sum(k[t] * k[s], axis=-1, keepdims=True) * jnp.exp(g[t] - g[s])
                u_t = u_t - m_ts * us[s]
                w_t = w_t - m_ts * ws[s]
            us.append(u_t)
            ws.append(w_t)
        lane = slice(h * dk, (h + 1) * dk)
        for t in range(nt):
            r_ref[t, :, lane] = ws[t]
            r_ref[nt + t, :, lane] = q[t] * jnp.exp(g[t])
            r_ref[2 * nt + t, :, lane] = us[t]
            r_ref[3 * nt + t, :, lane] = k[t] * jnp.exp(g[nt - 1] - g[t])
            for s in range(nt):
                if s <= t:
                    intra_ref[t * nt + s, :, lane] = (jnp.sum(q[t] * k[s], axis=-1, keepdims=True)
                                                       * jnp.exp(g[t] - g[s]))
                else:
                    intra_ref[t * nt + s, :, lane] = zrow
        r_ref[4 * nt, :, lane] = jnp.exp(g[nt - 1])
        for r in range(4 * nt + 1, r_ref.shape[0]):
            r_ref[r, :, lane] = zrow


def _gdn_sample_state_kernel(r_ref, s_ref, ws_ref, sout_ref, *, nt, bb):
    nh = GDN_HEADS
    rows = lax.broadcasted_iota(jnp.int32, (2 * nt, GDN_DK), 0)
    inst = [(bi, h) for bi in range(bb) for h in range(nh)]
    ws = {x: _dot(r_ref[x[0], x[1], 0:2 * nt, :].astype(BF16), s_ref[x].astype(BF16)) for x in inst}
    for x in inst:
        ud = r_ref[x[0], x[1], 2 * nt:4 * nt, :]
        vn = jnp.where(rows < nt, ud - ws[x], 0.0)
        kd = jnp.where(rows < nt, pltpu.roll(ud, nt, axis=0), 0.0)
        ws_ref[x] = ws[x]
        sout_ref[x] = (s_ref[x] * r_ref[x[0], x[1], 4 * nt:4 * nt + 1, :]
                       + _dot_tn(kd.astype(BF16), vn.astype(BF16)))


def _gdn_sample_post_kernel(ws_ref, r_ref, intra_ref, z_ref, gn_ref, og_ref, *, nt):
    nh, dk = GDN_HEADS, GDN_DK
    for h in range(nh):
        lane = slice(h * dk, (h + 1) * dk)
        vn = [r_ref[2 * nt + t, :, lane] - ws_ref[t, :, lane] for t in range(nt)]
        for t in range(nt):
            o = ws_ref[nt + t, :, lane]
            for s in range(t + 1):
                o = o + intra_ref[t * nt + s, :, lane] * vn[s]
            o = o * lax.rsqrt(jnp.mean(o * o, axis=-1, keepdims=True) + EPS) * gn_ref[...]
            og_ref[t, :, lane] = (o * _silu(z_ref[t, :, lane])).astype(og_ref.dtype)


def _gdn_sample(qkv_tm, hist_tm, ba_tm, z_tm, state_s, cw, alr, dtr, gn, *, bb):
    nt, nb, _ = qkv_tm.shape
    nh, dk = GDN_HEADS, GDN_DK
    gw = nh * dk
    nr = 6 * nt
    full = lambda shape: pl.BlockSpec(shape, lambda *_: (0,) * len(shape))
    r_tm, intra = pl.pallas_call(
        functools.partial(_gdn_sample_pre_kernel, nt=nt),
        grid=(1,),
        in_specs=[full(qkv_tm.shape), full(hist_tm.shape), full(ba_tm.shape), full(cw.shape),
                  full(alr.shape), full(dtr.shape)],
        out_specs=(full((nr, nb, gw)), full((nt * nt, nb, gw))),
        out_shape=(jax.ShapeDtypeStruct((nr, nb, gw), F32), jax.ShapeDtypeStruct((nt * nt, nb, gw), F32)),
        compiler_params=_cparams(("arbitrary",)), name="gdn_sample_pre",
    )(qkv_tm, hist_tm, ba_tm, cw, alr, dtr)
    r_bm = jnp.transpose(r_tm.reshape(nr, nb, nh, dk), (1, 2, 0, 3))
    blk = lambda b: (b, 0, 0, 0)
    ws_bm, s_new = pl.pallas_call(
        functools.partial(_gdn_sample_state_kernel, nt=nt, bb=bb),
        grid=(nb // bb,),
        in_specs=[pl.BlockSpec((bb, nh, nr, dk), blk), pl.BlockSpec((bb, nh, dk, dk), blk)],
        out_specs=(pl.BlockSpec((bb, nh, 2 * nt, dk), blk), pl.BlockSpec((bb, nh, dk, dk), blk)),
        out_shape=(jax.ShapeDtypeStruct((nb, nh, 2 * nt, dk), F32), jax.ShapeDtypeStruct((nb, nh, dk, dk), F32)),
        compiler_params=_cparams(("parallel",)), name="gdn_sample_state",
    )(r_bm, state_s)
    ws_tm = jnp.transpose(ws_bm, (2, 0, 1, 3)).reshape(2 * nt, nb, gw)
    og = pl.pallas_call(
        functools.partial(_gdn_sample_post_kernel, nt=nt),
        grid=(1,),
        in_specs=[full(ws_tm.shape), full(r_tm.shape), full(intra.shape), full(z_tm.shape), full(gn.shape)],
        out_specs=full((nt, nb, gw)),
        out_shape=jax.ShapeDtypeStruct((nt, nb, gw), BF16),
        compiler_params=_cparams(("arbitrary",)), name="gdn_sample_post",
    )(ws_tm, r_tm, intra, z_tm, gn)
    return og, s_new


def _lambda(lq1_ref, lk1_ref, lq2_ref, lk2_ref, lam_init):
    s1 = jnp.sum(lq1_ref[...] * lk1_ref[...], axis=-1, keepdims=True)
    s2 = jnp.sum(lq2_ref[...] * lk2_ref[...], axis=-1, keepdims=True)
    return jnp.exp(s1) - jnp.exp(s2) + lam_init


def _subln(o, g_ref, lam_init):
    return o * lax.rsqrt(jnp.mean(o * o, axis=-1, keepdims=True) + EPS) * g_ref[...] * (1.0 - lam_init)


def _attn_prompt_kernel(q_ref, k_ref, vt_ref, lq1_ref, lk1_ref, lq2_ref, lk2_ref, sg_ref, o_ref, *, tq, lam_init):
    qi = pl.program_id(1)
    nh, dv, dqk = DIFF_HEADS, DIFF_DV, DIFF_DQK
    lam = _lambda(lq1_ref, lk1_ref, lq2_ref, lk2_ref, lam_init)
    lane = lax.broadcasted_iota(jnp.int32, (tq, dv), 1)
    kidx = lax.broadcasted_iota(jnp.int32, (tq, tq), 0)
    qidx = lax.broadcasted_iota(jnp.int32, (tq, tq), 1)
    causal = kidx <= qidx
    nsum = 2 * SUBLANES
    ones_rows = jnp.ones((nsum, tq), BF16)
    lanes = [slice(h * dv, (h + 1) * dv) for h in range(nh)]
    qs = []
    for h in range(nh):
        q = q_ref[:, lanes[h]]
        zero = jnp.zeros_like(q)
        qs.append((jnp.where(lane < dqk, q, zero), jnp.where(lane >= dqk, q, zero)))

    nvt = tq // vt_ref.shape[-1]

    def block(j, carry, masked):
        st, out = [], []
        for h in range(nh):
            kb = k_ref[pl.ds(pl.multiple_of(j * tq, tq), tq), lanes[h]]
            for c in range(2):
                s_hc = _dot_nt(kb, qs[h][c])
                st.append(jnp.where(causal, s_hc, NEG) if masked else s_hc)
        ps, ms, alphas = [], [], []
        for x in range(2 * nh):
            m_i = carry[x][0]
            m_new = jnp.maximum(m_i, jnp.max(st[x], axis=0, keepdims=True))
            alphas.append(jnp.exp2(m_i - m_new))
            ps.append(jnp.exp2(st[x] - m_new).astype(BF16))
            ms.append(m_new)
        for h in range(nh):
            vt = jnp.concatenate([vt_ref[j * nvt + t, lanes[h], :] for t in range(nvt)], axis=1)
            vt = jnp.concatenate([vt, ones_rows], axis=0)
            for c in range(2):
                x = 2 * h + c
                out.append((ms[x], alphas[x] * carry[x][1] + _dot(vt, ps[x])))
        return tuple(out)

    init = tuple((jnp.full((1, tq), NEG, F32), jnp.zeros((dv + nsum, tq), F32)) for _ in range(2 * nh))
    carry = lax.fori_loop(0, qi, lambda j, cr: block(j, cr, False), init)
    carry = block(qi, carry, True)
    for h in range(nh):
        ls = lanes[h]
        a1, a2 = carry[2 * h][1], carry[2 * h + 1][1]
        ot = a1[:dv] / a1[dv:dv + 1] - lam * (a2[:dv] / a2[dv:dv + 1])
        ot = ot * lax.rsqrt(jnp.mean(ot * ot, axis=0, keepdims=True) + EPS) * (1.0 - lam_init)
        o_ref[:, ls] = (jnp.transpose(ot) * sg_ref[...]).astype(o_ref.dtype)


def _attn_prompt(qa, ka, vt, lq1, lk1, lq2, lk2, sg, *, nb, seq, tq, lam_init):
    nh, dv = DIFF_HEADS, DIFF_DV
    wd = nh * dv
    nq = seq // tq
    qa3, ka3 = qa.reshape(nb, seq, wd), ka.reshape(nb, seq, wd)
    tv = vt.shape[-1]
    nv = seq // tv
    vt4 = vt.reshape(nb, nv, wd, tv)
    vec = _const_spec((1, DIFF_DQK))
    out = pl.pallas_call(
        functools.partial(_attn_prompt_kernel, tq=tq, lam_init=lam_init),
        grid=(nb, nq),
        in_specs=[pl.BlockSpec((None, tq, wd), lambda b, i: (b, i, 0)),
                  pl.BlockSpec((None, seq, wd), lambda b, i: (b, 0, 0)),
                  pl.BlockSpec((None, nv, wd, tv), lambda b, i: (b, 0, 0, 0)),
                  vec, vec, vec, vec, _const_spec((1, dv))],
        out_specs=pl.BlockSpec((None, tq, wd), lambda b, i: (b, i, 0)),
        out_shape=jax.ShapeDtypeStruct((nb, seq, wd), BF16),
        compiler_params=_cparams(("parallel", "arbitrary")), name="attn_prompt",
    )(qa3, ka3, vt4, lq1, lk1, lq2, lk2, sg)
    return out.reshape(nb * seq, wd)


def _attn_sample_kernel(pt_ref, q_ref, kn_ref, vn_ref, *rest, n_pages, nt, lam_init):
    k_refs = rest[:n_pages]
    v_refs = rest[n_pages:2 * n_pages]
    lq1_ref, lk1_ref, lq2_ref, lk2_ref, sg_ref, o_ref = rest[2 * n_pages:]
    del pt_ref
    nh, dv, dqk = DIFF_HEADS, DIFF_DV, DIFF_DQK
    page = k_refs[0].shape[0] // nh
    nr = 2 * nt
    lam = _lambda(lq1_ref, lk1_ref, lq2_ref, lk2_ref, lam_init)
    row = lax.broadcasted_iota(jnp.int32, (nr, dv), 0)
    lane = lax.broadcasted_iota(jnp.int32, (nr, dv), 1)
    first = row < nt
    keep = jnp.logical_xor(lane >= dqk, first)
    tpos = jnp.where(first, row, row - nt)
    new_ok = (lane < nt) & (lane <= tpos)
    zpad = jnp.zeros((page - nr, dv), F32)
    lanes = [slice(h * dv, (h + 1) * dv) for h in range(nh)]
    head_rows = [pl.ds(h, page, stride=nh) for h in range(nh)]
    groups = [list(range(j, min(j + 2, n_pages))) for j in range(0, n_pages, 2)]

    def past(refs, h, grp):
        return jnp.concatenate([refs[j][head_rows[h], :] for j in grp], axis=0).astype(BF16)

    s_all = []
    for h in range(nh):
        qh = q_ref[:, lanes[h]]
        qz = jnp.where(keep, qh, jnp.zeros_like(qh))
        s_h = [_dot_nt(qz, past(k_refs, h, grp)) for grp in groups]
        knew = jnp.concatenate([kn_ref[:, lanes[h]], zpad], axis=0)
        s_h.append(jnp.where(new_ok, _dot_nt(qz, knew.astype(BF16)), NEG))
        s_all.append(s_h)
    p_all, l_all = [], []
    for h in range(nh):
        m = jnp.max(s_all[h][0], axis=-1, keepdims=True)
        for s in s_all[h][1:]:
            m = jnp.maximum(m, jnp.max(s, axis=-1, keepdims=True))
        ps = [jnp.exp2(s - m) for s in s_all[h]]
        l = jnp.sum(ps[0], axis=-1, keepdims=True)
        for p in ps[1:]:
            l = l + jnp.sum(p, axis=-1, keepdims=True)
        p_all.append([p.astype(BF16) for p in ps])
        l_all.append(l)
    for h in range(nh):
        acc = _dot(p_all[h][-1], jnp.concatenate([vn_ref[:, lanes[h]], zpad], axis=0).astype(BF16))
        for gi, grp in enumerate(groups):
            acc = acc + _dot(p_all[h][gi], past(v_refs, h, grp))
        o2 = acc / l_all[h]
        o = o2 - lam * pltpu.roll(o2, nt, axis=0)
        o_ref[:, lanes[h]] = _subln(o, sg_ref, lam_init)


def _attn_sample(page_table, q2, kn8, vn8, cache_k, cache_v, lq1, lk1, lq2, lk2, sg, *, nt, lam_init):
    nb, n_pages = page_table.shape
    nh, dv = DIFF_HEADS, DIFF_DV
    page = cache_k.shape[1]
    wd = nh * dv
    ck = cache_k.reshape(cache_k.shape[0], page * nh, dv)
    cv = cache_v.reshape(cache_v.shape[0], page * nh, dv)
    nr = 2 * nt
    small = pl.BlockSpec((None, nr, wd), lambda b, pt: (b, 0, 0))

    def page_spec(j):
        return pl.BlockSpec((None, page * nh, dv), lambda b, pt: (pt[b, j], 0, 0))

    vec = pl.BlockSpec((1, DIFF_DQK), lambda b, pt: (0, 0))
    grid_spec = pltpu.PrefetchScalarGridSpec(
        num_scalar_prefetch=1, grid=(nb,),
        in_specs=[small, small, small] + [page_spec(j) for j in range(n_pages)] * 2
        + [vec, vec, vec, vec, pl.BlockSpec((1, dv), lambda b, pt: (0, 0))],
        out_specs=small)
    return pl.pallas_call(
        functools.partial(_attn_sample_kernel, n_pages=n_pages, nt=nt, lam_init=lam_init),
        grid_spec=grid_spec, out_shape=jax.ShapeDtypeStruct((nb, nr, wd), F32),
        compiler_params=_cparams(("parallel",)), name="attn_sample",
    )(page_table, q2, kn8, vn8, *([ck] * n_pages), *([cv] * n_pages), lq1, lk1, lq2, lk2, sg)


def _post_kernel(x_ref, og_ref, od_ref, hist_ref, wo_ref, gf_ref, wu_ref, fcw_ref, fcb_ref, wd_ref,
                 y_ref, tail_ref, ubuf, *, tm, shift, hr, dff):
    i = pl.program_id(1)
    gw = og_ref.shape[1]

    @pl.when(i == 0)
    def _():
        ubuf[0:hr, :] = hist_ref[...]

    hres = x_ref[...] + _dot(og_ref[...], wo_ref[0:gw, :]) + _dot(od_ref[...], wo_ref[gw:, :])
    hn = hres * lax.rsqrt(jnp.mean(hres * hres, axis=-1, keepdims=True) + EPS) * gf_ref[...]
    ubuf[hr:hr + tm, :] = _dot(hn.astype(BF16), wu_ref[...])
    fcw = fcw_ref[...]
    u = fcw[FFN_CONV - 1:FFN_CONV, :] * ubuf[hr:hr + tm, :] + fcb_ref[...]
    for j in range(1, FFN_CONV):
        u = u + fcw[FFN_CONV - 1 - j:FFN_CONV - j, :] * ubuf[hr - j * shift:hr - j * shift + tm, :]
    tail = ubuf[tm:tm + hr, :]
    tail_ref[...] = tail
    ubuf[0:hr, :] = tail
    act = (_silu(u[:, :dff]) * u[:, dff:]).astype(BF16)
    y_ref[...] = hres + _dot(act, wd_ref[...])


def _post(x2d, og, od, hist, w_out, g_ffn, w_up, fcw, fcb, w_down, *, nseq, tm, shift, hr):
    n, d = x2d.shape
    gw = og.shape[1]
    dff2 = w_up.shape[1]
    nt = n // (nseq * tm)
    row = lambda b, i: (b * nt + i, 0)
    in_specs = [
        pl.BlockSpec((tm, d), row), pl.BlockSpec((tm, gw), row), pl.BlockSpec((tm, od.shape[1]), row),
        pl.BlockSpec((None, hr, dff2), lambda b, i: (b, 0, 0)),
        _const_spec(w_out.shape), _const_spec((1, d)), _const_spec(w_up.shape),
        _const_spec((FFN_CONV, dff2)), _const_spec((1, dff2)), _const_spec(w_down.shape),
    ]
    out_shape = (jax.ShapeDtypeStruct((n, d), F32), jax.ShapeDtypeStruct((nseq, hr, dff2), F32))
    out_specs = (pl.BlockSpec((tm, d), row), pl.BlockSpec((None, hr, dff2), lambda b, i: (b, 0, 0)))
    return pl.pallas_call(
        functools.partial(_post_kernel, tm=tm, shift=shift, hr=hr, dff=dff2 // 2),
        grid=(nseq, nt), in_specs=in_specs, out_specs=out_specs, out_shape=out_shape,
        scratch_shapes=[pltpu.VMEM((hr + tm, dff2), F32)],
        compiler_params=_cparams(("parallel", "arbitrary")), name="post",
    )(x2d, og, od, hist, w_out, g_ffn, w_up, fcw, fcb, w_down)


def _lane_pad(vec, offset):
    out = jnp.zeros((1, LANES), F32)
    return lax.dynamic_update_slice(out, vec.reshape(1, -1).astype(F32), (0, offset))


def _layer(l, x_prompt, x_sample, state_gdn_conv, state_gdn_s, cache_k, cache_v, page_table, state_ffn_conv, wl):
    (attn_norm_g, w_in, gdn_conv_w, gdn_a_log, gdn_dt_bias, gdn_out_norm_g, diff_q_norm_g, diff_k_norm_g,
     lq1, lk1, lq2, lk2, diff_subln_g, w_out, ffn_norm_g, w_up, ffn_conv_w, ffn_conv_b, w_down) = wl
    nbp, seq, d = x_prompt.shape
    nbs, nts, _ = x_sample.shape
    nh, dk = GDN_HEADS, GDN_DK
    gw = nh * dk
    dw = DIFF_HEADS * DIFF_DV
    lam_init = 0.8 - 0.6 * math.exp(-0.3 * l)

    c_b = 4 * gw
    c_d = c_b + 2 * nh
    w_ba = w_in[:, c_b:c_d]
    w_main = jnp.concatenate([w_in[:, :c_b], w_in[:, c_d:], w_ba,
                              jnp.zeros((d, LANES - 2 * nh), w_in.dtype)], axis=1).astype(BF16)
    w_t = jnp.transpose(jnp.concatenate([w_ba, w_in[:, c_d + 2 * dw:]], axis=1)).astype(BF16)
    reps = dw // DIFF_DQK
    gq_t = jnp.tile(diff_q_norm_g.reshape(1, -1), (1, reps))
    gk_t = jnp.tile(diff_k_norm_g.reshape(1, -1), (1, reps))
    gid = jnp.arange(dw) // DIFF_DQK
    grp = jnp.where(gid[:, None] == gid[None, :], 1.0 / DIFF_DQK, 0.0).astype(BF16)
    g_attn = attn_norm_g.reshape(1, d)
    alr, dtr = _lane_pad(gdn_a_log, nh), _lane_pad(gdn_dt_bias, nh)
    alc = jnp.transpose(alr[:, :SUBLANES])
    dtc = jnp.transpose(dtr[:, :SUBLANES])
    gn = gdn_out_norm_g.reshape(1, dk)
    vecs = [v.reshape(1, -1) for v in (lq1, lk1, lq2, lk2)]
    sg = diff_subln_g.reshape(1, -1)
    w_out_b, w_up_b, w_down_b = w_out.astype(BF16), w_up.astype(BF16), w_down.astype(BF16)
    g_ffn = ffn_norm_g.reshape(1, d)
    fcb = ffn_conv_b.reshape(1, -1)
    dff2 = w_up.shape[1]

    xp2 = x_prompt.reshape(nbp * seq, d)
    tm_p = 256
    (z, gcol, grow, qn, kn, kb, vb, tail_g, knew, vnew, qa, ka, vt) = _inproj_gdn(
        xp2, g_attn, w_main, w_t, gq_t, gk_t, grp, gdn_conv_w, alr, dtr, alc, dtc,
        nb=nbp, tm=tm_p, gw=gw, dw=dw)
    og, s_prompt = _gdn_prompt(qn, kn, kb, vb, gcol, grow, z, gn, nb=nbp, seq=seq, tl=512)
    od = _attn_prompt(qa, ka, vt, *vecs, sg, nb=nbp, seq=seq, tq=512, lam_init=lam_init)
    hr_p = SUBLANES
    y_p, tail_p = _post(xp2, og, od, jnp.zeros((nbp, hr_p, dff2), F32), w_out_b, g_ffn, w_up_b,
                        ffn_conv_w, fcb, w_down_b, nseq=nbp, tm=256, shift=1, hr=hr_p)
    out_p = (y_p.reshape(nbp, seq, d),
             tail_g[:, SUBLANES - (GDN_CONV - 1):, :],
             s_prompt,
             knew.reshape(nbp, seq, DIFF_HEADS, DIFF_DV),
             vnew.reshape(nbp, seq, DIFF_HEADS, DIFF_DV),
             tail_p[:, hr_p - (FFN_CONV - 1):, :])

    xs2 = jnp.transpose(x_sample, (1, 0, 2)).reshape(nts * nbs, d)
    qkv, z, ba, knew, vnew, qa, _, _ = _inproj(xs2, g_attn, w_main, w_t, gq_t, gk_t, grp,
                                               tm=min(256, nts * nbs), gw=gw, dw=dw)
    qkv_tm = qkv.reshape(nts, nbs, 3 * gw)
    hist_tm = jnp.transpose(state_gdn_conv, (1, 0, 2))
    og_tm, s_sample = _gdn_sample(qkv_tm, hist_tm, ba.reshape(nts, nbs, LANES), z.reshape(nts, nbs, gw),
                                  state_gdn_s, gdn_conv_w, alr, dtr, gn, bb=8)
    conv_all = jnp.concatenate([hist_tm, qkv_tm], axis=0)
    conv_s = jnp.transpose(conv_all[nts:], (1, 0, 2))
    to_bm = lambda a: jnp.transpose(a.reshape(nts, nbs, -1), (1, 0, 2))
    q_bm = to_bm(qa)
    q2 = jnp.concatenate([q_bm, q_bm], axis=1)
    pad = jnp.zeros((nbs, nts, dw), F32)
    kn_bm, vn_bm = to_bm(knew), to_bm(vnew)
    od_bm = _attn_sample(page_table, q2, jnp.concatenate([kn_bm, pad], axis=1),
                         jnp.concatenate([vn_bm, pad], axis=1), cache_k, cache_v, *vecs, sg,
                         nt=nts, lam_init=lam_init)
    od_tm = jnp.transpose(od_bm[:, :nts, :], (1, 0, 2)).reshape(nts * nbs, dw).astype(BF16)
    hr_s = (FFN_CONV - 1) * nbs
    hist_f = jnp.transpose(state_ffn_conv, (1, 0, 2)).reshape(1, hr_s, dff2)
    y_s, tail_s = _post(xs2, og_tm.reshape(nts * nbs, gw), od_tm, hist_f, w_out_b, g_ffn, w_up_b,
                        ffn_conv_w, fcb, w_down_b, nseq=1, tm=nbs, shift=nbs, hr=hr_s)
    out_s = (jnp.transpose(y_s.reshape(nts, nbs, d), (1, 0, 2)),
             conv_s,
             s_sample,
             kn_bm.reshape(nbs, nts, DIFF_HEADS, DIFF_DV),
             vn_bm.reshape(nbs, nts, DIFF_HEADS, DIFF_DV),
             jnp.transpose(tail_s.reshape(FFN_CONV - 1, nbs, dff2), (1, 0, 2)))
    return out_p, out_s


def kernel(x_prompt, x_sample, state_gdn_conv, state_gdn_S, cache_k, cache_v, page_table, state_ffn_conv, attn_norm_g, w_in, gdn_conv_w, gdn_A_log, gdn_dt_bias, gdn_out_norm_g, diff_q_norm_g, diff_k_norm_g, diff_lambda_q1, diff_lambda_k1, diff_lambda_q2, diff_lambda_k2, diff_subln_g, w_out, ffn_norm_g, w_up, ffn_conv_w, ffn_conv_b, w_down):
    depth = w_in.shape[0]
    hp, hs = x_prompt, x_sample
    outs_p, outs_s = [], []
    for l in range(depth):
        wl = (attn_norm_g[l], w_in[l], gdn_conv_w[l], gdn_A_log[l], gdn_dt_bias[l], gdn_out_norm_g[l],
              diff_q_norm_g[l], diff_k_norm_g[l], diff_lambda_q1[l], diff_lambda_k1[l], diff_lambda_q2[l],
              diff_lambda_k2[l], diff_subln_g[l], w_out[l], ffn_norm_g[l], w_up[l], ffn_conv_w[l],
              ffn_conv_b[l], w_down[l])
        out_p, out_s = _layer(l, hp, hs, state_gdn_conv[l], state_gdn_S[l], cache_k[l], cache_v[l],
                              page_table, state_ffn_conv[l], wl)
        hp, hs = out_p[0], out_s[0]
        outs_p.append(out_p[1:])
        outs_s.append(out_s[1:])
    stack = lambda outs, i: jnp.stack([o[i] for o in outs])
    return (hp, hs) + tuple(stack(outs_p, i) for i in range(5)) + tuple(stack(outs_s, i) for i in range(5))
```

```python
import functools
import math

import jax
import jax.numpy as jnp
from jax import lax
from jax.experimental import pallas as pl
from jax.experimental.pallas import tpu as pltpu

F32 = jnp.float32
BF16 = jnp.bfloat16
EPS = 1e-6
NEG = -1e30

GDN_HEADS = 4
GDN_DK = 128
GDN_CONV = 4
GDN_CHUNK = 64
DIFF_HEADS = 4
DIFF_DV = 128
DIFF_DQK = 64
FFN_CONV = 3
LANES = 128
SUBLANES = 8
VMEM_LIMIT = 56 * 1024 * 1024


def _cparams(sem):
    return pltpu.CompilerParams(dimension_semantics=sem, vmem_limit_bytes=VMEM_LIMIT)


def _const_spec(shape):
    nd = len(shape)
    return pl.BlockSpec(shape, lambda *_: (0,) * nd, pipeline_mode=pl.Buffered(1))


def _dot(a, b):
    return jnp.dot(a, b, preferred_element_type=F32)


def _dot_nt(a, b):
    return lax.dot_general(a, b, (((1,), (1,)), ((), ())), preferred_element_type=F32)


def _dot_tn(a, b):
    return lax.dot_general(a, b, (((0,), (0,)), ((), ())), preferred_element_type=F32)


def _softplus(x):
    return jnp.maximum(x, 0.0) + jnp.log1p(jnp.exp(-jnp.abs(x)))


def _silu(x):
    return x * jax.nn.sigmoid(x)


def _split3(x):
    hi = x.astype(BF16)
    r = x - hi.astype(F32)
    mid = r.astype(BF16)
    lo = (r - mid.astype(F32)).astype(BF16)
    return hi, mid, lo


def _inproj_core(x_ref, g_ref, wm_ref, wt_ref, gq_ref, gk_ref, grp_ref,
                 knew_ref, vnew_ref, qa_ref, ka_ref, vt_ref, gw, dw):
    tm = x_ref.shape[0]
    x = x_ref[...]
    xn = x * lax.rsqrt(jnp.mean(x * x, axis=-1, keepdims=True) + EPS) * g_ref[...]
    xb = xn.astype(BF16)
    proj = _dot(xb, wm_ref[...])
    tr = _dot_nt(wt_ref[...], xb)
    vt_ref[...] = tr[SUBLANES:].astype(BF16)
    c0 = 4 * gw
    dq = proj[:, c0:c0 + dw]
    dk = proj[:, c0 + dw:c0 + 2 * dw]
    dv = proj[:, c0 + 2 * dw:c0 + 3 * dw]
    grp = grp_ref[...]
    msq = _dot((dq * dq).astype(BF16), grp)
    msk = _dot((dk * dk).astype(BF16), grp)
    dqn = dq * lax.rsqrt(msq + EPS) * gq_ref[...]
    dkn = dk * lax.rsqrt(msk + EPS) * gk_ref[...]
    for h in range(DIFF_HEADS):
        head_rows = pl.ds(h, tm, stride=DIFF_HEADS)
        knew_ref[head_rows, :] = dkn[:, h * DIFF_DV:(h + 1) * DIFF_DV]
        vnew_ref[head_rows, :] = dv[:, h * DIFF_DV:(h + 1) * DIFF_DV]
    qa_ref[...] = (dqn * (DIFF_DQK ** -0.5 * math.log2(math.e))).astype(BF16)
    ka_ref[...] = dkn.astype(BF16)
    return proj, tr[:SUBLANES]


def _inproj_kernel(x_ref, g_ref, wm_ref, wt_ref, gq_ref, gk_ref, grp_ref,
                   qkv_ref, z_ref, ba_ref, knew_ref, vnew_ref, qa_ref, ka_ref, vt_ref, *, gw, dw):
    proj, _ = _inproj_core(x_ref, g_ref, wm_ref, wt_ref, gq_ref, gk_ref, grp_ref,
                           knew_ref, vnew_ref, qa_ref, ka_ref, vt_ref, gw, dw)
    qkv_ref[...] = proj[:, :3 * gw]
    z_ref[...] = proj[:, 3 * gw:4 * gw]
    ba_ref[...] = proj[:, 4 * gw + 3 * dw:4 * gw + 3 * dw + LANES]


def _inproj_gdn_kernel(x_ref, g_ref, wm_ref, wt_ref, gq_ref, gk_ref, grp_ref,
                       cw_ref, alr_ref, dtr_ref, alc_ref, dtc_ref,
                       z_ref, gcol_ref, grow_ref, qn_ref, kn_ref, kb_ref, vb_ref, tail_ref,
                       knew_ref, vnew_ref, qa_ref, ka_ref, vt_ref, xbuf, *, gw, dw):
    i = pl.program_id(1)
    tm = x_ref.shape[0]
    nh, dk = GDN_HEADS, GDN_DK
    hr = SUBLANES

    @pl.when(i == 0)
    def _():
        xbuf[0:hr, :] = jnp.zeros((hr, 3 * gw), F32)

    proj, trow = _inproj_core(x_ref, g_ref, wm_ref, wt_ref, gq_ref, gk_ref, grp_ref,
                              knew_ref, vnew_ref, qa_ref, ka_ref, vt_ref, gw, dw)
    z_ref[...] = proj[:, 3 * gw:4 * gw]
    xbuf[hr:hr + tm, :] = proj[:, :3 * gw]
    cw = cw_ref[...]
    y = cw[GDN_CONV - 1:GDN_CONV, :] * xbuf[hr:hr + tm, :]
    for j in range(1, GDN_CONV):
        y = y + cw[GDN_CONV - 1 - j:GDN_CONV - j, :] * xbuf[hr - j:hr - j + tm, :]
    tail = xbuf[tm:tm + hr, :]
    tail_ref[...] = tail
    xbuf[0:hr, :] = tail
    y = _silu(y)
    ba = proj[:, 4 * gw + 3 * dw:4 * gw + 3 * dw + LANES]
    beta_c = jax.nn.sigmoid(ba)
    gcol_ref[...] = -jnp.exp(alr_ref[...]) * _softplus(ba + dtr_ref[...])
    grow_ref[...] = -jnp.exp(alc_ref[...]) * _softplus(trow + dtc_ref[...])
    for h in range(nh):
        ls = slice(h * dk, (h + 1) * dk)
        q = y[:, h * dk:(h + 1) * dk]
        k = y[:, gw + h * dk:gw + (h + 1) * dk]
        v = y[:, 2 * gw + h * dk:2 * gw + (h + 1) * dk]
        qn = q * lax.rsqrt(jnp.sum(q * q, axis=-1, keepdims=True) + EPS) * (dk ** -0.5)
        kn = k * lax.rsqrt(jnp.sum(k * k, axis=-1, keepdims=True) + EPS)
        beta = jnp.broadcast_to(beta_c[:, h:h + 1], (tm, dk))
        qn_ref[:, ls] = qn.astype(BF16)
        kn_ref[:, ls] = kn.astype(BF16)
        kb_ref[:, ls] = (kn * beta).astype(BF16)
        vb_ref[:, ls] = (v * beta).astype(BF16)


def _attn_out(n, tm, dw):
    return (
        jax.ShapeDtypeStruct((n * DIFF_HEADS, DIFF_DV), F32),
        jax.ShapeDtypeStruct((n * DIFF_HEADS, DIFF_DV), F32),
        jax.ShapeDtypeStruct((n, dw), BF16),
        jax.ShapeDtypeStruct((n, dw), BF16),
        jax.ShapeDtypeStruct((n // tm, dw, tm), BF16),
    )


def _inproj(x2d, g_attn, w_main, w_t, gq_t, gk_t, grp, *, tm, gw, dw):
    n, d = x2d.shape
    row = lambda i: (i, 0)
    out_shape = (
        jax.ShapeDtypeStruct((n, 3 * gw), F32),
        jax.ShapeDtypeStruct((n, gw), F32),
        jax.ShapeDtypeStruct((n, LANES), F32),
    ) + _attn_out(n, tm, dw)
    out_specs = (
        pl.BlockSpec((tm, 3 * gw), row), pl.BlockSpec((tm, gw), row), pl.BlockSpec((tm, LANES), row),
        pl.BlockSpec((tm * DIFF_HEADS, DIFF_DV), row), pl.BlockSpec((tm * DIFF_HEADS, DIFF_DV), row),
        pl.BlockSpec((tm, dw), row), pl.BlockSpec((tm, dw), row),
        pl.BlockSpec((None, dw, tm), lambda i: (i, 0, 0)),
    )
    in_specs = [
        pl.BlockSpec((tm, d), row), _const_spec((1, d)), _const_spec(w_main.shape),
        _const_spec(w_t.shape), _const_spec((1, dw)), _const_spec((1, dw)), _const_spec((dw, dw)),
    ]
    return pl.pallas_call(
        functools.partial(_inproj_kernel, gw=gw, dw=dw),
        grid=(n // tm,), in_specs=in_specs, out_specs=out_specs, out_shape=out_shape,
        compiler_params=_cparams(("parallel",)), name="inproj",
    )(x2d, g_attn, w_main, w_t, gq_t, gk_t, grp)


def _inproj_gdn(x2d, g_attn, w_main, w_t, gq_t, gk_t, grp, cw, alr, dtr, alc, dtc, *, nb, tm, gw, dw):
    n, d = x2d.shape
    nt = n // (nb * tm)
    row = lambda b, i: (b * nt + i, 0)
    out_shape = (
        jax.ShapeDtypeStruct((n, gw), F32),
        jax.ShapeDtypeStruct((n, LANES), F32),
        jax.ShapeDtypeStruct((SUBLANES, n), F32),
        jax.ShapeDtypeStruct((n, gw), BF16),
        jax.ShapeDtypeStruct((n, gw), BF16),
        jax.ShapeDtypeStruct((n, gw), BF16),
        jax.ShapeDtypeStruct((n, gw), BF16),
        jax.ShapeDtypeStruct((nb, SUBLANES, 3 * gw), F32),
    ) + _attn_out(n, tm, dw)
    out_specs = (
        pl.BlockSpec((tm, gw), row), pl.BlockSpec((tm, LANES), row),
        pl.BlockSpec((SUBLANES, tm), lambda b, i: (0, b * nt + i)),
        pl.BlockSpec((tm, gw), row), pl.BlockSpec((tm, gw), row), pl.BlockSpec((tm, gw), row),
        pl.BlockSpec((tm, gw), row),
        pl.BlockSpec((None, SUBLANES, 3 * gw), lambda b, i: (b, 0, 0)),
        pl.BlockSpec((tm * DIFF_HEADS, DIFF_DV), row), pl.BlockSpec((tm * DIFF_HEADS, DIFF_DV), row),
        pl.BlockSpec((tm, dw), row), pl.BlockSpec((tm, dw), row),
        pl.BlockSpec((None, dw, tm), lambda b, i: (b * nt + i, 0, 0)),
    )
    in_specs = [
        pl.BlockSpec((tm, d), row), _const_spec((1, d)), _const_spec(w_main.shape),
        _const_spec(w_t.shape), _const_spec((1, dw)), _const_spec((1, dw)), _const_spec((dw, dw)),
        _const_spec(cw.shape), _const_spec((1, LANES)), _const_spec((1, LANES)),
        _const_spec((SUBLANES, 1)), _const_spec((SUBLANES, 1)),
    ]
    return pl.pallas_call(
        functools.partial(_inproj_gdn_kernel, gw=gw, dw=dw),
        grid=(nb, nt), in_specs=in_specs, out_specs=out_specs, out_shape=out_shape,
        scratch_shapes=[pltpu.VMEM((tm + SUBLANES, 3 * gw), F32)],
        compiler_params=_cparams(("parallel", "arbitrary")), name="inproj_gdn",
    )(x2d, g_attn, w_main, w_t, gq_t, gk_t, grp, cw, alr, dtr, alc, dtc)


def _gdn_prompt_kernel(qn_ref, kn_ref, kb_ref, vb_ref, gcol_ref, grow_ref, z_ref, gn_ref,
                       og_ref, sout_ref, s_scr, *, tl):
    i = pl.program_id(1)
    nh, dk, c = GDN_HEADS, GDN_DK, GDN_CHUNK

    @pl.when(i == 0)
    def _():
        s_scr[...] = jnp.zeros_like(s_scr)

    g_c = gcol_ref[...]
    g_rep = jnp.concatenate([jnp.broadcast_to(g_c[:, nh + h:nh + h + 1], (tl, dk)) for h in range(nh)], axis=1)
    g_r = grow_ref[...]

    ii = lax.broadcasted_iota(jnp.int32, (tl, tl), 0)
    jj = lax.broadcasted_iota(jnp.int32, (tl, tl), 1)
    same = lax.shift_right_logical(ii, 6) == lax.shift_right_logical(jj, 6)
    tri = jnp.where(same & (ii >= jj), 1.0, 0.0).astype(BF16)
    triu = jnp.where(same & (ii <= jj), 1.0, 0.0).astype(BF16)
    gc_rep = sum(_dot(tri, p) for p in _split3(g_rep))
    gc_r = sum(_dot(p, triu) for p in _split3(g_r))

    pr = 2 * c
    pi = lax.broadcasted_iota(jnp.int32, (pr, pr), 0)
    pj = lax.broadcasted_iota(jnp.int32, (pr, pr), 1)
    psame = lax.shift_right_logical(pi, 6) == lax.shift_right_logical(pj, 6)
    lower = psame & (pi >= pj)
    strict = psame & (pi > pj)
    eye = jnp.where(pi == pj, 1.0, 0.0).astype(F32)
    zeros_c = jnp.zeros((c, dk), F32)

    npair = tl // pr
    hp = [(h, p) for h in range(nh) for p in range(npair)]
    def tile(ref, h, p):
        return ref[p * pr:(p + 1) * pr, h * dk:(h + 1) * dk]

    rows = lambda arr, p: arr[p * pr:(p + 1) * pr]
    gcp = {(h, p): rows(gc_rep[:, h * dk:(h + 1) * dk], p) for h, p in hp}
    dec = {(h, p): jnp.exp(jnp.where(lower, gcp[h, p] - gc_r[nh + h:nh + h + 1, p * pr:(p + 1) * pr], NEG))
           for h, p in hp}
    kpb = {(h, p): tile(kn_ref, h, p) for h, p in hp}
    kbp = {(h, p): tile(kb_ref, h, p) for h, p in hp}
    m = {x: -jnp.where(strict, _dot_nt(kbp[x], kpb[x]) * dec[x], 0.0) for x in hp}
    a = {x: eye + m[x] for x in hp}
    pw = {x: _dot(m[x].astype(BF16), m[x].astype(BF16)) for x in hp}
    for it in range(1, 6):
        for x in hp:
            pwb = pw[x].astype(BF16)
            if it < 5:
                res = _dot(pwb, jnp.concatenate([a[x].astype(BF16), pwb], axis=1))
                a[x] = a[x] + res[:, :pr]
                pw[x] = res[:, pr:]
            else:
                a[x] = a[x] + _dot(pwb, a[x].astype(BF16))
    egc = {x: jnp.exp(gcp[x]) for x in hp}
    sol = {x: _dot(a[x].astype(BF16),
                   jnp.concatenate([tile(vb_ref, *x), (kbp[x].astype(F32) * egc[x]).astype(BF16)], axis=1))
           for x in hp}
    qk = {x: (_dot_nt(tile(qn_ref, *x), kpb[x]) * dec[x]).astype(BF16) for x in hp}
    qg = {x: tile(qn_ref, *x).astype(F32) * egc[x] for x in hp}

    s_cur = [s_scr[h] for h in range(nh)]
    o_rows = [[] for _ in range(nh)]
    for p in range(npair):
        for cc in range(2):
            c0 = cc * c
            ws = []
            for h in range(nh):
                wq = jnp.concatenate([sol[h, p][c0:c0 + c, dk:], qg[h, p][c0:c0 + c]], axis=0)
                ws.append(_dot(wq.astype(BF16), s_cur[h].astype(BF16)))
            for h in range(nh):
                g_c0 = gcp[h, p][c0:c0 + c]
                glast = g_c0[c - 1:c, :]
                kd = kpb[h, p][c0:c0 + c].astype(F32) * jnp.exp(glast - g_c0)
                v_new = (sol[h, p][c0:c0 + c, :dk] - ws[h][:c]).astype(BF16)
                s_cur[h] = s_cur[h] * jnp.exp(glast) + _dot_tn(kd.astype(BF16), v_new)
                zc = jnp.zeros_like(v_new)
                vn_pad = jnp.concatenate([v_new, zc] if cc == 0 else [zc, v_new], axis=0)
                o_rows[h].append(ws[h][c:] + _dot(qk[h, p][c0:c0 + c], vn_pad))
    for h in range(nh):
        s_scr[h] = s_cur[h]
        o = jnp.concatenate(o_rows[h], axis=0)
        o = o * lax.rsqrt(jnp.mean(o * o, axis=-1, keepdims=True) + EPS) * gn_ref[...]
        og_ref[:, h * dk:(h + 1) * dk] = (o * _silu(z_ref[:, h * dk:(h + 1) * dk])).astype(og_ref.dtype)

    @pl.when(i == pl.num_programs(1) - 1)
    def _():
        sout_ref[...] = s_scr[...]


def _gdn_prompt(qn, kn, kb, vb, gcol, grow, z, gn, *, nb, seq, tl):
    nh, dk = GDN_HEADS, GDN_DK
    gw = nh * dk
    nt = seq // tl
    row = lambda b, i: (b * nt + i, 0)
    wide = pl.BlockSpec((tl, gw), row)
    in_specs = [
        wide, wide, wide, wide, pl.BlockSpec((tl, LANES), row),
        pl.BlockSpec((SUBLANES, tl), lambda b, i: (0, b * nt + i)),
        wide, _const_spec((1, dk)),
    ]
    out_shape = (jax.ShapeDtypeStruct((nb * seq, gw), BF16),
                 jax.ShapeDtypeStruct((nb, nh, dk, dk), F32))
    out_specs = (pl.BlockSpec((tl, gw), row),
                 pl.BlockSpec((None, nh, dk, dk), lambda b, i: (b, 0, 0, 0)))
    return pl.pallas_call(
        functools.partial(_gdn_prompt_kernel, tl=tl),
        grid=(nb, nt), in_specs=in_specs, out_specs=out_specs, out_shape=out_shape,
        scratch_shapes=[pltpu.VMEM((nh, dk, dk), F32)],
        compiler_params=_cparams(("parallel", "arbitrary")), name="gdn_prompt",
    )(qn, kn, kb, vb, gcol, grow, z, gn)


def _gdn_sample_pre_kernel(qkv_ref, hist_ref, ba_ref, cw_ref, alr_ref, dtr_ref, r_ref, intra_ref, *, nt):
    nh, dk = GDN_HEADS, GDN_DK
    gw = nh * dk
    nb = qkv_ref.shape[1]
    cw = cw_ref[...]
    nhist = GDN_CONV - 1
    xp = [hist_ref[j] for j in range(nhist)] + [qkv_ref[t] for t in range(nt)]
    ys = []
    for t in range(nt):
        y = cw[0:1, :] * xp[t]
        for j in range(1, GDN_CONV):
            y = y + cw[j:j + 1, :] * xp[t + j]
        ys.append(_silu(y))
    beta, gc = [], []
    for t in range(nt):
        ba = ba_ref[t]
        beta_c = jax.nn.sigmoid(ba)
        g_c = -jnp.exp(alr_ref[...]) * _softplus(ba + dtr_ref[...])
        beta.append([jnp.broadcast_to(beta_c[:, h:h + 1], (nb, dk)) for h in range(nh)])
        g_t = [jnp.broadcast_to(g_c[:, nh + h:nh + h + 1], (nb, dk)) for h in range(nh)]
        gc.append(g_t if t == 0 else [gc[t - 1][h] + g_t[h] for h in range(nh)])
    zrow = jnp.zeros((nb, dk), F32)
    for h in range(nh):
        q, k, v = [], [], []
        for t in range(nt):
            qt = ys[t][:, h * dk:(h + 1) * dk]
            kt = ys[t][:, gw + h * dk:gw + (h + 1) * dk]
            q.append(qt * lax.rsqrt(jnp.sum(qt * qt, axis=-1, keepdims=True) + EPS) * (dk ** -0.5))
            k.append(kt * lax.rsqrt(jnp.sum(kt * kt, axis=-1, keepdims=True) + EPS))
            v.append(ys[t][:, 2 * gw + h * dk:2 * gw + (h + 1) * dk])
        g = [gc[t][h] for t in range(nt)]
        us, ws = [], []
        for t in range(nt):
            b_t = beta[t][h]
            u_t = v[t] * b_t
            w_t = k[t] * b_t * jnp.exp(g[t])
            for s in range(t):
                m_ts = b_t * jnp.sum(k[t] * k[s], axis=-1, keepdims=True) * jnp.exp(g[t] - g[s])
                u_t = u_t - m_ts * us[s]
                w_t = w_t - m_ts * ws[s]
            us.append(u_t)
            ws.append(w_t)
        lane = slice(h * dk, (h + 1) * dk)
        for t in range(nt):
            r_ref[t, :, lane] = ws[t]
            r_ref[nt + t, :, lane] = q[t] * jnp.exp(g[t])
            r_ref[2 * nt + t, :, lane] = us[t]
            r_ref[3 * nt + t, :, lane] = k[t] * jnp.exp(g[nt - 1] - g[t])
            for s in range(nt):
                if s <= t:
                    intra_ref[t * nt + s, :, lane] = (jnp.sum(q[t] * k[s], axis=-1, keepdims=True)
                                                       * jnp.exp(g[t] - g[s]))
                else:
                    intra_ref[t * nt + s, :, lane] = zrow
        r_ref[4 * nt, :, lane] = jnp.exp(g[nt - 1])
        for r in range(4 * nt + 1, r_ref.shape[0]):
            r_ref[r, :, lane] = zrow


def _gdn_sample_state_kernel(r_ref, s_ref, ws_ref, sout_ref, *, nt, bb):
    nh = GDN_HEADS
    rows = lax.broadcasted_iota(jnp.int32, (2 * nt, GDN_DK), 0)
    inst = [(bi, h) for bi in range(bb) for h in range(nh)]
    ws = {x: _dot(r_ref[x[0], x[1], 0:2 * nt, :].astype(BF16), s_ref[x].astype(BF16)) for x in inst}
    for x in inst:
        ud = r_ref[x[0], x[1], 2 * nt:4 * nt, :]
        vn = jnp.where(rows < nt, ud - ws[x], 0.0)
        kd = jnp.where(rows < nt, pltpu.roll(ud, nt, axis=0), 0.0)
        ws_ref[x] = ws[x]
        sout_ref[x] = (s_ref[x] * r_ref[x[0], x[1], 4 * nt:4 * nt + 1, :]
                       + _dot_tn(kd.astype(BF16), vn.astype(BF16)))


def _gdn_sample_post_kernel(ws_ref, r_ref, intra_ref, z_ref, gn_ref, og_ref, *, nt):
    nh, dk = GDN_HEADS, GDN_DK
    for h in range(nh):
        lane = slice(h * dk, (h + 1) * dk)
        vn = [r_ref[2 * nt + t, :, lane] - ws_ref[t, :, lane] for t in range(nt)]
        for t in range(nt):
            o = ws_ref[nt + t, :, lane]
            for s in range(t + 1):
                o = o + intra_ref[t * nt + s, :, lane] * vn[s]
            o = o * lax.rsqrt(jnp.mean(o * o, axis=-1, keepdims=True) + EPS) * gn_ref[...]
            og_ref[t, :, lane] = (o * _silu(z_ref[t, :, lane])).astype(og_ref.dtype)


def _gdn_sample(qkv_tm, hist_tm, ba_tm, z_tm, state_s, cw, alr, dtr, gn, *, bb):
    nt, nb, _ = qkv_tm.shape
    nh, dk = GDN_HEADS, GDN_DK
    gw = nh * dk
    nr = 6 * nt
    full = lambda shape: pl.BlockSpec(shape, lambda *_: (0,) * len(shape))
    r_tm, intra = pl.pallas_call(
        functools.partial(_gdn_sample_pre_kernel, nt=nt),
        grid=(1,),
        in_specs=[full(qkv_tm.shape), full(hist_tm.shape), full(ba_tm.shape), full(cw.shape),
                  full(alr.shape), full(dtr.shape)],
        out_specs=(full((nr, nb, gw)), full((nt * nt, nb, gw))),
        out_shape=(jax.ShapeDtypeStruct((nr, nb, gw), F32), jax.ShapeDtypeStruct((nt * nt, nb, gw), F32)),
        compiler_params=_cparams(("arbitrary",)), name="gdn_sample_pre",
    )(qkv_tm, hist_tm, ba_tm, cw, alr, dtr)
    r_bm = jnp.transpose(r_tm.reshape(nr, nb, nh, dk), (1, 2, 0, 3))
    blk = lambda b: (b, 0, 0, 0)
    ws_bm, s_new = pl.pallas_call(
        functools.partial(_gdn_sample_state_kernel, nt=nt, bb=bb),
        grid=(nb // bb,),
        in_specs=[pl.BlockSpec((bb, nh, nr, dk), blk), pl.BlockSpec((bb, nh, dk, dk), blk)],
        out_specs=(pl.BlockSpec((bb, nh, 2 * nt, dk), blk), pl.BlockSpec((bb, nh, dk, dk), blk)),
        out_shape=(jax.ShapeDtypeStruct((nb, nh, 2 * nt, dk), F32), jax.ShapeDtypeStruct((nb, nh, dk, dk), F32)),
        compiler_params=_cparams(("parallel",)), name="gdn_sample_state",
    )(r_bm, state_s)
    ws_tm = jnp.transpose(ws_bm, (2, 0, 1, 3)).reshape(2 * nt, nb, gw)
    og = pl.pallas_call(
        functools.partial(_gdn_sample_post_kernel, nt=nt),
        grid=(1,),
        in_specs=[full(ws_tm.shape), full(r_tm.shape), full(intra.shape), full(z_tm.shape), full(gn.shape)],
        out_specs=full((nt, nb, gw)),
        out_shape=jax.ShapeDtypeStruct((nt, nb, gw), BF16),
        compiler_params=_cparams(("arbitrary",)), name="gdn_sample_post",
    )(ws_tm, r_tm, intra, z_tm, gn)
    return og, s_new


def _lambda(lq1_ref, lk1_ref, lq2_ref, lk2_ref, lam_init):
    s1 = jnp.sum(lq1_ref[...] * lk1_ref[...], axis=-1, keepdims=True)
    s2 = jnp.sum(lq2_ref[...] * lk2_ref[...], axis=-1, keepdims=True)
    return jnp.exp(s1) - jnp.exp(s2) + lam_init


def _subln(o, g_ref, lam_init):
    return o * lax.rsqrt(jnp.mean(o * o, axis=-1, keepdims=True) + EPS) * g_ref[...] * (1.0 - lam_init)


def _attn_prompt_kernel(q_ref, k_ref, vt_ref, lq1_ref, lk1_ref, lq2_ref, lk2_ref, sg_ref, o_ref,
                        sa_scr, sb_scr, m_scr, acc_scr, *, tq, lam_init):
    qi = pl.program_id(1)
    nh, dv, dqk = DIFF_HEADS, DIFF_DV, DIFF_DQK
    lam = _lambda(lq1_ref, lk1_ref, lq2_ref, lk2_ref, lam_init)
    lane = lax.broadcasted_iota(jnp.int32, (tq, dv), 1)
    kidx = lax.broadcasted_iota(jnp.int32, (tq, tq), 0)
    qidx = lax.broadcasted_iota(jnp.int32, (tq, tq), 1)
    causal = kidx <= qidx
    nsum = 2 * SUBLANES
    ones_rows = jnp.ones((nsum, tq), BF16)
    lanes = [slice(h * dv, (h + 1) * dv) for h in range(nh)]
    qs = []
    for h in range(nh):
        q = q_ref[:, lanes[h]]
        zero = jnp.zeros_like(q)
        qs.append((jnp.where(lane < dqk, q, zero), jnp.where(lane >= dqk, q, zero)))

    nvt = tq // vt_ref.shape[-1]
    nchain = 2 * nh

    def scores(dst, j):
        for h in range(nh):
            kb = k_ref[pl.ds(pl.multiple_of(j * tq, tq), tq), lanes[h]]
            for c in range(2):
                dst[2 * h + c] = _dot_nt(kb, qs[h][c])

    def softmax_pv(src, j, masked):
        ps, alphas = [], []
        for x in range(nchain):
            s_x = jnp.where(causal, src[x], NEG) if masked else src[x]
            m_i = m_scr[x]
            m_new = jnp.maximum(m_i, jnp.max(s_x, axis=0, keepdims=True))
            m_scr[x] = m_new
            alphas.append(jnp.exp2(m_i - m_new))
            ps.append(jnp.exp2(s_x - m_new).astype(BF16))
        for h in range(nh):
            vt = jnp.concatenate([vt_ref[j * nvt + t, lanes[h], :] for t in range(nvt)], axis=1)
            vt = jnp.concatenate([vt, ones_rows], axis=0)
            for c in range(2):
                x = 2 * h + c
                acc_scr[x] = alphas[x] * acc_scr[x] + _dot(vt, ps[x])

    m_scr[...] = jnp.full(m_scr.shape, NEG, F32)
    acc_scr[...] = jnp.zeros(acc_scr.shape, F32)
    scores(sa_scr, 0)

    def two_blocks(i, _):
        scores(sb_scr, 2 * i + 1)
        softmax_pv(sa_scr, 2 * i, False)
        scores(sa_scr, 2 * i + 2)
        softmax_pv(sb_scr, 2 * i + 1, False)
        return 0

    lax.fori_loop(0, lax.shift_right_logical(qi, 1), two_blocks, 0)
    odd = lax.rem(qi, 2) == 1

    @pl.when(odd)
    def _():
        scores(sb_scr, qi)
        softmax_pv(sa_scr, qi - 1, False)
        softmax_pv(sb_scr, qi, True)

    @pl.when(jnp.logical_not(odd))
    def _():
        softmax_pv(sa_scr, qi, True)

    for h in range(nh):
        ls = lanes[h]
        a1, a2 = acc_scr[2 * h], acc_scr[2 * h + 1]
        ot = a1[:dv] / a1[dv:dv + 1] - lam * (a2[:dv] / a2[dv:dv + 1])
        ot = ot * lax.rsqrt(jnp.mean(ot * ot, axis=0, keepdims=True) + EPS) * (1.0 - lam_init)
        o_ref[:, ls] = (jnp.transpose(ot) * sg_ref[...]).astype(o_ref.dtype)


def _attn_prompt(qa, ka, vt, lq1, lk1, lq2, lk2, sg, *, nb, seq, tq, lam_init):
    nh, dv = DIFF_HEADS, DIFF_DV
    wd = nh * dv
    nq = seq // tq
    qa3, ka3 = qa.reshape(nb, seq, wd), ka.reshape(nb, seq, wd)
    tv = vt.shape[-1]
    nv = seq // tv
    vt4 = vt.reshape(nb, nv, wd, tv)
    vec = _const_spec((1, DIFF_DQK))
    out = pl.pallas_call(
        functools.partial(_attn_prompt_kernel, tq=tq, lam_init=lam_init),
        grid=(nb, nq),
        in_specs=[pl.BlockSpec((None, tq, wd), lambda b, i: (b, i, 0)),
                  pl.BlockSpec((None, seq, wd), lambda b, i: (b, 0, 0)),
                  pl.BlockSpec((None, nv, wd, tv), lambda b, i: (b, 0, 0, 0)),
                  vec, vec, vec, vec, _const_spec((1, dv))],
        out_specs=pl.BlockSpec((None, tq, wd), lambda b, i: (b, i, 0)),
        out_shape=jax.ShapeDtypeStruct((nb, seq, wd), BF16),
        scratch_shapes=[pltpu.VMEM((2 * nh, tq, tq), F32), pltpu.VMEM((2 * nh, tq, tq), F32),
                        pltpu.VMEM((2 * nh, 1, tq), F32), pltpu.VMEM((2 * nh, dv + 2 * SUBLANES, tq), F32)],
        compiler_params=_cparams(("parallel", "arbitrary")), name="attn_prompt",
    )(qa3, ka3, vt4, lq1, lk1, lq2, lk2, sg)
    return out.reshape(nb * seq, wd)


def _attn_sample_kernel(pt_ref, q_ref, kn_ref, vn_ref, *rest, n_pages, nt, lam_init):
    k_refs = rest[:n_pages]
    v_refs = rest[n_pages:2 * n_pages]
    lq1_ref, lk1_ref, lq2_ref, lk2_ref, sg_ref, o_ref = rest[2 * n_pages:]
    del pt_ref
    nh, dv, dqk = DIFF_HEADS, DIFF_DV, DIFF_DQK
    page = k_refs[0].shape[0] // nh
    nr = 2 * nt
    lam = _lambda(lq1_ref, lk1_ref, lq2_ref, lk2_ref, lam_init)
    row = lax.broadcasted_iota(jnp.int32, (nr, dv), 0)
    lane = lax.broadcasted_iota(jnp.int32, (nr, dv), 1)
    first = row < nt
    keep = jnp.logical_xor(lane >= dqk, first)
    tpos = jnp.where(first, row, row - nt)
    new_ok = (lane < nt) & (lane <= tpos)
    zpad = jnp.zeros((page - nr, dv), F32)
    lanes = [slice(h * dv, (h + 1) * dv) for h in range(nh)]
    head_rows = [pl.ds(h, page, stride=nh) for h in range(nh)]
    groups = [list(range(j, min(j + 2, n_pages))) for j in range(0, n_pages, 2)]

    def past(refs, h, grp):
        return jnp.concatenate([refs[j][head_rows[h], :] for j in grp], axis=0).astype(BF16)

    s_all = []
    for h in range(nh):
        qh = q_ref[:, lanes[h]]
        qz = jnp.where(keep, qh, jnp.zeros_like(qh))
        s_h = [_dot_nt(qz, past(k_refs, h, grp)) for grp in groups]
        knew = jnp.concatenate([kn_ref[:, lanes[h]], zpad], axis=0)
        s_h.append(jnp.where(new_ok, _dot_nt(qz, knew.astype(BF16)), NEG))
        s_all.append(s_h)
    p_all, l_all = [], []
    for h in range(nh):
        m = jnp.max(s_all[h][0], axis=-1, keepdims=True)
        for s in s_all[h][1:]:
            m = jnp.maximum(m, jnp.max(s, axis=-1, keepdims=True))
        ps = [jnp.exp2(s - m) for s in s_all[h]]
        l = jnp.sum(ps[0], axis=-1, keepdims=True)
        for p in ps[1:]:
            l = l + jnp.sum(p, axis=-1, keepdims=True)
        p_all.append([p.astype(BF16) for p in ps])
        l_all.append(l)
    for h in range(nh):
        acc = _dot(p_all[h][-1], jnp.concatenate([vn_ref[:, lanes[h]], zpad], axis=0).astype(BF16))
        for gi, grp in enumerate(groups):
            acc = acc + _dot(p_all[h][gi], past(v_refs, h, grp))
        o2 = acc / l_all[h]
        o = o2 - lam * pltpu.roll(o2, nt, axis=0)
        o_ref[:, lanes[h]] = _subln(o, sg_ref, lam_init)


def _attn_sample(page_table, q2, kn8, vn8, cache_k, cache_v, lq1, lk1, lq2, lk2, sg, *, nt, lam_init):
    nb, n_pages = page_table.shape
    nh, dv = DIFF_HEADS, DIFF_DV
    page = cache_k.shape[1]
    wd = nh * dv
    ck = cache_k.reshape(cache_k.shape[0], page * nh, dv)
    cv = cache_v.reshape(cache_v.shape[0], page * nh, dv)
    nr = 2 * nt
    small = pl.BlockSpec((None, nr, wd), lambda b, pt: (b, 0, 0))

    def page_spec(j):
        return pl.BlockSpec((None, page * nh, dv), lambda b, pt: (pt[b, j], 0, 0))

    vec = pl.BlockSpec((1, DIFF_DQK), lambda b, pt: (0, 0))
    grid_spec = pltpu.PrefetchScalarGridSpec(
        num_scalar_prefetch=1, grid=(nb,),
        in_specs=[small, small, small] + [page_spec(j) for j in range(n_pages)] * 2
        + [vec, vec, vec, vec, pl.BlockSpec((1, dv), lambda b, pt: (0, 0))],
        out_specs=small)
    return pl.pallas_call(
        functools.partial(_attn_sample_kernel, n_pages=n_pages, nt=nt, lam_init=lam_init),
        grid_spec=grid_spec, out_shape=jax.ShapeDtypeStruct((nb, nr, wd), F32),
        compiler_params=_cparams(("parallel",)), name="attn_sample",
    )(page_table, q2, kn8, vn8, *([ck] * n_pages), *([cv] * n_pages), lq1, lk1, lq2, lk2, sg)


def _post_kernel(x_ref, og_ref, od_ref, hist_ref, wo_ref, gf_ref, wu_ref, fcw_ref, fcb_ref, wd_ref,
                 y_ref, tail_ref, ubuf, *, tm, shift, hr, dff):
    i = pl.program_id(1)
    gw = og_ref.shape[1]

    @pl.when(i == 0)
    def _():
        ubuf[0:hr, :] = hist_ref[...]

    hres = x_ref[...] + _dot(og_ref[...], wo_ref[0:gw, :]) + _dot(od_ref[...], wo_ref[gw:, :])
    hn = hres * lax.rsqrt(jnp.mean(hres * hres, axis=-1, keepdims=True) + EPS) * gf_ref[...]
    ubuf[hr:hr + tm, :] = _dot(hn.astype(BF16), wu_ref[...])
    fcw = fcw_ref[...]
    u = fcw[FFN_CONV - 1:FFN_CONV, :] * ubuf[hr:hr + tm, :] + fcb_ref[...]
    for j in range(1, FFN_CONV):
        u = u + fcw[FFN_CONV - 1 - j:FFN_CONV - j, :] * ubuf[hr - j * shift:hr - j * shift + tm, :]
    tail = ubuf[tm:tm + hr, :]
    tail_ref[...] = tail
    ubuf[0:hr, :] = tail
    act = (_silu(u[:, :dff]) * u[:, dff:]).astype(BF16)
    y_ref[...] = hres + _dot(act, wd_ref[...])


def _post(x2d, og, od, hist, w_out, g_ffn, w_up, fcw, fcb, w_down, *, nseq, tm, shift, hr):
    n, d = x2d.shape
    gw = og.shape[1]
    dff2 = w_up.shape[1]
    nt = n // (nseq * tm)
    row = lambda b, i: (b * nt + i, 0)
    in_specs = [
        pl.BlockSpec((tm, d), row), pl.BlockSpec((tm, gw), row), pl.BlockSpec((tm, od.shape[1]), row),
        pl.BlockSpec((None, hr, dff2), lambda b, i: (b, 0, 0)),
        _const_spec(w_out.shape), _const_spec((1, d)), _const_spec(w_up.shape),
        _const_spec((FFN_CONV, dff2)), _const_spec((1, dff2)), _const_spec(w_down.shape),
    ]
    out_shape = (jax.ShapeDtypeStruct((n, d), F32), jax.ShapeDtypeStruct((nseq, hr, dff2), F32))
    out_specs = (pl.BlockSpec((tm, d), row), pl.BlockSpec((None, hr, dff2), lambda b, i: (b, 0, 0)))
    return pl.pallas_call(
        functools.partial(_post_kernel, tm=tm, shift=shift, hr=hr, dff=dff2 // 2),
        grid=(nseq, nt), in_specs=in_specs, out_specs=out_specs, out_shape=out_shape,
        scratch_shapes=[pltpu.VMEM((hr + tm, dff2), F32)],
        compiler_params=_cparams(("parallel", "arbitrary")), name="post",
    )(x2d, og, od, hist, w_out, g_ffn, w_up, fcw, fcb, w_down)


def _lane_pad(vec, offset):
    out = jnp.zeros((1, LANES), F32)
    return lax.dynamic_update_slice(out, vec.reshape(1, -1).astype(F32), (0, offset))


def _layer(l, x_prompt, x_sample, state_gdn_conv, state_gdn_s, cache_k, cache_v, page_table, state_ffn_conv, wl):
    (attn_norm_g, w_in, gdn_conv_w, gdn_a_log, gdn_dt_bias, gdn_out_norm_g, diff_q_norm_g, diff_k_norm_g,
     lq1, lk1, lq2, lk2, diff_subln_g, w_out, ffn_norm_g, w_up, ffn_conv_w, ffn_conv_b, w_down) = wl
    nbp, seq, d = x_prompt.shape
    nbs, nts, _ = x_sample.shape
    nh, dk = GDN_HEADS, GDN_DK
    gw = nh * dk
    dw = DIFF_HEADS * DIFF_DV
    lam_init = 0.8 - 0.6 * math.exp(-0.3 * l)

    c_b = 4 * gw
    c_d = c_b + 2 * nh
    w_ba = w_in[:, c_b:c_d]
    w_main = jnp.concatenate([w_in[:, :c_b], w_in[:, c_d:], w_ba,
                              jnp.zeros((d, LANES - 2 * nh), w_in.dtype)], axis=1).astype(BF16)
    w_t = jnp.transpose(jnp.concatenate([w_ba, w_in[:, c_d + 2 * dw:]], axis=1)).astype(BF16)
    reps = dw // DIFF_DQK
    gq_t = jnp.tile(diff_q_norm_g.reshape(1, -1), (1, reps))
    gk_t = jnp.tile(diff_k_norm_g.reshape(1, -1), (1, reps))
    gid = jnp.arange(dw) // DIFF_DQK
    grp = jnp.where(gid[:, None] == gid[None, :], 1.0 / DIFF_DQK, 0.0).astype(BF16)
    g_attn = attn_norm_g.reshape(1, d)
    alr, dtr = _lane_pad(gdn_a_log, nh), _lane_pad(gdn_dt_bias, nh)
    alc = jnp.transpose(alr[:, :SUBLANES])
    dtc = jnp.transpose(dtr[:, :SUBLANES])
    gn = gdn_out_norm_g.reshape(1, dk)
    vecs = [v.reshape(1, -1) for v in (lq1, lk1, lq2, lk2)]
    sg = diff_subln_g.reshape(1, -1)
    w_out_b, w_up_b, w_down_b = w_out.astype(BF16), w_up.astype(BF16), w_down.astype(BF16)
    g_ffn = ffn_norm_g.reshape(1, d)
    fcb = ffn_conv_b.reshape(1, -1)
    dff2 = w_up.shape[1]

    xp2 = x_prompt.reshape(nbp * seq, d)
    tm_p = 256
    (z, gcol, grow, qn, kn, kb, vb, tail_g, knew, vnew, qa, ka, vt) = _inproj_gdn(
        xp2, g_attn, w_main, w_t, gq_t, gk_t, grp, gdn_conv_w, alr, dtr, alc, dtc,
        nb=nbp, tm=tm_p, gw=gw, dw=dw)
    og, s_prompt = _gdn_prompt(qn, kn, kb, vb, gcol, grow, z, gn, nb=nbp, seq=seq, tl=512)
    od = _attn_prompt(qa, ka, vt, *vecs, sg, nb=nbp, seq=seq, tq=512, lam_init=lam_init)
    hr_p = SUBLANES
    y_p, tail_p = _post(xp2, og, od, jnp.zeros((nbp, hr_p, dff2), F32), w_out_b, g_ffn, w_up_b,
                        ffn_conv_w, fcb, w_down_b, nseq=nbp, tm=256, shift=1, hr=hr_p)
    out_p = (y_p.reshape(nbp, seq, d),
             tail_g[:, SUBLANES - (GDN_CONV - 1):, :],
             s_prompt,
             knew.reshape(nbp, seq, DIFF_HEADS, DIFF_DV),
             vnew.reshape(nbp, seq, DIFF_HEADS, DIFF_DV),
             tail_p[:, hr_p - (FFN_CONV - 1):, :])

    xs2 = jnp.transpose(x_sample, (1, 0, 2)).reshape(nts * nbs, d)
    qkv, z, ba, knew, vnew, qa, _, _ = _inproj(xs2, g_attn, w_main, w_t, gq_t, gk_t, grp,
                                               tm=min(256, nts * nbs), gw=gw, dw=dw)
    qkv_tm = qkv.reshape(nts, nbs, 3 * gw)
    hist_tm = jnp.transpose(state_gdn_conv, (1, 0, 2))
    og_tm, s_sample = _gdn_sample(qkv_tm, hist_tm, ba.reshape(nts, nbs, LANES), z.reshape(nts, nbs, gw),
                                  state_gdn_s, gdn_conv_w, alr, dtr, gn, bb=8)
    conv_all = jnp.concatenate([hist_tm, qkv_tm], axis=0)
    conv_s = jnp.transpose(conv_all[nts:], (1, 0, 2))
    to_bm = lambda a: jnp.transpose(a.reshape(nts, nbs, -1), (1, 0, 2))
    q_bm = to_bm(qa)
    q2 = jnp.concatenate([q_bm, q_bm], axis=1)
    pad = jnp.zeros((nbs, nts, dw), F32)
    kn_bm, vn_bm = to_bm(knew), to_bm(vnew)
    od_bm = _attn_sample(page_table, q2, jnp.concatenate([kn_bm, pad], axis=1),
                         jnp.concatenate([vn_bm, pad], axis=1), cache_k, cache_v, *vecs, sg,
                         nt=nts, lam_init=lam_init)
    od_tm = jnp.transpose(od_bm[:, :nts, :], (1, 0, 2)).reshape(nts * nbs, dw).astype(BF16)
    hr_s = (FFN_CONV - 1) * nbs
    hist_f = jnp.transpose(state_ffn_conv, (1, 0, 2)).reshape(1, hr_s, dff2)
    y_s, tail_s = _post(xs2, og_tm.reshape(nts * nbs, gw), od_tm, hist_f, w_out_b, g_ffn, w_up_b,
                        ffn_conv_w, fcb, w_down_b, nseq=1, tm=nbs, shift=nbs, hr=hr_s)
    out_s = (jnp.transpose(y_s.reshape(nts, nbs, d), (1, 0, 2)),
             conv_s,
             s_sample,
             kn_bm.reshape(nbs, nts, DIFF_HEADS, DIFF_DV),
             vn_bm.reshape(nbs, nts, DIFF_HEADS, DIFF_DV),
             jnp.transpose(tail_s.reshape(FFN_CONV - 1, nbs, dff2), (1, 0, 2)))
    return out_p, out_s


def kernel(x_prompt, x_sample, state_gdn_conv, state_gdn_S, cache_k, cache_v, page_table, state_ffn_conv, attn_norm_g, w_in, gdn_conv_w, gdn_A_log, gdn_dt_bias, gdn_out_norm_g, diff_q_norm_g, diff_k_norm_g, diff_lambda_q1, diff_lambda_k1, diff_lambda_q2, diff_lambda_k2, diff_subln_g, w_out, ffn_norm_g, w_up, ffn_conv_w, ffn_conv_b, w_down):
    depth = w_in.shape[0]
    hp, hs = x_prompt, x_sample
    outs_p, outs_s = [], []
    for l in range(depth):
        wl = (attn_norm_g[l], w_in[l], gdn_conv_w[l], gdn_A_log[l], gdn_dt_bias[l], gdn_out_norm_g[l],
              diff_q_norm_g[l], diff_k_norm_g[l], diff_lambda_q1[l], diff_lambda_k1[l], diff_lambda_q2[l],
              diff_lambda_k2[l], diff_subln_g[l], w_out[l], ffn_norm_g[l], w_up[l], ffn_conv_w[l],
              ffn_conv_b[l], w_down[l])
        out_p, out_s = _layer(l, hp, hs, state_gdn_conv[l], state_gdn_S[l], cache_k[l], cache_v[l],
                              page_table, state_ffn_conv[l], wl)
        hp, hs = out_p[0], out_s[0]
        outs_p.append(out_p[1:])
        outs_s.append(out_s[1:])
    stack = lambda outs, i: jnp.stack([o[i] for o in outs])
    return (hp, hs) + tuple(stack(outs_p, i) for i in range(5)) + tuple(stack(outs_s, i) for i in range(5))
```

```python
import functools
import math

import jax
import jax.numpy as jnp
from jax import lax
from jax.experimental import pallas as pl
from jax.experimental.pallas import tpu as pltpu

F32 = jnp.float32
BF16 = jnp.bfloat16
EPS = 1e-6
NEG = -1e30

GDN_HEADS = 4
GDN_DK = 128
GDN_CONV = 4
GDN_CHUNK = 64
DIFF_HEADS = 4
DIFF_DV = 128
DIFF_DQK = 64
FFN_CONV = 3
LANES = 128
SUBLANES = 8
VMEM_LIMIT = 56 * 1024 * 1024


def _cparams(sem):
    return pltpu.CompilerParams(dimension_semantics=sem, vmem_limit_bytes=VMEM_LIMIT)


def _const_spec(shape):
    nd = len(shape)
    return pl.BlockSpec(shape, lambda *_: (0,) * nd, pipeline_mode=pl.Buffered(1))


def _dot(a, b):
    return jnp.dot(a, b, preferred_element_type=F32)


def _dot_nt(a, b):
    return lax.dot_general(a, b, (((1,), (1,)), ((), ())), preferred_element_type=F32)


def _dot_tn(a, b):
    return lax.dot_general(a, b, (((0,), (0,)), ((), ())), preferred_element_type=F32)


def _softplus(x):
    return jnp.maximum(x, 0.0) + jnp.log1p(jnp.exp(-jnp.abs(x)))


def _silu(x):
    return x * jax.nn.sigmoid(x)


def _split3(x):
    hi = x.astype(BF16)
    r = x - hi.astype(F32)
    mid = r.astype(BF16)
    lo = (r - mid.astype(F32)).astype(BF16)
    return hi, mid, lo


def _inproj_core(x_ref, g_ref, wg_ref, wd_ref, wt_ref, gq_ref, gk_ref, grp_ref,
                 knew_ref, vnew_ref, qa_ref, ka_ref, vt_ref, gw, dw):
    tm = x_ref.shape[0]
    x = x_ref[...]
    xn = x * lax.rsqrt(jnp.mean(x * x, axis=-1, keepdims=True) + EPS) * g_ref[...]
    xb = xn.astype(BF16)
    proj = _dot(xb, wg_ref[...])
    pd = _dot(xb, wd_ref[...])
    tr = _dot_nt(wt_ref[...], xb)
    vt_ref[...] = tr[LANES:].astype(BF16)
    dq = pd[:, :dw]
    dk = pd[:, dw:2 * dw]
    dv = pd[:, 2 * dw:]
    grp = grp_ref[...]
    msq = _dot((dq * dq).astype(BF16), grp)
    msk = _dot((dk * dk).astype(BF16), grp)
    dqn = dq * lax.rsqrt(msq + EPS) * gq_ref[...]
    dkn = dk * lax.rsqrt(msk + EPS) * gk_ref[...]
    for h in range(DIFF_HEADS):
        head_rows = pl.ds(h, tm, stride=DIFF_HEADS)
        knew_ref[head_rows, :] = dkn[:, h * DIFF_DV:(h + 1) * DIFF_DV]
        vnew_ref[head_rows, :] = dv[:, h * DIFF_DV:(h + 1) * DIFF_DV]
    qa_ref[...] = (dqn * (DIFF_DQK ** -0.5 * math.log2(math.e))).astype(BF16)
    ka_ref[...] = dkn.astype(BF16)
    return proj, tr[:LANES]


def _inproj_kernel(x_ref, g_ref, wg_ref, wd_ref, wt_ref, gq_ref, gk_ref, grp_ref,
                   qkv_ref, z_ref, ba_ref, knew_ref, vnew_ref, qa_ref, ka_ref, vt_ref, *, gw, dw):
    proj, trow = _inproj_core(x_ref, g_ref, wg_ref, wd_ref, wt_ref, gq_ref, gk_ref, grp_ref,
                              knew_ref, vnew_ref, qa_ref, ka_ref, vt_ref, gw, dw)
    qkv_ref[...] = proj[:, :3 * gw]
    z_ref[...] = proj[:, 3 * gw:]
    ba_ref[...] = jnp.transpose(trow)


def _inproj_gdn_kernel(x_ref, g_ref, wg_ref, wd_ref, wt_ref, gq_ref, gk_ref, grp_ref,
                       cw_ref, alr_ref, dtr_ref, alc_ref, dtc_ref,
                       z_ref, gcol_ref, grow_ref, qn_ref, kn_ref, kb_ref, vb_ref, tail_ref,
                       knew_ref, vnew_ref, qa_ref, ka_ref, vt_ref, xbuf, *, gw, dw):
    i = pl.program_id(1)
    tm = x_ref.shape[0]
    nh, dk = GDN_HEADS, GDN_DK
    hr = SUBLANES

    @pl.when(i == 0)
    def _():
        xbuf[0:hr, :] = jnp.zeros((hr, 3 * gw), F32)

    proj, trow = _inproj_core(x_ref, g_ref, wg_ref, wd_ref, wt_ref, gq_ref, gk_ref, grp_ref,
                              knew_ref, vnew_ref, qa_ref, ka_ref, vt_ref, gw, dw)
    z_ref[...] = proj[:, 3 * gw:]
    xbuf[hr:hr + tm, :] = proj[:, :3 * gw]
    cw = cw_ref[...]
    y = cw[GDN_CONV - 1:GDN_CONV, :] * xbuf[hr:hr + tm, :]
    for j in range(1, GDN_CONV):
        y = y + cw[GDN_CONV - 1 - j:GDN_CONV - j, :] * xbuf[hr - j:hr - j + tm, :]
    tail = xbuf[tm:tm + hr, :]
    tail_ref[...] = tail
    xbuf[0:hr, :] = tail
    y = _silu(y)
    ba = jnp.transpose(trow)
    beta_c = jax.nn.sigmoid(ba)
    gcol_ref[...] = -jnp.exp(alr_ref[...]) * _softplus(ba + dtr_ref[...])
    grow_ref[...] = -jnp.exp(alc_ref[...]) * _softplus(trow[:SUBLANES] + dtc_ref[...])
    for h in range(nh):
        ls = slice(h * dk, (h + 1) * dk)
        q = y[:, h * dk:(h + 1) * dk]
        k = y[:, gw + h * dk:gw + (h + 1) * dk]
        v = y[:, 2 * gw + h * dk:2 * gw + (h + 1) * dk]
        qn = q * lax.rsqrt(jnp.sum(q * q, axis=-1, keepdims=True) + EPS) * (dk ** -0.5)
        kn = k * lax.rsqrt(jnp.sum(k * k, axis=-1, keepdims=True) + EPS)
        beta = jnp.broadcast_to(beta_c[:, h:h + 1], (tm, dk))
        qn_ref[:, ls] = qn.astype(BF16)
        kn_ref[:, ls] = kn.astype(BF16)
        kb_ref[:, ls] = (kn * beta).astype(BF16)
        vb_ref[:, ls] = (v * beta).astype(BF16)


def _attn_out(n, tm, dw):
    return (
        jax.ShapeDtypeStruct((n * DIFF_HEADS, DIFF_DV), F32),
        jax.ShapeDtypeStruct((n * DIFF_HEADS, DIFF_DV), F32),
        jax.ShapeDtypeStruct((n, dw), BF16),
        jax.ShapeDtypeStruct((n, dw), BF16),
        jax.ShapeDtypeStruct((n // tm, dw, tm), BF16),
    )


def _inproj(x2d, g_attn, w_g, w_d, w_t, gq_t, gk_t, grp, *, tm, gw, dw):
    n, d = x2d.shape
    row = lambda i: (i, 0)
    out_shape = (
        jax.ShapeDtypeStruct((n, 3 * gw), F32),
        jax.ShapeDtypeStruct((n, gw), F32),
        jax.ShapeDtypeStruct((n, LANES), F32),
    ) + _attn_out(n, tm, dw)
    out_specs = (
        pl.BlockSpec((tm, 3 * gw), row), pl.BlockSpec((tm, gw), row), pl.BlockSpec((tm, LANES), row),
        pl.BlockSpec((tm * DIFF_HEADS, DIFF_DV), row), pl.BlockSpec((tm * DIFF_HEADS, DIFF_DV), row),
        pl.BlockSpec((tm, dw), row), pl.BlockSpec((tm, dw), row),
        pl.BlockSpec((None, dw, tm), lambda i: (i, 0, 0)),
    )
    in_specs = [
        pl.BlockSpec((tm, d), row), _const_spec((1, d)), _const_spec(w_g.shape), _const_spec(w_d.shape),
        _const_spec(w_t.shape), _const_spec((1, dw)), _const_spec((1, dw)), _const_spec((dw, dw)),
    ]
    return pl.pallas_call(
        functools.partial(_inproj_kernel, gw=gw, dw=dw),
        grid=(n // tm,), in_specs=in_specs, out_specs=out_specs, out_shape=out_shape,
        compiler_params=_cparams(("parallel",)), name="inproj",
    )(x2d, g_attn, w_g, w_d, w_t, gq_t, gk_t, grp)


def _inproj_gdn(x2d, g_attn, w_g, w_d, w_t, gq_t, gk_t, grp, cw, alr, dtr, alc, dtc, *, nb, tm, gw, dw):
    n, d = x2d.shape
    nt = n // (nb * tm)
    row = lambda b, i: (b * nt + i, 0)
    out_shape = (
        jax.ShapeDtypeStruct((n, gw), F32),
        jax.ShapeDtypeStruct((n, LANES), F32),
        jax.ShapeDtypeStruct((nb, SUBLANES, n // nb), F32),
        jax.ShapeDtypeStruct((n, gw), BF16),
        jax.ShapeDtypeStruct((n, gw), BF16),
        jax.ShapeDtypeStruct((n, gw), BF16),
        jax.ShapeDtypeStruct((n, gw), BF16),
        jax.ShapeDtypeStruct((nb, SUBLANES, 3 * gw), F32),
    ) + _attn_out(n, tm, dw)
    out_specs = (
        pl.BlockSpec((tm, gw), row), pl.BlockSpec((tm, LANES), row),
        pl.BlockSpec((None, SUBLANES, tm), lambda b, i: (b, 0, i)),
        pl.BlockSpec((tm, gw), row), pl.BlockSpec((tm, gw), row), pl.BlockSpec((tm, gw), row),
        pl.BlockSpec((tm, gw), row),
        pl.BlockSpec((None, SUBLANES, 3 * gw), lambda b, i: (b, 0, 0)),
        pl.BlockSpec((tm * DIFF_HEADS, DIFF_DV), row), pl.BlockSpec((tm * DIFF_HEADS, DIFF_DV), row),
        pl.BlockSpec((tm, dw), row), pl.BlockSpec((tm, dw), row),
        pl.BlockSpec((None, dw, tm), lambda b, i: (b * nt + i, 0, 0)),
    )
    in_specs = [
        pl.BlockSpec((tm, d), row), _const_spec((1, d)), _const_spec(w_g.shape), _const_spec(w_d.shape),
        _const_spec(w_t.shape), _const_spec((1, dw)), _const_spec((1, dw)), _const_spec((dw, dw)),
        _const_spec(cw.shape), _const_spec((1, LANES)), _const_spec((1, LANES)),
        _const_spec((SUBLANES, 1)), _const_spec((SUBLANES, 1)),
    ]
    return pl.pallas_call(
        functools.partial(_inproj_gdn_kernel, gw=gw, dw=dw),
        grid=(nb, nt), in_specs=in_specs, out_specs=out_specs, out_shape=out_shape,
        scratch_shapes=[pltpu.VMEM((tm + SUBLANES, 3 * gw), F32)],
        compiler_params=_cparams(("parallel", "arbitrary")), name="inproj_gdn",
    )(x2d, g_attn, w_g, w_d, w_t, gq_t, gk_t, grp, cw, alr, dtr, alc, dtc)


def _gdn_prompt_kernel(qn_ref, kn_ref, kb_ref, vb_ref, gcol_ref, grow_ref, z_ref, gn_ref,
                       og_ref, sout_ref, s_scr, *, tl):
    i = pl.program_id(0)
    nb = qn_ref.shape[0]
    nh, dk, c = GDN_HEADS, GDN_DK, GDN_CHUNK
    gw = nh * dk

    @pl.when(i == 0)
    def _():
        s_scr[...] = jnp.zeros_like(s_scr)

    g_rep = jnp.concatenate([jnp.broadcast_to(gcol_ref[b][:, nh + h:nh + h + 1], (tl, dk))
                             for b in range(nb) for h in range(nh)], axis=1)
    g_r = jnp.concatenate([grow_ref[b] for b in range(nb)], axis=0)

    ii = lax.broadcasted_iota(jnp.int32, (tl, tl), 0)
    jj = lax.broadcasted_iota(jnp.int32, (tl, tl), 1)
    same = lax.shift_right_logical(ii, 6) == lax.shift_right_logical(jj, 6)
    tri = jnp.where(same & (ii >= jj), 1.0, 0.0).astype(BF16)
    triu = jnp.where(same & (ii <= jj), 1.0, 0.0).astype(BF16)
    gc_rep = sum(_dot(tri, p) for p in _split3(g_rep))
    gc_r = sum(_dot(p, triu) for p in _split3(g_r))

    pr = 2 * c
    pi = lax.broadcasted_iota(jnp.int32, (pr, pr), 0)
    pj = lax.broadcasted_iota(jnp.int32, (pr, pr), 1)
    psame = lax.shift_right_logical(pi, 6) == lax.shift_right_logical(pj, 6)
    lower = psame & (pi >= pj)
    strict = psame & (pi > pj)
    eye = jnp.where(pi == pj, 1.0, 0.0).astype(F32)
    zeros_c = jnp.zeros((c, dk), F32)

    npair = tl // pr
    seqs = [(b, h) for b in range(nb) for h in range(nh)]
    hp = [(b, h, p) for b, h in seqs for p in range(npair)]

    def tile(ref, b, h, p):
        return ref[b, p * pr:(p + 1) * pr, h * dk:(h + 1) * dk]

    gcp = {(b, h, p): gc_rep[p * pr:(p + 1) * pr, b * gw + h * dk:b * gw + (h + 1) * dk] for b, h, p in hp}
    dec = {(b, h, p): jnp.exp(jnp.where(
        lower, gcp[b, h, p] - gc_r[b * SUBLANES + nh + h:b * SUBLANES + nh + h + 1, p * pr:(p + 1) * pr], NEG))
        for b, h, p in hp}
    kpb = {x: tile(kn_ref, *x) for x in hp}
    kbp = {x: tile(kb_ref, *x) for x in hp}
    m = {x: -jnp.where(strict, _dot_nt(kbp[x], kpb[x]) * dec[x], 0.0) for x in hp}
    a = {x: eye + m[x] for x in hp}
    pw = {x: _dot(m[x].astype(BF16), m[x].astype(BF16)) for x in hp}
    for it in range(1, 6):
        for x in hp:
            pwb = pw[x].astype(BF16)
            if it < 5:
                res = _dot(pwb, jnp.concatenate([a[x].astype(BF16), pwb], axis=1))
                a[x] = a[x] + res[:, :pr]
                pw[x] = res[:, pr:]
            else:
                a[x] = a[x] + _dot(pwb, a[x].astype(BF16))
    egc = {x: jnp.exp(gcp[x]) for x in hp}
    sol = {x: _dot(a[x].astype(BF16),
                   jnp.concatenate([tile(vb_ref, *x), (kbp[x].astype(F32) * egc[x]).astype(BF16)], axis=1))
           for x in hp}
    qk = {x: (_dot_nt(tile(qn_ref, *x), kpb[x]) * dec[x]).astype(BF16) for x in hp}
    qg = {x: tile(qn_ref, *x).astype(F32) * egc[x] for x in hp}

    s_cur = {x: s_scr[x] for x in seqs}
    o_rows = {x: [] for x in seqs}
    for p in range(npair):
        for cc in range(2):
            c0 = cc * c
            ws = {}
            for b, h in seqs:
                wq = jnp.concatenate([sol[b, h, p][c0:c0 + c, dk:], qg[b, h, p][c0:c0 + c]], axis=0)
                ws[b, h] = _dot(wq.astype(BF16), s_cur[b, h].astype(BF16))
            for b, h in seqs:
                g_c0 = gcp[b, h, p][c0:c0 + c]
                glast = g_c0[c - 1:c, :]
                kd = kpb[b, h, p][c0:c0 + c].astype(F32) * jnp.exp(glast - g_c0)
                v_new = (sol[b, h, p][c0:c0 + c, :dk] - ws[b, h][:c]).astype(BF16)
                s_cur[b, h] = s_cur[b, h] * jnp.exp(glast) + _dot_tn(kd.astype(BF16), v_new)
                zc = jnp.zeros_like(v_new)
                vn_pad = jnp.concatenate([v_new, zc] if cc == 0 else [zc, v_new], axis=0)
                o_rows[b, h].append(ws[b, h][c:] + _dot(qk[b, h, p][c0:c0 + c], vn_pad))
    for b, h in seqs:
        s_scr[b, h] = s_cur[b, h]
        o = jnp.concatenate(o_rows[b, h], axis=0)
        o = o * lax.rsqrt(jnp.mean(o * o, axis=-1, keepdims=True) + EPS) * gn_ref[...]
        og_ref[b, :, h * dk:(h + 1) * dk] = (o * _silu(z_ref[b, :, h * dk:(h + 1) * dk])).astype(og_ref.dtype)

    @pl.when(i == pl.num_programs(0) - 1)
    def _():
        sout_ref[...] = s_scr[...]


def _gdn_prompt(qn, kn, kb, vb, gcol, grow, z, gn, *, nb, seq, tl):
    nh, dk = GDN_HEADS, GDN_DK
    gw = nh * dk
    nt = seq // tl
    tok = lambda i: (0, i, 0)
    wide = pl.BlockSpec((nb, tl, gw), tok)
    per_seq = lambda a: a.reshape(nb, seq, a.shape[-1])
    in_specs = [
        wide, wide, wide, wide, pl.BlockSpec((nb, tl, LANES), tok),
        pl.BlockSpec((nb, SUBLANES, tl), lambda i: (0, 0, i)),
        wide, _const_spec((1, dk)),
    ]
    out_shape = (jax.ShapeDtypeStruct((nb, seq, gw), BF16),
                 jax.ShapeDtypeStruct((nb, nh, dk, dk), F32))
    out_specs = (wide, pl.BlockSpec((nb, nh, dk, dk), lambda i: (0, 0, 0, 0)))
    og, s_out = pl.pallas_call(
        functools.partial(_gdn_prompt_kernel, tl=tl),
        grid=(nt,), in_specs=in_specs, out_specs=out_specs, out_shape=out_shape,
        scratch_shapes=[pltpu.VMEM((nb, nh, dk, dk), F32)],
        compiler_params=_cparams(("arbitrary",)), name="gdn_prompt",
    )(per_seq(qn), per_seq(kn), per_seq(kb), per_seq(vb), per_seq(gcol), grow, per_seq(z), gn)
    return og.reshape(nb * seq, gw), s_out


def _gdn_sample_pre_kernel(qkv_ref, hist_ref, ba_ref, cw_ref, alr_ref, dtr_ref, r_ref, intra_ref, *, nt):
    nh, dk = GDN_HEADS, GDN_DK
    gw = nh * dk
    nb = qkv_ref.shape[1]
    cw = cw_ref[...]
    nhist = GDN_CONV - 1
    xp = [hist_ref[j] for j in range(nhist)] + [qkv_ref[t] for t in range(nt)]
    ys = []
    for t in range(nt):
        y = cw[0:1, :] * xp[t]
        for j in range(1, GDN_CONV):
            y = y + cw[j:j + 1, :] * xp[t + j]
        ys.append(_silu(y))
    beta, gc = [], []
    for t in range(nt):
        ba = ba_ref[t]
        beta_c = jax.nn.sigmoid(ba)
        g_c = -jnp.exp(alr_ref[...]) * _softplus(ba + dtr_ref[...])
        beta.append([jnp.broadcast_to(beta_c[:, h:h + 1], (nb, dk)) for h in range(nh)])
        g_t = [jnp.broadcast_to(g_c[:, nh + h:nh + h + 1], (nb, dk)) for h in range(nh)]
        gc.append(g_t if t == 0 else [gc[t - 1][h] + g_t[h] for h in range(nh)])
    zrow = jnp.zeros((nb, dk), F32)
    for h in range(nh):
        q, k, v = [], [], []
        for t in range(nt):
            qt = ys[t][:, h * dk:(h + 1) * dk]
            kt = ys[t][:, gw + h * dk:gw + (h + 1) * dk]
            q.append(qt * lax.rsqrt(jnp.sum(qt * qt, axis=-1, keepdims=True) + EPS) * (dk ** -0.5))
            k.append(kt * lax.rsqrt(jnp.sum(kt * kt, axis=-1, keepdims=True) + EPS))
            v.append(ys[t][:, 2 * gw + h * dk:2 * gw + (h + 1) * dk])
        g = [gc[t][h] for t in range(nt)]
        us, ws = [], []
        for t in range(nt):
            b_t = beta[t][h]
            u_t = v[t] * b_t
            w_t = k[t] * b_t * jnp.exp(g[t])
            for s in range(t):
                m_ts = b_t * jnp.sum(k[t] * k[s], axis=-1, keepdims=True) * jnp.exp(g[t] - g[s])
                u_t = u_t - m_ts * us[s]
                w_t = w_t - m_ts * ws[s]
            us.append(u_t)
            ws.append(w_t)
        lane = slice(h * dk, (h + 1) * dk)
        for t in range(nt):
            r_ref[t, :, lane] = ws[t]
            r_ref[nt + t, :, lane] = q[t] * jnp.exp(g[t])
            r_ref[2 * nt + t, :, lane] = us[t]
            r_ref[3 * nt + t, :, lane] = k[t] * jnp.exp(g[nt - 1] - g[t])
            for s in range(nt):
                if s <= t:
                    intra_ref[t * nt + s, :, lane] = (jnp.sum(q[t] * k[s], axis=-1, keepdims=True)
                                                       * jnp.exp(g[t] - g[s]))
                else:
                    intra_ref[t * nt + s, :, lane] = zrow
        r_ref[4 * nt, :, lane] = jnp.exp(g[nt - 1])
        for r in range(4 * nt + 1, r_ref.shape[0]):
            r_ref[r, :, lane] = zrow


def _gdn_sample_state_kernel(r_ref, s_ref, ws_ref, sout_ref, *, nt, bb):
    nh = GDN_HEADS
    rows = lax.broadcasted_iota(jnp.int32, (2 * nt, GDN_DK), 0)
    inst = [(bi, h) for bi in range(bb) for h in range(nh)]
    ws = {x: _dot(r_ref[x[0], x[1], 0:2 * nt, :].astype(BF16), s_ref[x].astype(BF16)) for x in inst}
    for x in inst:
        ud = r_ref[x[0], x[1], 2 * nt:4 * nt, :]
        vn = jnp.where(rows < nt, ud - ws[x], 0.0)
        kd = jnp.where(rows < nt, pltpu.roll(ud, nt, axis=0), 0.0)
        ws_ref[x] = ws[x]
        sout_ref[x] = (s_ref[x] * r_ref[x[0], x[1], 4 * nt:4 * nt + 1, :]
                       + _dot_tn(kd.astype(BF16), vn.astype(BF16)))


def _gdn_sample_post_kernel(ws_ref, r_ref, intra_ref, z_ref, gn_ref, og_ref, *, nt):
    nh, dk = GDN_HEADS, GDN_DK
    for h in range(nh):
        lane = slice(h * dk, (h + 1) * dk)
        vn = [r_ref[2 * nt + t, :, lane] - ws_ref[t, :, lane] for t in range(nt)]
        for t in range(nt):
            o = ws_ref[nt + t, :, lane]
            for s in range(t + 1):
                o = o + intra_ref[t * nt + s, :, lane] * vn[s]
            o = o * lax.rsqrt(jnp.mean(o * o, axis=-1, keepdims=True) + EPS) * gn_ref[...]
            og_ref[t, :, lane] = (o * _silu(z_ref[t, :, lane])).astype(og_ref.dtype)


def _gdn_sample(qkv_tm, hist_tm, ba_tm, z_tm, state_s, cw, alr, dtr, gn, *, bb):
    nt, nb, _ = qkv_tm.shape
    nh, dk = GDN_HEADS, GDN_DK
    gw = nh * dk
    nr = 6 * nt
    full = lambda shape: pl.BlockSpec(shape, lambda *_: (0,) * len(shape))
    r_tm, intra = pl.pallas_call(
        functools.partial(_gdn_sample_pre_kernel, nt=nt),
        grid=(1,),
        in_specs=[full(qkv_tm.shape), full(hist_tm.shape), full(ba_tm.shape), full(cw.shape),
                  full(alr.shape), full(dtr.shape)],
        out_specs=(full((nr, nb, gw)), full((nt * nt, nb, gw))),
        out_shape=(jax.ShapeDtypeStruct((nr, nb, gw), F32), jax.ShapeDtypeStruct((nt * nt, nb, gw), F32)),
        compiler_params=_cparams(("arbitrary",)), name="gdn_sample_pre",
    )(qkv_tm, hist_tm, ba_tm, cw, alr, dtr)
    r_bm = jnp.transpose(r_tm.reshape(nr, nb, nh, dk), (1, 2, 0, 3))
    blk = lambda b: (b, 0, 0, 0)
    ws_bm, s_new = pl.pallas_call(
        functools.partial(_gdn_sample_state_kernel, nt=nt, bb=bb),
        grid=(nb // bb,),
        in_specs=[pl.BlockSpec((bb, nh, nr, dk), blk), pl.BlockSpec((bb, nh, dk, dk), blk)],
        out_specs=(pl.BlockSpec((bb, nh, 2 * nt, dk), blk), pl.BlockSpec((bb, nh, dk, dk), blk)),
        out_shape=(jax.ShapeDtypeStruct((nb, nh, 2 * nt, dk), F32), jax.ShapeDtypeStruct((nb, nh, dk, dk), F32)),
        compiler_params=_cparams(("parallel",)), name="gdn_sample_state",
    )(r_bm, state_s)
    ws_tm = jnp.transpose(ws_bm, (2, 0, 1, 3)).reshape(2 * nt, nb, gw)
    og = pl.pallas_call(
        functools.partial(_gdn_sample_post_kernel, nt=nt),
        grid=(1,),
        in_specs=[full(ws_tm.shape), full(r_tm.shape), full(intra.shape), full(z_tm.shape), full(gn.shape)],
        out_specs=full((nt, nb, gw)),
        out_shape=jax.ShapeDtypeStruct((nt, nb, gw), BF16),
        compiler_params=_cparams(("arbitrary",)), name="gdn_sample_post",
    )(ws_tm, r_tm, intra, z_tm, gn)
    return og, s_new


def _lambda(lq1_ref, lk1_ref, lq2_ref, lk2_ref, lam_init):
    s1 = jnp.sum(lq1_ref[...] * lk1_ref[...], axis=-1, keepdims=True)
    s2 = jnp.sum(lq2_ref[...] * lk2_ref[...], axis=-1, keepdims=True)
    return jnp.exp(s1) - jnp.exp(s2) + lam_init


def _subln(o, g_ref, lam_init):
    return o * lax.rsqrt(jnp.mean(o * o, axis=-1, keepdims=True) + EPS) * g_ref[...] * (1.0 - lam_init)


def _attn_prompt_kernel(q_ref, k_ref, vt_ref, lq1_ref, lk1_ref, lq2_ref, lk2_ref, sg_ref, o_ref,
                        sa_scr, sb_scr, m_scr, acc_scr, *, tq, lam_init):
    qi = pl.program_id(1)
    nh, dv, dqk = DIFF_HEADS, DIFF_DV, DIFF_DQK
    lam = _lambda(lq1_ref, lk1_ref, lq2_ref, lk2_ref, lam_init)
    lane = lax.broadcasted_iota(jnp.int32, (tq, dv), 1)
    kidx = lax.broadcasted_iota(jnp.int32, (tq, tq), 0)
    qidx = lax.broadcasted_iota(jnp.int32, (tq, tq), 1)
    causal = kidx <= qidx
    nsum = 2 * SUBLANES
    ones_rows = jnp.ones((nsum, tq), BF16)
    lanes = [slice(h * dv, (h + 1) * dv) for h in range(nh)]
    qs = []
    for h in range(nh):
        q = q_ref[:, lanes[h]]
        zero = jnp.zeros_like(q)
        qs.append((jnp.where(lane < dqk, q, zero), jnp.where(lane >= dqk, q, zero)))

    nvt = tq // vt_ref.shape[-1]
    nchain = 2 * nh

    def scores(dst, j):
        for h in range(nh):
            kb = k_ref[pl.ds(pl.multiple_of(j * tq, tq), tq), lanes[h]]
            for c in range(2):
                dst[2 * h + c] = _dot_nt(kb, qs[h][c])

    def softmax_pv(src, j, masked):
        ps, alphas = [], []
        for x in range(nchain):
            s_x = jnp.where(causal, src[x], NEG) if masked else src[x]
            m_i = m_scr[x]
            m_new = jnp.maximum(m_i, jnp.max(s_x, axis=0, keepdims=True))
            m_scr[x] = m_new
            alphas.append(jnp.exp2(m_i - m_new))
            ps.append(jnp.exp2(s_x - m_new).astype(BF16))
        for h in range(nh):
            vt = jnp.concatenate([vt_ref[j * nvt + t, lanes[h], :] for t in range(nvt)], axis=1)
            vt = jnp.concatenate([vt, ones_rows], axis=0)
            for c in range(2):
                x = 2 * h + c
                acc_scr[x] = alphas[x] * acc_scr[x] + _dot(vt, ps[x])

    m_scr[...] = jnp.full(m_scr.shape, NEG, F32)
    acc_scr[...] = jnp.zeros(acc_scr.shape, F32)
    scores(sa_scr, 0)

    def two_blocks(i, _):
        scores(sb_scr, 2 * i + 1)
        softmax_pv(sa_scr, 2 * i, False)
        scores(sa_scr, 2 * i + 2)
        softmax_pv(sb_scr, 2 * i + 1, False)
        return 0

    lax.fori_loop(0, lax.shift_right_logical(qi, 1), two_blocks, 0)
    odd = lax.rem(qi, 2) == 1

    @pl.when(odd)
    def _():
        scores(sb_scr, qi)
        softmax_pv(sa_scr, qi - 1, False)
        softmax_pv(sb_scr, qi, True)

    @pl.when(jnp.logical_not(odd))
    def _():
        softmax_pv(sa_scr, qi, True)

    for h in range(nh):
        ls = lanes[h]
        a1, a2 = acc_scr[2 * h], acc_scr[2 * h + 1]
        ot = a1[:dv] / a1[dv:dv + 1] - lam * (a2[:dv] / a2[dv:dv + 1])
        ot = ot * lax.rsqrt(jnp.mean(ot * ot, axis=0, keepdims=True) + EPS) * (1.0 - lam_init)
        o_ref[:, ls] = (jnp.transpose(ot) * sg_ref[...]).astype(o_ref.dtype)


def _attn_prompt(qa, ka, vt, lq1, lk1, lq2, lk2, sg, *, nb, seq, tq, lam_init):
    nh, dv = DIFF_HEADS, DIFF_DV
    wd = nh * dv
    nq = seq // tq
    qa3, ka3 = qa.reshape(nb, seq, wd), ka.reshape(nb, seq, wd)
    tv = vt.shape[-1]
    nv = seq // tv
    vt4 = vt.reshape(nb, nv, wd, tv)
    vec = _const_spec((1, DIFF_DQK))
    out = pl.pallas_call(
        functools.partial(_attn_prompt_kernel, tq=tq, lam_init=lam_init),
        grid=(nb, nq),
        in_specs=[pl.BlockSpec((None, tq, wd), lambda b, i: (b, i, 0)),
                  pl.BlockSpec((None, seq, wd), lambda b, i: (b, 0, 0)),
                  pl.BlockSpec((None, nv, wd, tv), lambda b, i: (b, 0, 0, 0)),
                  vec, vec, vec, vec, _const_spec((1, dv))],
        out_specs=pl.BlockSpec((None, tq, wd), lambda b, i: (b, i, 0)),
        out_shape=jax.ShapeDtypeStruct((nb, seq, wd), BF16),
        scratch_shapes=[pltpu.VMEM((2 * nh, tq, tq), F32), pltpu.VMEM((2 * nh, tq, tq), F32),
                        pltpu.VMEM((2 * nh, 1, tq), F32), pltpu.VMEM((2 * nh, dv + 2 * SUBLANES, tq), F32)],
        compiler_params=_cparams(("parallel", "arbitrary")), name="attn_prompt",
    )(qa3, ka3, vt4, lq1, lk1, lq2, lk2, sg)
    return out.reshape(nb * seq, wd)


def _attn_sample_kernel(pt_ref, q_ref, kn_ref, vn_ref, *rest, n_pages, nt, lam_init):
    k_refs = rest[:n_pages]
    v_refs = rest[n_pages:2 * n_pages]
    lq1_ref, lk1_ref, lq2_ref, lk2_ref, sg_ref, o_ref = rest[2 * n_pages:]
    del pt_ref
    nh, dv, dqk = DIFF_HEADS, DIFF_DV, DIFF_DQK
    page = k_refs[0].shape[0] // nh
    nr = 2 * nt
    lam = _lambda(lq1_ref, lk1_ref, lq2_ref, lk2_ref, lam_init)
    row = lax.broadcasted_iota(jnp.int32, (nr, dv), 0)
    lane = lax.broadcasted_iota(jnp.int32, (nr, dv), 1)
    first = row < nt
    keep = jnp.logical_xor(lane >= dqk, first)
    tpos = jnp.where(first, row, row - nt)
    new_ok = (lane < nt) & (lane <= tpos)
    zpad = jnp.zeros((page - nr, dv), F32)
    lanes = [slice(h * dv, (h + 1) * dv) for h in range(nh)]
    head_rows = [pl.ds(h, page, stride=nh) for h in range(nh)]
    groups = [list(range(j, min(j + 2, n_pages))) for j in range(0, n_pages, 2)]

    def past(refs, h, grp):
        return jnp.concatenate([refs[j][head_rows[h], :] for j in grp], axis=0).astype(BF16)

    s_all = []
    for h in range(nh):
        qh = q_ref[:, lanes[h]]
        qz = jnp.where(keep, qh, jnp.zeros_like(qh))
        s_h = [_dot_nt(qz, past(k_refs, h, grp)) for grp in groups]
        knew = jnp.concatenate([kn_ref[:, lanes[h]], zpad], axis=0)
        s_h.append(jnp.where(new_ok, _dot_nt(qz, knew.astype(BF16)), NEG))
        s_all.append(s_h)
    p_all, l_all = [], []
    for h in range(nh):
        m = jnp.max(s_all[h][0], axis=-1, keepdims=True)
        for s in s_all[h][1:]:
            m = jnp.maximum(m, jnp.max(s, axis=-1, keepdims=True))
        ps = [jnp.exp2(s - m) for s in s_all[h]]
        l = jnp.sum(ps[0], axis=-1, keepdims=True)
        for p in ps[1:]:
            l = l + jnp.sum(p, axis=-1, keepdims=True)
        p_all.append([p.astype(BF16) for p in ps])
        l_all.append(l)
    for h in range(nh):
        acc = _dot(p_all[h][-1], jnp.concatenate([vn_ref[:, lanes[h]], zpad], axis=0).astype(BF16))
        for gi, grp in enumerate(groups):
            acc = acc + _dot(p_all[h][gi], past(v_refs, h, grp))
        o2 = acc / l_all[h]
        o = o2 - lam * pltpu.roll(o2, nt, axis=0)
        o_ref[:, lanes[h]] = _subln(o, sg_ref, lam_init)


def _attn_sample(page_table, q2, kn8, vn8, cache_k, cache_v, lq1, lk1, lq2, lk2, sg, *, nt, lam_init):
    nb, n_pages = page_table.shape
    nh, dv = DIFF_HEADS, DIFF_DV
    page = cache_k.shape[1]
    wd = nh * dv
    ck = cache_k.reshape(cache_k.shape[0], page * nh, dv)
    cv = cache_v.reshape(cache_v.shape[0], page * nh, dv)
    nr = 2 * nt
    small = pl.BlockSpec((None, nr, wd), lambda b, pt: (b, 0, 0))

    def page_spec(j):
        return pl.BlockSpec((None, page * nh, dv), lambda b, pt: (pt[b, j], 0, 0))

    vec = pl.BlockSpec((1, DIFF_DQK), lambda b, pt: (0, 0))
    grid_spec = pltpu.PrefetchScalarGridSpec(
        num_scalar_prefetch=1, grid=(nb,),
        in_specs=[small, small, small] + [page_spec(j) for j in range(n_pages)] * 2
        + [vec, vec, vec, vec, pl.BlockSpec((1, dv), lambda b, pt: (0, 0))],
        out_specs=small)
    return pl.pallas_call(
        functools.partial(_attn_sample_kernel, n_pages=n_pages, nt=nt, lam_init=lam_init),
        grid_spec=grid_spec, out_shape=jax.ShapeDtypeStruct((nb, nr, wd), F32),
        compiler_params=_cparams(("parallel",)), name="attn_sample",
    )(page_table, q2, kn8, vn8, *([ck] * n_pages), *([cv] * n_pages), lq1, lk1, lq2, lk2, sg)


def _post_kernel(x_ref, og_ref, od_ref, hist_ref, wo_ref, gf_ref, wu_ref, fcw_ref, fcb_ref, wd_ref,
                 y_ref, tail_ref, ubuf, *, tm, shift, hr, dff):
    i = pl.program_id(1)
    gw = og_ref.shape[1]

    @pl.when(i == 0)
    def _():
        ubuf[0:hr, :] = hist_ref[...]

    hres = x_ref[...] + _dot(og_ref[...], wo_ref[0:gw, :]) + _dot(od_ref[...], wo_ref[gw:, :])
    hn = hres * lax.rsqrt(jnp.mean(hres * hres, axis=-1, keepdims=True) + EPS) * gf_ref[...]
    ubuf[hr:hr + tm, :] = _dot(hn.astype(BF16), wu_ref[...])
    fcw = fcw_ref[...]
    u = fcw[FFN_CONV - 1:FFN_CONV, :] * ubuf[hr:hr + tm, :] + fcb_ref[...]
    for j in range(1, FFN_CONV):
        u = u + fcw[FFN_CONV - 1 - j:FFN_CONV - j, :] * ubuf[hr - j * shift:hr - j * shift + tm, :]
    tail = ubuf[tm:tm + hr, :]
    tail_ref[...] = tail
    ubuf[0:hr, :] = tail
    act = (_silu(u[:, :dff]) * u[:, dff:]).astype(BF16)
    y_ref[...] = hres + _dot(act, wd_ref[...])


def _post(x2d, og, od, hist, w_out, g_ffn, w_up, fcw, fcb, w_down, *, nseq, tm, shift, hr):
    n, d = x2d.shape
    gw = og.shape[1]
    dff2 = w_up.shape[1]
    nt = n // (nseq * tm)
    row = lambda b, i: (b * nt + i, 0)
    in_specs = [
        pl.BlockSpec((tm, d), row), pl.BlockSpec((tm, gw), row), pl.BlockSpec((tm, od.shape[1]), row),
        pl.BlockSpec((None, hr, dff2), lambda b, i: (b, 0, 0)),
        _const_spec(w_out.shape), _const_spec((1, d)), _const_spec(w_up.shape),
        _const_spec((FFN_CONV, dff2)), _const_spec((1, dff2)), _const_spec(w_down.shape),
    ]
    out_shape = (jax.ShapeDtypeStruct((n, d), F32), jax.ShapeDtypeStruct((nseq, hr, dff2), F32))
    out_specs = (pl.BlockSpec((tm, d), row), pl.BlockSpec((None, hr, dff2), lambda b, i: (b, 0, 0)))
    return pl.pallas_call(
        functools.partial(_post_kernel, tm=tm, shift=shift, hr=hr, dff=dff2 // 2),
        grid=(nseq, nt), in_specs=in_specs, out_specs=out_specs, out_shape=out_shape,
        scratch_shapes=[pltpu.VMEM((hr + tm, dff2), F32)],
        compiler_params=_cparams(("parallel", "arbitrary")), name="post",
    )(x2d, og, od, hist, w_out, g_ffn, w_up, fcw, fcb, w_down)


def _lane_pad(vec, offset):
    out = jnp.zeros((1, LANES), F32)
    return lax.dynamic_update_slice(out, vec.reshape(1, -1).astype(F32), (0, offset))


def _layer(l, x_prompt, x_sample, state_gdn_conv, state_gdn_s, cache_k, cache_v, page_table, state_ffn_conv, wl):
    (attn_norm_g, w_in, gdn_conv_w, gdn_a_log, gdn_dt_bias, gdn_out_norm_g, diff_q_norm_g, diff_k_norm_g,
     lq1, lk1, lq2, lk2, diff_subln_g, w_out, ffn_norm_g, w_up, ffn_conv_w, ffn_conv_b, w_down) = wl
    nbp, seq, d = x_prompt.shape
    nbs, nts, _ = x_sample.shape
    nh, dk = GDN_HEADS, GDN_DK
    gw = nh * dk
    dw = DIFF_HEADS * DIFF_DV
    lam_init = 0.8 - 0.6 * math.exp(-0.3 * l)

    c_b = 4 * gw
    c_d = c_b + 2 * nh
    w_g = w_in[:, :c_b].astype(BF16)
    w_d = w_in[:, c_d:].astype(BF16)
    w_t = jnp.transpose(jnp.concatenate(
        [w_in[:, c_b:c_d], jnp.zeros((d, LANES - 2 * nh), w_in.dtype), w_in[:, c_d + 2 * dw:]], axis=1)).astype(BF16)
    reps = dw // DIFF_DQK
    gq_t = jnp.tile(diff_q_norm_g.reshape(1, -1), (1, reps))
    gk_t = jnp.tile(diff_k_norm_g.reshape(1, -1), (1, reps))
    gid = jnp.arange(dw) // DIFF_DQK
    grp = jnp.where(gid[:, None] == gid[None, :], 1.0 / DIFF_DQK, 0.0).astype(BF16)
    g_attn = attn_norm_g.reshape(1, d)
    alr, dtr = _lane_pad(gdn_a_log, nh), _lane_pad(gdn_dt_bias, nh)
    alc = jnp.transpose(alr[:, :SUBLANES])
    dtc = jnp.transpose(dtr[:, :SUBLANES])
    gn = gdn_out_norm_g.reshape(1, dk)
    vecs = [v.reshape(1, -1) for v in (lq1, lk1, lq2, lk2)]
    sg = diff_subln_g.reshape(1, -1)
    w_out_b, w_up_b, w_down_b = w_out.astype(BF16), w_up.astype(BF16), w_down.astype(BF16)
    g_ffn = ffn_norm_g.reshape(1, d)
    fcb = ffn_conv_b.reshape(1, -1)
    dff2 = w_up.shape[1]

    xp2 = x_prompt.reshape(nbp * seq, d)
    tm_p = 256
    (z, gcol, grow, qn, kn, kb, vb, tail_g, knew, vnew, qa, ka, vt) = _inproj_gdn(
        xp2, g_attn, w_g, w_d, w_t, gq_t, gk_t, grp, gdn_conv_w, alr, dtr, alc, dtc,
        nb=nbp, tm=tm_p, gw=gw, dw=dw)
    og, s_prompt = _gdn_prompt(qn, kn, kb, vb, gcol, grow, z, gn, nb=nbp, seq=seq, tl=128)
    od = _attn_prompt(qa, ka, vt, *vecs, sg, nb=nbp, seq=seq, tq=512, lam_init=lam_init)
    hr_p = SUBLANES
    y_p, tail_p = _post(xp2, og, od, jnp.zeros((nbp, hr_p, dff2), F32), w_out_b, g_ffn, w_up_b,
                        ffn_conv_w, fcb, w_down_b, nseq=nbp, tm=256, shift=1, hr=hr_p)
    out_p = (y_p.reshape(nbp, seq, d),
             tail_g[:, SUBLANES - (GDN_CONV - 1):, :],
             s_prompt,
             knew.reshape(nbp, seq, DIFF_HEADS, DIFF_DV),
             vnew.reshape(nbp, seq, DIFF_HEADS, DIFF_DV),
             tail_p[:, hr_p - (FFN_CONV - 1):, :])

    xs2 = jnp.transpose(x_sample, (1, 0, 2)).reshape(nts * nbs, d)
    qkv, z, ba, knew, vnew, qa, _, _ = _inproj(xs2, g_attn, w_g, w_d, w_t, gq_t, gk_t, grp,
                                               tm=min(256, nts * nbs), gw=gw, dw=dw)
    qkv_tm = qkv.reshape(nts, nbs, 3 * gw)
    hist_tm = jnp.transpose(state_gdn_conv, (1, 0, 2))
    og_tm, s_sample = _gdn_sample(qkv_tm, hist_tm, ba.reshape(nts, nbs, LANES), z.reshape(nts, nbs, gw),
                                  state_gdn_s, gdn_conv_w, alr, dtr, gn, bb=8)
    conv_all = jnp.concatenate([hist_tm, qkv_tm], axis=0)
    conv_s = jnp.transpose(conv_all[nts:], (1, 0, 2))
    to_bm = lambda a: jnp.transpose(a.reshape(nts, nbs, -1), (1, 0, 2))
    q_bm = to_bm(qa)
    q2 = jnp.concatenate([q_bm, q_bm], axis=1)
    pad = jnp.zeros((nbs, nts, dw), F32)
    kn_bm, vn_bm = to_bm(knew), to_bm(vnew)
    od_bm = _attn_sample(page_table, q2, jnp.concatenate([kn_bm, pad], axis=1),
                         jnp.concatenate([vn_bm, pad], axis=1), cache_k, cache_v, *vecs, sg,
                         nt=nts, lam_init=lam_init)
    od_tm = jnp.transpose(od_bm[:, :nts, :], (1, 0, 2)).reshape(nts * nbs, dw).astype(BF16)
    hr_s = (FFN_CONV - 1) * nbs
    hist_f = jnp.transpose(state_ffn_conv, (1, 0, 2)).reshape(1, hr_s, dff2)
    y_s, tail_s = _post(xs2, og_tm.reshape(nts * nbs, gw), od_tm, hist_f, w_out_b, g_ffn, w_up_b,
                        ffn_conv_w, fcb, w_down_b, nseq=1, tm=nbs, shift=nbs, hr=hr_s)
    out_s = (jnp.transpose(y_s.reshape(nts, nbs, d), (1, 0, 2)),
             conv_s,
             s_sample,
             kn_bm.reshape(nbs, nts, DIFF_HEADS, DIFF_DV),
             vn_bm.reshape(nbs, nts, DIFF_HEADS, DIFF_DV),
             jnp.transpose(tail_s.reshape(FFN_CONV - 1, nbs, dff2), (1, 0, 2)))
    return out_p, out_s


def kernel(x_prompt, x_sample, state_gdn_conv, state_gdn_S, cache_k, cache_v, page_table, state_ffn_conv, attn_norm_g, w_in, gdn_conv_w, gdn_A_log, gdn_dt_bias, gdn_out_norm_g, diff_q_norm_g, diff_k_norm_g, diff_lambda_q1, diff_lambda_k1, diff_lambda_q2, diff_lambda_k2, diff_subln_g, w_out, ffn_norm_g, w_up, ffn_conv_w, ffn_conv_b, w_down):
    depth = w_in.shape[0]
    hp, hs = x_prompt, x_sample
    outs_p, outs_s = [], []
    for l in range(depth):
        wl = (attn_norm_g[l], w_in[l], gdn_conv_w[l], gdn_A_log[l], gdn_dt_bias[l], gdn_out_norm_g[l],
              diff_q_norm_g[l], diff_k_norm_g[l], diff_lambda_q1[l], diff_lambda_k1[l], diff_lambda_q2[l],
              diff_lambda_k2[l], diff_subln_g[l], w_out[l], ffn_norm_g[l], w_up[l], ffn_conv_w[l],
              ffn_conv_b[l], w_down[l])
        out_p, out_s = _layer(l, hp, hs, state_gdn_conv[l], state_gdn_S[l], cache_k[l], cache_v[l],
                              page_table, state_ffn_conv[l], wl)
        hp, hs = out_p[0], out_s[0]
        outs_p.append(out_p[1:])
        outs_s.append(out_s[1:])
    stack = lambda outs, i: jnp.stack([o[i] for o in outs])
    return (hp, hs) + tuple(stack(outs_p, i) for i in range(5)) + tuple(stack(outs_s, i) for i in range(5))
```

```python
import functools
import math

import jax
import jax.numpy as jnp
from jax import lax
from jax.experimental import pallas as pl
from jax.experimental.pallas import tpu as pltpu

F32 = jnp.float32
BF16 = jnp.bfloat16
EPS = 1e-6
NEG = -1e30

GDN_HEADS = 4
GDN_DK = 128
GDN_CONV = 4
GDN_CHUNK = 64
DIFF_HEADS = 4
DIFF_DV = 128
DIFF_DQK = 64
FFN_CONV = 3
LANES = 128
SUBLANES = 8
VMEM_LIMIT = 56 * 1024 * 1024


def _cparams(sem):
    return pltpu.CompilerParams(dimension_semantics=sem, vmem_limit_bytes=VMEM_LIMIT)


def _const_spec(shape):
    nd = len(shape)
    return pl.BlockSpec(shape, lambda *_: (0,) * nd, pipeline_mode=pl.Buffered(1))


def _dot(a, b):
    return jnp.dot(a, b, preferred_element_type=F32)


def _dot_nt(a, b):
    return lax.dot_general(a, b, (((1,), (1,)), ((), ())), preferred_element_type=F32)


def _dot_tn(a, b):
    return lax.dot_general(a, b, (((0,), (0,)), ((), ())), preferred_element_type=F32)


def _softplus(x):
    return jnp.maximum(x, 0.0) + jnp.log1p(jnp.exp(-jnp.abs(x)))


def _silu(x):
    return x * jax.nn.sigmoid(x)


def _split3(x):
    hi = x.astype(BF16)
    r = x - hi.astype(F32)
    mid = r.astype(BF16)
    lo = (r - mid.astype(F32)).astype(BF16)
    return hi, mid, lo


def _inproj_core(x_ref, g_ref, wg_ref, wd_ref, wt_ref, gq_ref, gk_ref,
                 knew_ref, vnew_ref, qa_ref, ka_ref, vt_ref, gw, dw):
    tm = x_ref.shape[0]
    x = x_ref[...]
    xn = x * lax.rsqrt(jnp.mean(x * x, axis=-1, keepdims=True) + EPS) * g_ref[...]
    xb = xn.astype(BF16)
    proj = _dot(xb, wg_ref[...])
    pd = _dot(xb, wd_ref[...])
    tr = _dot_nt(wt_ref[...], xb)
    dq = pd[:, :dw]
    dk = pd[:, dw:2 * dw]
    dv = pd[:, 2 * dw:]
    vt_ref[...] = jnp.transpose(dv).astype(BF16)
    low =lax.broadcasted_iota(jnp.int32, (tm, DIFF_DV), 1) < DIFF_DQK

    def half_rms(t):
        out = []
        for h in range(DIFF_HEADS):
            th = t[:, h * DIFF_DV:(h + 1) * DIFF_DV]
            sq = th * th
            s1 = jnp.sum(jnp.where(low, sq, 0.0), axis=-1, keepdims=True)
            s2 = jnp.sum(jnp.where(low, 0.0, sq), axis=-1, keepdims=True)
            r1 = lax.rsqrt(s1 * (1.0 / DIFF_DQK) + EPS)
            r2 = lax.rsqrt(s2 * (1.0 / DIFF_DQK) + EPS)
            out.append(th * jnp.where(low, r1, r2))
        return jnp.concatenate(out, axis=1)

    dqn = half_rms(dq) * gq_ref[...]
    dkn = half_rms(dk) * gk_ref[...]
    for h in range(DIFF_HEADS):
        head_rows = pl.ds(h, tm, stride=DIFF_HEADS)
        knew_ref[head_rows, :] = dkn[:, h * DIFF_DV:(h + 1) * DIFF_DV]
        vnew_ref[head_rows, :] = dv[:, h * DIFF_DV:(h + 1) * DIFF_DV]
    qa_ref[...] = (dqn * (DIFF_DQK ** -0.5 * math.log2(math.e))).astype(BF16)
    ka_ref[...] = dkn.astype(BF16)
    return proj, tr


def _inproj_kernel(x_ref, g_ref, wg_ref, wd_ref, wt_ref, gq_ref, gk_ref,
                   qkv_ref, z_ref, ba_ref, knew_ref, vnew_ref, qa_ref, ka_ref, vt_ref, *, gw, dw):
    proj, trow = _inproj_core(x_ref, g_ref, wg_ref, wd_ref, wt_ref, gq_ref, gk_ref,
                              knew_ref, vnew_ref, qa_ref, ka_ref, vt_ref, gw, dw)
    qkv_ref[...] = proj[:, :3 * gw]
    z_ref[...] = proj[:, 3 * gw:]
    ba_ref[...] = jnp.transpose(trow)


def _inproj_gdn_kernel(x_ref, g_ref, wg_ref, wd_ref, wt_ref, gq_ref, gk_ref,
                       cw_ref, alr_ref, dtr_ref, alc_ref, dtc_ref,
                       z_ref, gcol_ref, grow_ref, qn_ref, kn_ref, kb_ref, vb_ref, tail_ref,
                       knew_ref, vnew_ref, qa_ref, ka_ref, vt_ref, xbuf, *, gw, dw):
    i = pl.program_id(1)
    tm = x_ref.shape[0]
    nh, dk = GDN_HEADS, GDN_DK
    hr = SUBLANES

    @pl.when(i == 0)
    def _():
        xbuf[0:hr, :] = jnp.zeros((hr, 3 * gw), F32)

    proj, trow = _inproj_core(x_ref, g_ref, wg_ref, wd_ref, wt_ref, gq_ref, gk_ref,
                              knew_ref, vnew_ref, qa_ref, ka_ref, vt_ref, gw, dw)
    z_ref[...] = proj[:, 3 * gw:]
    xbuf[hr:hr + tm, :] = proj[:, :3 * gw]
    cw = cw_ref[...]
    y = cw[GDN_CONV - 1:GDN_CONV, :] * xbuf[hr:hr + tm, :]
    for j in range(1, GDN_CONV):
        y = y + cw[GDN_CONV - 1 - j:GDN_CONV - j, :] * xbuf[hr - j:hr - j + tm, :]
    tail = xbuf[tm:tm + hr, :]
    tail_ref[...] = tail
    xbuf[0:hr, :] = tail
    y = _silu(y)
    ba = jnp.transpose(trow)
    beta_c = jax.nn.sigmoid(ba)
    gcol_ref[...] = -jnp.exp(alr_ref[...]) * _softplus(ba + dtr_ref[...])
    grow_ref[...] = -jnp.exp(alc_ref[...]) * _softplus(trow[:SUBLANES] + dtc_ref[...])
    for h in range(nh):
        ls = slice(h * dk, (h + 1) * dk)
        q = y[:, h * dk:(h + 1) * dk]
        k = y[:, gw + h * dk:gw + (h + 1) * dk]
        v = y[:, 2 * gw + h * dk:2 * gw + (h + 1) * dk]
        qn = q * lax.rsqrt(jnp.sum(q * q, axis=-1, keepdims=True) + EPS) * (dk ** -0.5)
        kn = k * lax.rsqrt(jnp.sum(k * k, axis=-1, keepdims=True) + EPS)
        beta = jnp.broadcast_to(beta_c[:, h:h + 1], (tm, dk))
        qn_ref[:, ls] = qn.astype(BF16)
        kn_ref[:, ls] = kn.astype(BF16)
        kb_ref[:, ls] = (kn * beta).astype(BF16)
        vb_ref[:, ls] = (v * beta).astype(BF16)


def _attn_out(n, tm, dw):
    return (
        jax.ShapeDtypeStruct((n * DIFF_HEADS, DIFF_DV), F32),
        jax.ShapeDtypeStruct((n * DIFF_HEADS, DIFF_DV), F32),
        jax.ShapeDtypeStruct((n, dw), BF16),
        jax.ShapeDtypeStruct((n, dw), BF16),
        jax.ShapeDtypeStruct((n // tm, dw, tm), BF16),
    )


def _inproj(x2d, g_attn, w_g, w_d, w_t, gq_t, gk_t, *, tm, gw, dw):
    n, d = x2d.shape
    row = lambda i: (i, 0)
    out_shape = (
        jax.ShapeDtypeStruct((n, 3 * gw), F32),
        jax.ShapeDtypeStruct((n, gw), F32),
        jax.ShapeDtypeStruct((n, LANES), F32),
    ) + _attn_out(n, tm, dw)
    out_specs = (
        pl.BlockSpec((tm, 3 * gw), row), pl.BlockSpec((tm, gw), row), pl.BlockSpec((tm, LANES), row),
        pl.BlockSpec((tm * DIFF_HEADS, DIFF_DV), row), pl.BlockSpec((tm * DIFF_HEADS, DIFF_DV), row),
        pl.BlockSpec((tm, dw), row), pl.BlockSpec((tm, dw), row),
        pl.BlockSpec((None, dw, tm), lambda i: (i, 0, 0)),
    )
    in_specs = [
        pl.BlockSpec((tm, d), row), _const_spec((1, d)), _const_spec(w_g.shape), _const_spec(w_d.shape),
        _const_spec(w_t.shape), _const_spec((1, dw)), _const_spec((1, dw)),
    ]
    return pl.pallas_call(
        functools.partial(_inproj_kernel, gw=gw, dw=dw),
        grid=(n // tm,), in_specs=in_specs, out_specs=out_specs, out_shape=out_shape,
        compiler_params=_cparams(("parallel",)), name="inproj",
    )(x2d, g_attn, w_g, w_d, w_t, gq_t, gk_t)


def _inproj_gdn(x2d, g_attn, w_g, w_d, w_t, gq_t, gk_t, cw, alr, dtr, alc, dtc, *, nb, tm, gw, dw):
    n, d = x2d.shape
    nt = n // (nb * tm)
    row = lambda b, i: (b * nt + i, 0)
    out_shape = (
        jax.ShapeDtypeStruct((n, gw), F32),
        jax.ShapeDtypeStruct((n, LANES), F32),
        jax.ShapeDtypeStruct((nb, SUBLANES, n // nb), F32),
        jax.ShapeDtypeStruct((n, gw), BF16),
        jax.ShapeDtypeStruct((n, gw), BF16),
        jax.ShapeDtypeStruct((n, gw), BF16),
        jax.ShapeDtypeStruct((n, gw), BF16),
        jax.ShapeDtypeStruct((nb, SUBLANES, 3 * gw), F32),
    ) + _attn_out(n, tm, dw)
    out_specs = (
        pl.BlockSpec((tm, gw), row), pl.BlockSpec((tm, LANES), row),
        pl.BlockSpec((None, SUBLANES, tm), lambda b, i: (b, 0, i)),
        pl.BlockSpec((tm, gw), row), pl.BlockSpec((tm, gw), row), pl.BlockSpec((tm, gw), row),
        pl.BlockSpec((tm, gw), row),
        pl.BlockSpec((None, SUBLANES, 3 * gw), lambda b, i: (b, 0, 0)),
        pl.BlockSpec((tm * DIFF_HEADS, DIFF_DV), row), pl.BlockSpec((tm * DIFF_HEADS, DIFF_DV), row),
        pl.BlockSpec((tm, dw), row), pl.BlockSpec((tm, dw), row),
        pl.BlockSpec((None, dw, tm), lambda b, i: (b * nt + i, 0, 0)),
    )
    in_specs = [
        pl.BlockSpec((tm, d), row), _const_spec((1, d)), _const_spec(w_g.shape), _const_spec(w_d.shape),
        _const_spec(w_t.shape), _const_spec((1, dw)), _const_spec((1, dw)),
        _const_spec(cw.shape), _const_spec((1, LANES)), _const_spec((1, LANES)),
        _const_spec((SUBLANES, 1)), _const_spec((SUBLANES, 1)),
    ]
    return pl.pallas_call(
        functools.partial(_inproj_gdn_kernel, gw=gw, dw=dw),
        grid=(nb, nt), in_specs=in_specs, out_specs=out_specs, out_shape=out_shape,
        scratch_shapes=[pltpu.VMEM((tm + SUBLANES, 3 * gw), F32)],
        compiler_params=_cparams(("parallel", "arbitrary")), name="inproj_gdn",
    )(x2d, g_attn, w_g, w_d, w_t, gq_t, gk_t, cw, alr, dtr, alc, dtc)


def _gdn_prompt_kernel(qn_ref, kn_ref, kb_ref, vb_ref, gcol_ref, grow_ref, z_ref, gn_ref,
                       og_ref, sout_ref, s_scr, *, tl):
    i = pl.program_id(0)
    nb = qn_ref.shape[0]
    nh, dk, c = GDN_HEADS, GDN_DK, GDN_CHUNK
    gw = nh * dk

    @pl.when(i == 0)
    def _():
        s_scr[...] = jnp.zeros_like(s_scr)

    g_rep = jnp.concatenate([jnp.broadcast_to(gcol_ref[b][:, nh + h:nh + h + 1], (tl, dk))
                             for b in range(nb) for h in range(nh)], axis=1)
    g_r = jnp.concatenate([grow_ref[b] for b in range(nb)], axis=0)

    ii = lax.broadcasted_iota(jnp.int32, (tl, tl), 0)
    jj = lax.broadcasted_iota(jnp.int32, (tl, tl), 1)
    same = lax.shift_right_logical(ii, 6) == lax.shift_right_logical(jj, 6)
    tri = jnp.where(same & (ii >= jj), 1.0, 0.0).astype(BF16)
    triu = jnp.where(same & (ii <= jj), 1.0, 0.0).astype(BF16)
    gc_rep = sum(_dot(tri, p) for p in _split3(g_rep))
    gc_r = sum(_dot(p, triu) for p in _split3(g_r))

    pr = 2 * c
    pi = lax.broadcasted_iota(jnp.int32, (pr, pr), 0)
    pj = lax.broadcasted_iota(jnp.int32, (pr, pr), 1)
    psame = lax.shift_right_logical(pi, 6) == lax.shift_right_logical(pj, 6)
    lower = psame & (pi >= pj)
    strict = psame & (pi > pj)
    eye = jnp.where(pi == pj, 1.0, 0.0).astype(F32)
    zeros_c = jnp.zeros((c, dk), F32)

    npair = tl // pr
    seqs = [(b, h) for b in range(nb) for h in range(nh)]
    hp = [(b, h, p) for b, h in seqs for p in range(npair)]

    def tile(ref, b, h, p):
        return ref[b, p * pr:(p + 1) * pr, h * dk:(h + 1) * dk]

    gcp = {(b, h, p): gc_rep[p * pr:(p + 1) * pr, b * gw + h * dk:b * gw + (h + 1) * dk] for b, h, p in hp}
    dec = {(b, h, p): jnp.exp(jnp.where(
        lower, gcp[b, h, p] - gc_r[b * SUBLANES + nh + h:b * SUBLANES + nh + h + 1, p * pr:(p + 1) * pr], NEG))
        for b, h, p in hp}
    kpb = {x: tile(kn_ref, *x) for x in hp}
    kbp = {x: tile(kb_ref, *x) for x in hp}
    m = {x: -jnp.where(strict, _dot_nt(kbp[x], kpb[x]) * dec[x], 0.0) for x in hp}
    a = {x: eye + m[x] for x in hp}
    pw = {x: _dot(m[x].astype(BF16), m[x].astype(BF16)) for x in hp}
    for it in range(1, 6):
        for x in hp:
            pwb = pw[x].astype(BF16)
            if it < 5:
                res = _dot(pwb, jnp.concatenate([a[x].astype(BF16), pwb], axis=1))
                a[x] = a[x] + res[:, :pr]
                pw[x] = res[:, pr:]
            else:
                a[x] = a[x] + _dot(pwb, a[x].astype(BF16))
    egc = {x: jnp.exp(gcp[x]) for x in hp}
    sol = {x: _dot(a[x].astype(BF16),
                   jnp.concatenate([tile(vb_ref, *x), (kbp[x].astype(F32) * egc[x]).astype(BF16)], axis=1))
           for x in hp}
    qk = {x: (_dot_nt(tile(qn_ref, *x), kpb[x]) * dec[x]).astype(BF16) for x in hp}
    qg = {x: tile(qn_ref, *x).astype(F32) * egc[x] for x in hp}

    s_cur = {x: s_scr[x] for x in seqs}
    o_rows = {x: [] for x in seqs}
    for p in range(npair):
        for cc in range(2):
            c0 = cc * c
            ws = {}
            for b, h in seqs:
                wq = jnp.concatenate([sol[b, h, p][c0:c0 + c, dk:], qg[b, h, p][c0:c0 + c]], axis=0)
                ws[b, h] = _dot(wq.astype(BF16), s_cur[b, h].astype(BF16))
            for b, h in seqs:
                g_c0 = gcp[b, h, p][c0:c0 + c]
                glast = g_c0[c - 1:c, :]
                kd = kpb[b, h, p][c0:c0 + c].astype(F32) * jnp.exp(glast - g_c0)
                v_new = (sol[b, h, p][c0:c0 + c, :dk] - ws[b, h][:c]).astype(BF16)
                s_cur[b, h] = s_cur[b, h] * jnp.exp(glast) + _dot_tn(kd.astype(BF16), v_new)
                zc = jnp.zeros_like(v_new)
                vn_pad = jnp.concatenate([v_new, zc] if cc == 0 else [zc, v_new], axis=0)
                o_rows[b, h].append(ws[b, h][c:] + _dot(qk[b, h, p][c0:c0 + c], vn_pad))
    for b, h in seqs:
        s_scr[b, h] = s_cur[b, h]
        o = jnp.concatenate(o_rows[b, h], axis=0)
        o = o * lax.rsqrt(jnp.mean(o * o, axis=-1, keepdims=True) + EPS) * gn_ref[...]
        og_ref[b, :, h * dk:(h + 1) * dk] = (o * _silu(z_ref[b, :, h * dk:(h + 1) * dk])).astype(og_ref.dtype)

    @pl.when(i == pl.num_programs(0) - 1)
    def _():
        sout_ref[...] = s_scr[...]


def _gdn_prompt(qn, kn, kb, vb, gcol, grow, z, gn, *, nb, seq, tl):
    nh, dk = GDN_HEADS, GDN_DK
    gw = nh * dk
    nt = seq // tl
    tok = lambda i: (0, i, 0)
    wide = pl.BlockSpec((nb, tl, gw), tok)
    per_seq = lambda a: a.reshape(nb, seq, a.shape[-1])
    in_specs = [
        wide, wide, wide, wide, pl.BlockSpec((nb, tl, LANES), tok),
        pl.BlockSpec((nb, SUBLANES, tl), lambda i: (0, 0, i)),
        wide, _const_spec((1, dk)),
    ]
    out_shape = (jax.ShapeDtypeStruct((nb, seq, gw), BF16),
                 jax.ShapeDtypeStruct((nb, nh, dk, dk), F32))
    out_specs = (wide, pl.BlockSpec((nb, nh, dk, dk), lambda i: (0, 0, 0, 0)))
    og, s_out = pl.pallas_call(
        functools.partial(_gdn_prompt_kernel, tl=tl),
        grid=(nt,), in_specs=in_specs, out_specs=out_specs, out_shape=out_shape,
        scratch_shapes=[pltpu.VMEM((nb, nh, dk, dk), F32)],
        compiler_params=_cparams(("arbitrary",)), name="gdn_prompt",
    )(per_seq(qn), per_seq(kn), per_seq(kb), per_seq(vb), per_seq(gcol), grow, per_seq(z), gn)
    return og.reshape(nb * seq, gw), s_out


def _gdn_sample_pre_kernel(qkv_ref, hist_ref, ba_ref, cw_ref, alr_ref, dtr_ref, r_ref, intra_ref, *, nt):
    nh, dk = GDN_HEADS, GDN_DK
    gw = nh * dk
    nb = qkv_ref.shape[1]
    cw = cw_ref[...]
    nhist = GDN_CONV - 1
    xp = [hist_ref[j] for j in range(nhist)] + [qkv_ref[t] for t in range(nt)]
    ys = []
    for t in range(nt):
        y = cw[0:1, :] * xp[t]
        for j in range(1, GDN_CONV):
            y = y + cw[j:j + 1, :] * xp[t + j]
        ys.append(_silu(y))
    beta, gc = [], []
    for t in range(nt):
        ba = ba_ref[t]
        beta_c = jax.nn.sigmoid(ba)
        g_c = -jnp.exp(alr_ref[...]) * _softplus(ba + dtr_ref[...])
        beta.append([jnp.broadcast_to(beta_c[:, h:h + 1], (nb, dk)) for h in range(nh)])
        g_t = [jnp.broadcast_to(g_c[:, nh + h:nh + h + 1], (nb, dk)) for h in range(nh)]
        gc.append(g_t if t == 0 else [gc[t - 1][h] + g_t[h] for h in range(nh)])
    zrow = jnp.zeros((nb, dk), F32)
    for h in range(nh):
        q, k, v = [], [], []
        for t in range(nt):
            qt = ys[t][:, h * dk:(h + 1) * dk]
            kt = ys[t][:, gw + h * dk:gw + (h + 1) * dk]
            q.append(qt * lax.rsqrt(jnp.sum(qt * qt, axis=-1, keepdims=True) + EPS) * (dk ** -0.5))
            k.append(kt * lax.rsqrt(jnp.sum(kt * kt, axis=-1, keepdims=True) + EPS))
            v.append(ys[t][:, 2 * gw + h * dk:2 * gw + (h + 1) * dk])
        g = [gc[t][h] for t in range(nt)]
        us, ws = [], []
        for t in range(nt):
            b_t = beta[t][h]
            u_t = v[t] * b_t
            w_t = k[t] * b_t * jnp.exp(g[t])
            for s in range(t):
                m_ts = b_t * jnp.sum(k[t] * k[s], axis=-1, keepdims=True) * jnp.exp(g[t] - g[s])
                u_t = u_t - m_ts * us[s]
                w_t = w_t - m_ts * ws[s]
            us.append(u_t)
            ws.append(w_t)
        lane = slice(h * dk, (h + 1) * dk)
        for t in range(nt):
            r_ref[t, :, lane] = ws[t]
            r_ref[nt + t, :, lane] = q[t] * jnp.exp(g[t])
            r_ref[2 * nt + t, :, lane] = us[t]
            r_ref[3 * nt + t, :, lane] = k[t] * jnp.exp(g[nt - 1] - g[t])
            for s in range(nt):
                if s <= t:
                    intra_ref[t * nt + s, :, lane] = (jnp.sum(q[t] * k[s], axis=-1, keepdims=True)
                                                       * jnp.exp(g[t] - g[s]))
                else:
                    intra_ref[t * nt + s, :, lane] = zrow
        r_ref[4 * nt, :, lane] = jnp.exp(g[nt - 1])
        for r in range(4 * nt + 1, r_ref.shape[0]):
            r_ref[r, :, lane] = zrow


def _gdn_sample_state_kernel(r_ref, s_ref, ws_ref, sout_ref, *, nt, bb):
    nh = GDN_HEADS
    rows = lax.broadcasted_iota(jnp.int32, (2 * nt, GDN_DK), 0)
    inst = [(bi, h) for bi in range(bb) for h in range(nh)]
    ws = {x: _dot(r_ref[x[0], x[1], 0:2 * nt, :].astype(BF16), s_ref[x].astype(BF16)) for x in inst}
    for x in inst:
        ud = r_ref[x[0], x[1], 2 * nt:4 * nt, :]
        vn = jnp.where(rows < nt, ud - ws[x], 0.0)
        kd = jnp.where(rows < nt, pltpu.roll(ud, nt, axis=0), 0.0)
        ws_ref[x] = ws[x]
        sout_ref[x] = (s_ref[x] * r_ref[x[0], x[1], 4 * nt:4 * nt + 1, :]
                       + _dot_tn(kd.astype(BF16), vn.astype(BF16)))


def _gdn_sample_post_kernel(ws_ref, r_ref, intra_ref, z_ref, gn_ref, og_ref, *, nt):
    nh, dk = GDN_HEADS, GDN_DK
    for h in range(nh):
        lane = slice(h * dk, (h + 1) * dk)
        vn = [r_ref[2 * nt + t, :, lane] - ws_ref[t, :, lane] for t in range(nt)]
        for t in range(nt):
            o = ws_ref[nt + t, :, lane]
            for s in range(t + 1):
                o = o + intra_ref[t * nt + s, :, lane] * vn[s]
            o = o * lax.rsqrt(jnp.mean(o * o, axis=-1, keepdims=True) + EPS) * gn_ref[...]
            og_ref[t, :, lane] = (o * _silu(z_ref[t, :, lane])).astype(og_ref.dtype)


def _gdn_sample(qkv_tm, hist_tm, ba_tm, z_tm, state_s, cw, alr, dtr, gn, *, bb):
    nt, nb, _ = qkv_tm.shape
    nh, dk = GDN_HEADS, GDN_DK
    gw = nh * dk
    nr = 6 * nt
    full = lambda shape: pl.BlockSpec(shape, lambda *_: (0,) * len(shape))
    r_tm, intra = pl.pallas_call(
        functools.partial(_gdn_sample_pre_kernel, nt=nt),
        grid=(1,),
        in_specs=[full(qkv_tm.shape), full(hist_tm.shape), full(ba_tm.shape), full(cw.shape),
                  full(alr.shape), full(dtr.shape)],
        out_specs=(full((nr, nb, gw)), full((nt * nt, nb, gw))),
        out_shape=(jax.ShapeDtypeStruct((nr, nb, gw), F32), jax.ShapeDtypeStruct((nt * nt, nb, gw), F32)),
        compiler_params=_cparams(("arbitrary",)), name="gdn_sample_pre",
    )(qkv_tm, hist_tm, ba_tm, cw, alr, dtr)
    r_bm = jnp.transpose(r_tm.reshape(nr, nb, nh, dk), (1, 2, 0, 3))
    blk = lambda b: (b, 0, 0, 0)
    ws_bm, s_new = pl.pallas_call(
        functools.partial(_gdn_sample_state_kernel, nt=nt, bb=bb),
        grid=(nb // bb,),
        in_specs=[pl.BlockSpec((bb, nh, nr, dk), blk), pl.BlockSpec((bb, nh, dk, dk), blk)],
        out_specs=(pl.BlockSpec((bb, nh, 2 * nt, dk), blk), pl.BlockSpec((bb, nh, dk, dk), blk)),
        out_shape=(jax.ShapeDtypeStruct((nb, nh, 2 * nt, dk), F32), jax.ShapeDtypeStruct((nb, nh, dk, dk), F32)),
        compiler_params=_cparams(("parallel",)), name="gdn_sample_state",
    )(r_bm, state_s)
    ws_tm = jnp.transpose(ws_bm, (2, 0, 1, 3)).reshape(2 * nt, nb, gw)
    og = pl.pallas_call(
        functools.partial(_gdn_sample_post_kernel, nt=nt),
        grid=(1,),
        in_specs=[full(ws_tm.shape), full(r_tm.shape), full(intra.shape), full(z_tm.shape), full(gn.shape)],
        out_specs=full((nt, nb, gw)),
        out_shape=jax.ShapeDtypeStruct((nt, nb, gw), BF16),
        compiler_params=_cparams(("arbitrary",)), name="gdn_sample_post",
    )(ws_tm, r_tm, intra, z_tm, gn)
    return og, s_new


def _lambda(lq1_ref, lk1_ref, lq2_ref, lk2_ref, lam_init):
    s1 = jnp.sum(lq1_ref[...] * lk1_ref[...], axis=-1, keepdims=True)
    s2 = jnp.sum(lq2_ref[...] * lk2_ref[...], axis=-1, keepdims=True)
    return jnp.exp(s1) - jnp.exp(s2) + lam_init


def _subln(o, g_ref, lam_init):
    return o * lax.rsqrt(jnp.mean(o * o, axis=-1, keepdims=True) + EPS) * g_ref[...] * (1.0 - lam_init)


def _attn_prompt_kernel(q_ref, k_ref, vt_ref, lq1_ref, lk1_ref, lq2_ref, lk2_ref, sg_ref, o_ref,
                        sa_scr, sb_scr, m_scr, acc_scr, *, tq, lam_init):
    qi = pl.program_id(1)
    nh, dv, dqk = DIFF_HEADS, DIFF_DV, DIFF_DQK
    lam = _lambda(lq1_ref, lk1_ref, lq2_ref, lk2_ref, lam_init)
    lane = lax.broadcasted_iota(jnp.int32, (tq, dv), 1)
    kidx = lax.broadcasted_iota(jnp.int32, (tq, tq), 0)
    qidx = lax.broadcasted_iota(jnp.int32, (tq, tq), 1)
    causal = kidx <= qidx
    nsum = 2 * SUBLANES
    ones_rows = jnp.ones((nsum, tq), BF16)
    lanes = [slice(h * dv, (h + 1) * dv) for h in range(nh)]
    qs = []
    for h in range(nh):
        q = q_ref[:, lanes[h]]
        zero = jnp.zeros_like(q)
        qs.append((jnp.where(lane < dqk, q, zero), jnp.where(lane >= dqk, q, zero)))

    nvt = tq // vt_ref.shape[-1]
    nchain = 2 * nh

    def scores(dst, j):
        for h in range(nh):
            kb = k_ref[pl.ds(pl.multiple_of(j * tq, tq), tq), lanes[h]]
            for c in range(2):
                dst[2 * h + c] = _dot_nt(kb, qs[h][c])

    def softmax_pv(src, j, masked):
        ps, alphas = [], []
        for x in range(nchain):
            s_x = jnp.where(causal, src[x], NEG) if masked else src[x]
            m_i = m_scr[x]
            m_new = jnp.maximum(m_i, jnp.max(s_x, axis=0, keepdims=True))
            m_scr[x] = m_new
            alphas.append(jnp.exp2(m_i - m_new))
            ps.append(jnp.exp2(s_x - m_new).astype(BF16))
        for h in range(nh):
            vt = jnp.concatenate([vt_ref[j * nvt + t, lanes[h], :] for t in range(nvt)], axis=1)
            vt = jnp.concatenate([vt, ones_rows], axis=0)
            for c in range(2):
                x = 2 * h + c
                acc_scr[x] = alphas[x] * acc_scr[x] + _dot(vt, ps[x])

    m_scr[...] = jnp.full(m_scr.shape, NEG, F32)
    acc_scr[...] = jnp.zeros(acc_scr.shape, F32)
    scores(sa_scr, 0)

    def two_blocks(i, _):
        scores(sb_scr, 2 * i + 1)
        softmax_pv(sa_scr, 2 * i, False)
        scores(sa_scr, 2 * i + 2)
        softmax_pv(sb_scr, 2 * i + 1, False)
        return 0

    lax.fori_loop(0, lax.shift_right_logical(qi, 1), two_blocks, 0)
    odd = lax.rem(qi, 2) == 1

    @pl.when(odd)
    def _():
        scores(sb_scr, qi)
        softmax_pv(sa_scr, qi - 1, False)
        softmax_pv(sb_scr, qi, True)

    @pl.when(jnp.logical_not(odd))
    def _():
        softmax_pv(sa_scr, qi, True)

    for h in range(nh):
        ls = lanes[h]
        a1, a2 = acc_scr[2 * h], acc_scr[2 * h + 1]
        ot = a1[:dv] / a1[dv:dv + 1] - lam * (a2[:dv] / a2[dv:dv + 1])
        ot = ot * lax.rsqrt(jnp.mean(ot * ot, axis=0, keepdims=True) + EPS) * (1.0 - lam_init)
        o_ref[:, ls] = (jnp.transpose(ot) * sg_ref[...]).astype(o_ref.dtype)


def _attn_prompt(qa, ka, vt, lq1, lk1, lq2, lk2, sg, *, nb, seq, tq, lam_init):
    nh, dv = DIFF_HEADS, DIFF_DV
    wd = nh * dv
    nq = seq // tq
    qa3, ka3 = qa.reshape(nb, seq, wd), ka.reshape(nb, seq, wd)
    tv = vt.shape[-1]
    nv = seq // tv
    vt4 = vt.reshape(nb, nv, wd, tv)
    vec = _const_spec((1, DIFF_DQK))
    out = pl.pallas_call(
        functools.partial(_attn_prompt_kernel, tq=tq, lam_init=lam_init),
        grid=(nb, nq),
        in_specs=[pl.BlockSpec((None, tq, wd), lambda b, i: (b, i, 0)),
                  pl.BlockSpec((None, seq, wd), lambda b, i: (b, 0, 0)),
                  pl.BlockSpec((None, nv, wd, tv), lambda b, i: (b, 0, 0, 0)),
                  vec, vec, vec, vec, _const_spec((1, dv))],
        out_specs=pl.BlockSpec((None, tq, wd), lambda b, i: (b, i, 0)),
        out_shape=jax.ShapeDtypeStruct((nb, seq, wd), BF16),
        scratch_shapes=[pltpu.VMEM((2 * nh, tq, tq), F32), pltpu.VMEM((2 * nh, tq, tq), F32),
                        pltpu.VMEM((2 * nh, 1, tq), F32), pltpu.VMEM((2 * nh, dv + 2 * SUBLANES, tq), F32)],
        compiler_params=_cparams(("parallel", "arbitrary")), name="attn_prompt",
    )(qa3, ka3, vt4, lq1, lk1, lq2, lk2, sg)
    return out.reshape(nb * seq, wd)


def _attn_sample_kernel(pt_ref, q_ref, kn_ref, vn_ref, *rest, n_pages, nt, lam_init):
    k_refs = rest[:n_pages]
    v_refs = rest[n_pages:2 * n_pages]
    lq1_ref, lk1_ref, lq2_ref, lk2_ref, sg_ref, o_ref = rest[2 * n_pages:]
    del pt_ref
    nh, dv, dqk = DIFF_HEADS, DIFF_DV, DIFF_DQK
    page = k_refs[0].shape[0] // nh
    nr = 2 * nt
    lam = _lambda(lq1_ref, lk1_ref, lq2_ref, lk2_ref, lam_init)
    row = lax.broadcasted_iota(jnp.int32, (nr, dv), 0)
    lane = lax.broadcasted_iota(jnp.int32, (nr, dv), 1)
    first = row < nt
    keep = jnp.logical_xor(lane >= dqk, first)
    tpos = jnp.where(first, row, row - nt)
    new_ok = (lane < nt) & (lane <= tpos)
    zpad = jnp.zeros((page - nr, dv), F32)
    lanes = [slice(h * dv, (h + 1) * dv) for h in range(nh)]
    head_rows = [pl.ds(h, page, stride=nh) for h in range(nh)]
    groups = [list(range(j, min(j + 2, n_pages))) for j in range(0, n_pages, 2)]

    def past(refs, h, grp):
        return jnp.concatenate([refs[j][head_rows[h], :] for j in grp], axis=0).astype(BF16)

    s_all = []
    for h in range(nh):
        qh = q_ref[:, lanes[h]]
        qz = jnp.where(keep, qh, jnp.zeros_like(qh))
        s_h = [_dot_nt(qz, past(k_refs, h, grp)) for grp in groups]
        knew = jnp.concatenate([kn_ref[:, lanes[h]], zpad], axis=0)
        s_h.append(jnp.where(new_ok, _dot_nt(qz, knew.astype(BF16)), NEG))
        s_all.append(s_h)
    p_all, l_all = [], []
    for h in range(nh):
        m = jnp.max(s_all[h][0], axis=-1, keepdims=True)
        for s in s_all[h][1:]:
            m = jnp.maximum(m, jnp.max(s, axis=-1, keepdims=True))
        ps = [jnp.exp2(s - m) for s in s_all[h]]
        l = jnp.sum(ps[0], axis=-1, keepdims=True)
        for p in ps[1:]:
            l = l + jnp.sum(p, axis=-1, keepdims=True)
        p_all.append([p.astype(BF16) for p in ps])
        l_all.append(l)
    for h in range(nh):
        acc = _dot(p_all[h][-1], jnp.concatenate([vn_ref[:, lanes[h]], zpad], axis=0).astype(BF16))
        for gi, grp in enumerate(groups):
            acc = acc + _dot(p_all[h][gi], past(v_refs, h, grp))
        o2 = acc / l_all[h]
        o = o2 - lam * pltpu.roll(o2, nt, axis=0)
        o_ref[:, lanes[h]] = _subln(o, sg_ref, lam_init)


def _attn_sample(page_table, q2, kn8, vn8, cache_k, cache_v, lq1, lk1, lq2, lk2, sg, *, nt, lam_init):
    nb, n_pages = page_table.shape
    nh, dv = DIFF_HEADS, DIFF_DV
    page = cache_k.shape[1]
    wd = nh * dv
    ck = cache_k.reshape(cache_k.shape[0], page * nh, dv)
    cv = cache_v.reshape(cache_v.shape[0], page * nh, dv)
    nr = 2 * nt
    small = pl.BlockSpec((None, nr, wd), lambda b, pt: (b, 0, 0))

    def page_spec(j):
        return pl.BlockSpec((None, page * nh, dv), lambda b, pt: (pt[b, j], 0, 0))

    vec = pl.BlockSpec((1, DIFF_DQK), lambda b, pt: (0, 0))
    grid_spec = pltpu.PrefetchScalarGridSpec(
        num_scalar_prefetch=1, grid=(nb,),
        in_specs=[small, small, small] + [page_spec(j) for j in range(n_pages)] * 2
        + [vec, vec, vec, vec, pl.BlockSpec((1, dv), lambda b, pt: (0, 0))],
        out_specs=small)
    return pl.pallas_call(
        functools.partial(_attn_sample_kernel, n_pages=n_pages, nt=nt, lam_init=lam_init),
        grid_spec=grid_spec, out_shape=jax.ShapeDtypeStruct((nb, nr, wd), F32),
        compiler_params=_cparams(("parallel",)), name="attn_sample",
    )(page_table, q2, kn8, vn8, *([ck] * n_pages), *([cv] * n_pages), lq1, lk1, lq2, lk2, sg)


def _post_kernel(x_ref, og_ref, od_ref, hist_ref, wo_ref, gf_ref, wu_ref, fcw_ref, fcb_ref, wd_ref,
                 y_ref, tail_ref, ubuf, *, tm, shift, hr, dff):
    i = pl.program_id(1)
    gw = og_ref.shape[1]

    @pl.when(i == 0)
    def _():
        ubuf[0:hr, :] = hist_ref[...]

    hres = x_ref[...] + _dot(og_ref[...], wo_ref[0:gw, :]) + _dot(od_ref[...], wo_ref[gw:, :])
    hn = hres * lax.rsqrt(jnp.mean(hres * hres, axis=-1, keepdims=True) + EPS) * gf_ref[...]
    ubuf[hr:hr + tm, :] = _dot(hn.astype(BF16), wu_ref[...])
    fcw = fcw_ref[...]
    u = fcw[FFN_CONV - 1:FFN_CONV, :] * ubuf[hr:hr + tm, :] + fcb_ref[...]
    for j in range(1, FFN_CONV):
        u = u + fcw[FFN_CONV - 1 - j:FFN_CONV - j, :] * ubuf[hr - j * shift:hr - j * shift + tm, :]
    tail = ubuf[tm:tm + hr, :]
    tail_ref[...] = tail
    ubuf[0:hr, :] = tail
    act = (_silu(u[:, :dff]) * u[:, dff:]).astype(BF16)
    y_ref[...] = hres + _dot(act, wd_ref[...])


def _post(x2d, og, od, hist, w_out, g_ffn, w_up, fcw, fcb, w_down, *, nseq, tm, shift, hr):
    n, d = x2d.shape
    gw = og.shape[1]
    dff2 = w_up.shape[1]
    nt = n // (nseq * tm)
    row = lambda b, i: (b * nt + i, 0)
    in_specs = [
        pl.BlockSpec((tm, d), row), pl.BlockSpec((tm, gw), row), pl.BlockSpec((tm, od.shape[1]), row),
        pl.BlockSpec((None, hr, dff2), lambda b, i: (b, 0, 0)),
        _const_spec(w_out.shape), _const_spec((1, d)), _const_spec(w_up.shape),
        _const_spec((FFN_CONV, dff2)), _const_spec((1, dff2)), _const_spec(w_down.shape),
    ]
    out_shape = (jax.ShapeDtypeStruct((n, d), F32), jax.ShapeDtypeStruct((nseq, hr, dff2), F32))
    out_specs = (pl.BlockSpec((tm, d), row), pl.BlockSpec((None, hr, dff2), lambda b, i: (b, 0, 0)))
    return pl.pallas_call(
        functools.partial(_post_kernel, tm=tm, shift=shift, hr=hr, dff=dff2 // 2),
        grid=(nseq, nt), in_specs=in_specs, out_specs=out_specs, out_shape=out_shape,
        scratch_shapes=[pltpu.VMEM((hr + tm, dff2), F32)],
        compiler_params=_cparams(("parallel", "arbitrary")), name="post",
    )(x2d, og, od, hist, w_out, g_ffn, w_up, fcw, fcb, w_down)


def _lane_pad(vec, offset):
    out = jnp.zeros((1, LANES), F32)
    return lax.dynamic_update_slice(out, vec.reshape(1, -1).astype(F32), (0, offset))


def _layer(l, x_prompt, x_sample, state_gdn_conv, state_gdn_s, cache_k, cache_v, page_table, state_ffn_conv, wl):
    (attn_norm_g, w_in, gdn_conv_w, gdn_a_log, gdn_dt_bias, gdn_out_norm_g, diff_q_norm_g, diff_k_norm_g,
     lq1, lk1, lq2, lk2, diff_subln_g, w_out, ffn_norm_g, w_up, ffn_conv_w, ffn_conv_b, w_down) = wl
    nbp, seq, d = x_prompt.shape
    nbs, nts, _ = x_sample.shape
    nh, dk = GDN_HEADS, GDN_DK
    gw = nh * dk
    dw = DIFF_HEADS * DIFF_DV
    lam_init = 0.8 - 0.6 * math.exp(-0.3 * l)

    c_b = 4 * gw
    c_d = c_b + 2 * nh
    w_g = w_in[:, :c_b].astype(BF16)
    w_d = w_in[:, c_d:].astype(BF16)
    w_t = jnp.transpose(jnp.concatenate(
        [w_in[:, c_b:c_d], jnp.zeros((d, LANES - 2 * nh), w_in.dtype)], axis=1)).astype(BF16)
    reps = dw // DIFF_DQK
    gq_t = jnp.tile(diff_q_norm_g.reshape(1, -1), (1, reps))
    gk_t = jnp.tile(diff_k_norm_g.reshape(1, -1), (1, reps))
    g_attn = attn_norm_g.reshape(1, d)
    alr, dtr = _lane_pad(gdn_a_log, nh), _lane_pad(gdn_dt_bias, nh)
    alc = jnp.transpose(alr[:, :SUBLANES])
    dtc = jnp.transpose(dtr[:, :SUBLANES])
    gn = gdn_out_norm_g.reshape(1, dk)
    vecs = [v.reshape(1, -1) for v in (lq1, lk1, lq2, lk2)]
    sg = diff_subln_g.reshape(1, -1)
    w_out_b, w_up_b, w_down_b = w_out.astype(BF16), w_up.astype(BF16), w_down.astype(BF16)
    g_ffn = ffn_norm_g.reshape(1, d)
    fcb = ffn_conv_b.reshape(1, -1)
    dff2 = w_up.shape[1]

    xp2 = x_prompt.reshape(nbp * seq, d)
    tm_p = 512
    (z, gcol, grow, qn, kn, kb, vb, tail_g, knew, vnew, qa, ka, vt) = _inproj_gdn(
        xp2, g_attn, w_g, w_d, w_t, gq_t, gk_t, gdn_conv_w, alr, dtr, alc, dtc,
        nb=nbp, tm=tm_p, gw=gw, dw=dw)
    og, s_prompt = _gdn_prompt(qn, kn, kb, vb, gcol, grow, z, gn, nb=nbp, seq=seq, tl=128)
    od = _attn_prompt(qa, ka, vt, *vecs, sg, nb=nbp, seq=seq, tq=512, lam_init=lam_init)
    hr_p = SUBLANES
    y_p, tail_p = _post(xp2, og, od, jnp.zeros((nbp, hr_p, dff2), F32), w_out_b, g_ffn, w_up_b,
                        ffn_conv_w, fcb, w_down_b, nseq=nbp, tm=256, shift=1, hr=hr_p)
    out_p = (y_p.reshape(nbp, seq, d),
             tail_g[:, SUBLANES - (GDN_CONV - 1):, :],
             s_prompt,
             knew.reshape(nbp, seq, DIFF_HEADS, DIFF_DV),
             vnew.reshape(nbp, seq, DIFF_HEADS, DIFF_DV),
             tail_p[:, hr_p - (FFN_CONV - 1):, :])

    xs2 = jnp.transpose(x_sample, (1, 0, 2)).reshape(nts * nbs, d)
    qkv, z, ba, knew, vnew, qa, _, _ = _inproj(xs2, g_attn, w_g, w_d, w_t, gq_t, gk_t,
                                               tm=min(256, nts * nbs), gw=gw, dw=dw)
    qkv_tm = qkv.reshape(nts, nbs, 3 * gw)
    hist_tm = jnp.transpose(state_gdn_conv, (1, 0, 2))
    og_tm, s_sample = _gdn_sample(qkv_tm, hist_tm, ba.reshape(nts, nbs, LANES), z.reshape(nts, nbs, gw),
                                  state_gdn_s, gdn_conv_w, alr, dtr, gn, bb=16)
    conv_all = jnp.concatenate([hist_tm, qkv_tm], axis=0)
    conv_s = jnp.transpose(conv_all[nts:], (1, 0, 2))
    to_bm = lambda a: jnp.transpose(a.reshape(nts, nbs, -1), (1, 0, 2))
    q_bm = to_bm(qa)
    q2 = jnp.concatenate([q_bm, q_bm], axis=1)
    pad = jnp.zeros((nbs, nts, dw), F32)
    kn_bm, vn_bm = to_bm(knew), to_bm(vnew)
    od_bm = _attn_sample(page_table, q2, jnp.concatenate([kn_bm, pad], axis=1),
                         jnp.concatenate([vn_bm, pad], axis=1), cache_k, cache_v, *vecs, sg,
                         nt=nts, lam_init=lam_init)
    od_tm = jnp.transpose(od_bm[:, :nts, :], (1, 0, 2)).reshape(nts * nbs, dw).astype(BF16)
    hr_s = (FFN_CONV - 1) * nbs
    hist_f = jnp.transpose(state_ffn_conv, (1, 0, 2)).reshape(1, hr_s, dff2)
    y_s, tail_s = _post(xs2, og_tm.reshape(nts * nbs, gw), od_tm, hist_f, w_out_b, g_ffn, w_up_b,
                        ffn_conv_w, fcb, w_down_b, nseq=1, tm=2 * nbs, shift=nbs, hr=hr_s)
    out_s = (jnp.transpose(y_s.reshape(nts, nbs, d), (1, 0, 2)),
             conv_s,
             s_sample,
             kn_bm.reshape(nbs, nts, DIFF_HEADS, DIFF_DV),
             vn_bm.reshape(nbs, nts, DIFF_HEADS, DIFF_DV),
             jnp.transpose(tail_s.reshape(FFN_CONV - 1, nbs, dff2), (1, 0, 2)))
    return out_p, out_s


def kernel(x_prompt, x_sample, state_gdn_conv, state_gdn_S, cache_k, cache_v, page_table, state_ffn_conv, attn_norm_g, w_in, gdn_conv_w, gdn_A_log, gdn_dt_bias, gdn_out_norm_g, diff_q_norm_g, diff_k_norm_g, diff_lambda_q1, diff_lambda_k1, diff_lambda_q2, diff_lambda_k2, diff_subln_g, w_out, ffn_norm_g, w_up, ffn_conv_w, ffn_conv_b, w_down):
    depth = w_in.shape[0]
    hp, hs = x_prompt, x_sample
    outs_p, outs_s = [], []
    for l in range(depth):
        wl = (attn_norm_g[l], w_in[l], gdn_conv_w[l], gdn_A_log[l], gdn_dt_bias[l], gdn_out_norm_g[l],
              diff_q_norm_g[l], diff_k_norm_g[l], diff_lambda_q1[l], diff_lambda_k1[l], diff_lambda_q2[l],
              diff_lambda_k2[l], diff_subln_g[l], w_out[l], ffn_norm_g[l], w_up[l], ffn_conv_w[l],
              ffn_conv_b[l], w_down[l])
        out_p, out_s = _layer(l, hp, hs, state_gdn_conv[l], state_gdn_S[l], cache_k[l], cache_v[l],
                              page_table, state_ffn_conv[l], wl)
        hp, hs = out_p[0], out_s[0]
        outs_p.append(out_p[1:])
        outs_s.append(out_s[1:])
    stack = lambda outs, i: jnp.stack([o[i] for o in outs])
    return (hp, hs) + tuple(stack(outs_p, i) for i in range(5)) + tuple(stack(outs_s, i) for i in range(5))
```

```python
import functools
import math

import jax
import jax.numpy as jnp
from jax import lax
from jax.experimental import pallas as pl
from jax.experimental.pallas import tpu as pltpu

F32 = jnp.float32
BF16 = jnp.bfloat16
EPS = 1e-6
NEG = -1e30

GDN_HEADS = 4
GDN_DK = 128
GDN_CONV = 4
GDN_CHUNK = 64
DIFF_HEADS = 4
DIFF_DV = 128
DIFF_DQK = 64
FFN_CONV = 3
LANES = 128
SUBLANES = 8
V7X_VMEM_BYTES = 64 * 1024 * 1024
VMEM_LIMIT = V7X_VMEM_BYTES - 8 * 1024 * 1024


class _Tiles:
    inproj_rows = 512
    gdn_tokens = 256
    attn_tile = 512
    post_rows = 512
    post_col_groups = 1
    sample_rows = 256
    state_seqs = 16
    paged_seqs = 2


def _cparams(sem):
    return pltpu.CompilerParams(dimension_semantics=sem, vmem_limit_bytes=VMEM_LIMIT)


def _const_spec(shape):
    nd = len(shape)
    return pl.BlockSpec(shape, lambda *_: (0,) * nd, pipeline_mode=pl.Buffered(1))


def _dot(a, b):
    return jnp.dot(a, b, preferred_element_type=F32)


def _dot_nt(a, b):
    return lax.dot_general(a, b, (((1,), (1,)), ((), ())), preferred_element_type=F32)


def _dot_tn(a, b):
    return lax.dot_general(a, b, (((0,), (0,)), ((), ())), preferred_element_type=F32)


def _softplus(x):
    return jnp.maximum(x, 0.0) + jnp.log1p(jnp.exp(-jnp.abs(x)))


def _silu(x):
    return x * jax.nn.sigmoid(x)


def _split3(x):
    hi = x.astype(BF16)
    r = x - hi.astype(F32)
    mid = r.astype(BF16)
    lo = (r - mid.astype(F32)).astype(BF16)
    return hi, mid, lo


def _inproj_core(x_ref, g_ref, wg_ref, wd_ref, wt_ref, gq_ref, gk_ref,
                 knew_ref, vnew_ref, qa_ref, ka_ref, vt_ref, gw, dw):
    tm = x_ref.shape[0]
    x = x_ref[...]
    xn = x * lax.rsqrt(jnp.mean(x * x, axis=-1, keepdims=True) + EPS) * g_ref[...]
    xb = xn.astype(BF16)
    proj = _dot(xb, wg_ref[...])
    pd = _dot(xb, wd_ref[...])
    tr = _dot_nt(wt_ref[...], xb)
    dq = pd[:, :dw]
    dk = pd[:, dw:2 * dw]
    dv = pd[:, 2 * dw:]
    vt_ref[...] = jnp.transpose(dv).astype(BF16)
    low =lax.broadcasted_iota(jnp.int32, (tm, DIFF_DV), 1) < DIFF_DQK

    def half_rms(t):
        out = []
        for h in range(DIFF_HEADS):
            th = t[:, h * DIFF_DV:(h + 1) * DIFF_DV]
            sq = th * th
            s1 = jnp.sum(jnp.where(low, sq, 0.0), axis=-1, keepdims=True)
            s2 = jnp.sum(jnp.where(low, 0.0, sq), axis=-1, keepdims=True)
            r1 = lax.rsqrt(s1 * (1.0 / DIFF_DQK) + EPS)
            r2 = lax.rsqrt(s2 * (1.0 / DIFF_DQK) + EPS)
            out.append(th * jnp.where(low, r1, r2))
        return jnp.concatenate(out, axis=1)

    dqn = half_rms(dq) * gq_ref[...]
    dkn = half_rms(dk) * gk_ref[...]
    for h in range(DIFF_HEADS):
        head_rows = pl.ds(h, tm, stride=DIFF_HEADS)
        knew_ref[head_rows, :] = dkn[:, h * DIFF_DV:(h + 1) * DIFF_DV]
        vnew_ref[head_rows, :] = dv[:, h * DIFF_DV:(h + 1) * DIFF_DV]
    qa_ref[...] = (dqn * (DIFF_DQK ** -0.5 * math.log2(math.e))).astype(BF16)
    ka_ref[...] = dkn.astype(BF16)
    return proj, tr


def _inproj_kernel(x_ref, g_ref, wg_ref, wd_ref, wt_ref, gq_ref, gk_ref,
                   qkv_ref, z_ref, ba_ref, knew_ref, vnew_ref, qa_ref, ka_ref, vt_ref, *, gw, dw):
    proj, trow = _inproj_core(x_ref, g_ref, wg_ref, wd_ref, wt_ref, gq_ref, gk_ref,
                              knew_ref, vnew_ref, qa_ref, ka_ref, vt_ref, gw, dw)
    qkv_ref[...] = proj[:, :3 * gw]
    z_ref[...] = proj[:, 3 * gw:]
    ba_ref[...] = jnp.transpose(trow)


def _inproj_gdn_kernel(x_ref, g_ref, wg_ref, wd_ref, wt_ref, gq_ref, gk_ref,
                       cw_ref, alr_ref, dtr_ref, alc_ref, dtc_ref,
                       z_ref, gcol_ref, grow_ref, qn_ref, kn_ref, kb_ref, vb_ref, tail_ref,
                       knew_ref, vnew_ref, qa_ref, ka_ref, vt_ref, xbuf, *, gw, dw):
    i = pl.program_id(1)
    tm = x_ref.shape[0]
    nh, dk = GDN_HEADS, GDN_DK
    hr = SUBLANES

    @pl.when(i == 0)
    def _():
        xbuf[0:hr, :] = jnp.zeros((hr, 3 * gw), F32)

    proj, trow = _inproj_core(x_ref, g_ref, wg_ref, wd_ref, wt_ref, gq_ref, gk_ref,
                              knew_ref, vnew_ref, qa_ref, ka_ref, vt_ref, gw, dw)
    z_ref[...] = proj[:, 3 * gw:]
    xbuf[hr:hr + tm, :] = proj[:, :3 * gw]
    cw = cw_ref[...]
    y = cw[GDN_CONV - 1:GDN_CONV, :] * xbuf[hr:hr + tm, :]
    for j in range(1, GDN_CONV):
        y = y + cw[GDN_CONV - 1 - j:GDN_CONV - j, :] * xbuf[hr - j:hr - j + tm, :]
    tail = xbuf[tm:tm + hr, :]
    tail_ref[...] = tail
    xbuf[0:hr, :] = tail
    y = _silu(y)
    ba = jnp.transpose(trow)
    beta_c = jax.nn.sigmoid(ba)
    gcol_ref[...] = -jnp.exp(alr_ref[...]) * _softplus(ba + dtr_ref[...])
    grow_ref[...] = -jnp.exp(alc_ref[...]) * _softplus(trow[:SUBLANES] + dtc_ref[...])
    for h in range(nh):
        ls = slice(h * dk, (h + 1) * dk)
        q = y[:, h * dk:(h + 1) * dk]
        k = y[:, gw + h * dk:gw + (h + 1) * dk]
        v = y[:, 2 * gw + h * dk:2 * gw + (h + 1) * dk]
        qn = q * lax.rsqrt(jnp.sum(q * q, axis=-1, keepdims=True) + EPS) * (dk ** -0.5)
        kn = k * lax.rsqrt(jnp.sum(k * k, axis=-1, keepdims=True) + EPS)
        beta = jnp.broadcast_to(beta_c[:, h:h + 1], (tm, dk))
        qn_ref[:, ls] = qn.astype(BF16)
        kn_ref[:, ls] = kn.astype(BF16)
        kb_ref[:, ls] = (kn * beta).astype(BF16)
        vb_ref[:, ls] = (v * beta).astype(BF16)


def _attn_out(n, tm, dw):
    return (
        jax.ShapeDtypeStruct((n * DIFF_HEADS, DIFF_DV), F32),
        jax.ShapeDtypeStruct((n * DIFF_HEADS, DIFF_DV), F32),
        jax.ShapeDtypeStruct((n, dw), BF16),
        jax.ShapeDtypeStruct((n, dw), BF16),
        jax.ShapeDtypeStruct((n // tm, dw, tm), BF16),
    )


def _inproj(x2d, g_attn, w_g, w_d, w_t, gq_t, gk_t, *, tm, gw, dw):
    n, d = x2d.shape
    row = lambda i: (i, 0)
    out_shape = (
        jax.ShapeDtypeStruct((n, 3 * gw), F32),
        jax.ShapeDtypeStruct((n, gw), F32),
        jax.ShapeDtypeStruct((n, LANES), F32),
    ) + _attn_out(n, tm, dw)
    out_specs = (
        pl.BlockSpec((tm, 3 * gw), row), pl.BlockSpec((tm, gw), row), pl.BlockSpec((tm, LANES), row),
        pl.BlockSpec((tm * DIFF_HEADS, DIFF_DV), row), pl.BlockSpec((tm * DIFF_HEADS, DIFF_DV), row),
        pl.BlockSpec((tm, dw), row), pl.BlockSpec((tm, dw), row),
        pl.BlockSpec((None, dw, tm), lambda i: (i, 0, 0)),
    )
    in_specs = [
        pl.BlockSpec((tm, d), row), _const_spec((1, d)), _const_spec(w_g.shape), _const_spec(w_d.shape),
        _const_spec(w_t.shape), _const_spec((1, dw)), _const_spec((1, dw)),
    ]
    return pl.pallas_call(
        functools.partial(_inproj_kernel, gw=gw, dw=dw),
        grid=(n // tm,), in_specs=in_specs, out_specs=out_specs, out_shape=out_shape,
        compiler_params=_cparams(("parallel",)), name="inproj",
    )(x2d, g_attn, w_g, w_d, w_t, gq_t, gk_t)


def _inproj_gdn(x2d, g_attn, w_g, w_d, w_t, gq_t, gk_t, cw, alr, dtr, alc, dtc, *, nb, tm, gw, dw):
    n, d = x2d.shape
    nt = n // (nb * tm)
    row = lambda b, i: (b * nt + i, 0)
    out_shape = (
        jax.ShapeDtypeStruct((n, gw), F32),
        jax.ShapeDtypeStruct((n, LANES), F32),
        jax.ShapeDtypeStruct((nb, SUBLANES, n // nb), F32),
        jax.ShapeDtypeStruct((n, gw), BF16),
        jax.ShapeDtypeStruct((n, gw), BF16),
        jax.ShapeDtypeStruct((n, gw), BF16),
        jax.ShapeDtypeStruct((n, gw), BF16),
        jax.ShapeDtypeStruct((nb, SUBLANES, 3 * gw), F32),
    ) + _attn_out(n, tm, dw)
    out_specs = (
        pl.BlockSpec((tm, gw), row), pl.BlockSpec((tm, LANES), row),
        pl.BlockSpec((None, SUBLANES, tm), lambda b, i: (b, 0, i)),
        pl.BlockSpec((tm, gw), row), pl.BlockSpec((tm, gw), row), pl.BlockSpec((tm, gw), row),
        pl.BlockSpec((tm, gw), row),
        pl.BlockSpec((None, SUBLANES, 3 * gw), lambda b, i: (b, 0, 0)),
        pl.BlockSpec((tm * DIFF_HEADS, DIFF_DV), row), pl.BlockSpec((tm * DIFF_HEADS, DIFF_DV), row),
        pl.BlockSpec((tm, dw), row), pl.BlockSpec((tm, dw), row),
        pl.BlockSpec((None, dw, tm), lambda b, i: (b * nt + i, 0, 0)),
    )
    in_specs = [
        pl.BlockSpec((tm, d), row), _const_spec((1, d)), _const_spec(w_g.shape), _const_spec(w_d.shape),
        _const_spec(w_t.shape), _const_spec((1, dw)), _const_spec((1, dw)),
        _const_spec(cw.shape), _const_spec((1, LANES)), _const_spec((1, LANES)),
        _const_spec((SUBLANES, 1)), _const_spec((SUBLANES, 1)),
    ]
    return pl.pallas_call(
        functools.partial(_inproj_gdn_kernel, gw=gw, dw=dw),
        grid=(nb, nt), in_specs=in_specs, out_specs=out_specs, out_shape=out_shape,
        scratch_shapes=[pltpu.VMEM((tm + SUBLANES, 3 * gw), F32)],
        compiler_params=_cparams(("parallel", "arbitrary")), name="inproj_gdn",
    )(x2d, g_attn, w_g, w_d, w_t, gq_t, gk_t, cw, alr, dtr, alc, dtc)


def _gdn_prompt_kernel(qn_ref, kn_ref, kb_ref, vb_ref, gcol_ref, grow_ref, z_ref, gn_ref,
                       og_ref, sout_ref, s_scr, *, tl):
    i = pl.program_id(0)
    nb = qn_ref.shape[0]
    nh, dk, c = GDN_HEADS, GDN_DK, GDN_CHUNK
    gw = nh * dk

    @pl.when(i == 0)
    def _():
        s_scr[...] = jnp.zeros_like(s_scr)

    g_rep = jnp.concatenate([jnp.broadcast_to(gcol_ref[b][:, nh + h:nh + h + 1], (tl, dk))
                             for b in range(nb) for h in range(nh)], axis=1)
    g_r = jnp.concatenate([grow_ref[b] for b in range(nb)], axis=0)

    ii = lax.broadcasted_iota(jnp.int32, (tl, tl), 0)
    jj = lax.broadcasted_iota(jnp.int32, (tl, tl), 1)
    same = lax.shift_right_logical(ii, 6) == lax.shift_right_logical(jj, 6)
    tri = jnp.where(same & (ii >= jj), 1.0, 0.0).astype(BF16)
    triu = jnp.where(same & (ii <= jj), 1.0, 0.0).astype(BF16)
    gc_rep = sum(_dot(tri, p) for p in _split3(g_rep))
    gc_r = sum(_dot(p, triu) for p in _split3(g_r))

    pr = 2 * c
    pi = lax.broadcasted_iota(jnp.int32, (pr, pr), 0)
    pj = lax.broadcasted_iota(jnp.int32, (pr, pr), 1)
    psame = lax.shift_right_logical(pi, 6) == lax.shift_right_logical(pj, 6)
    lower = psame & (pi >= pj)
    strict = psame & (pi > pj)
    eye = jnp.where(pi == pj, 1.0, 0.0).astype(F32)
    zeros_c = jnp.zeros((c, dk), F32)

    npair = tl // pr
    seqs = [(b, h) for b in range(nb) for h in range(nh)]
    hp = [(b, h, p) for b, h in seqs for p in range(npair)]

    def tile(ref, b, h, p):
        return ref[b, p * pr:(p + 1) * pr, h * dk:(h + 1) * dk]

    gcp = {(b, h, p): gc_rep[p * pr:(p + 1) * pr, b * gw + h * dk:b * gw + (h + 1) * dk] for b, h, p in hp}
    dec = {(b, h, p): jnp.exp(jnp.where(
        lower, gcp[b, h, p] - gc_r[b * SUBLANES + nh + h:b * SUBLANES + nh + h + 1, p * pr:(p + 1) * pr], NEG))
        for b, h, p in hp}
    kpb = {x: tile(kn_ref, *x) for x in hp}
    kbp = {x: tile(kb_ref, *x) for x in hp}
    m = {x: -jnp.where(strict, _dot_nt(kbp[x], kpb[x]) * dec[x], 0.0) for x in hp}
    a = {x: eye + m[x] for x in hp}
    pw = {x: _dot(m[x].astype(BF16), m[x].astype(BF16)) for x in hp}
    for it in range(1, 6):
        for x in hp:
            pwb = pw[x].astype(BF16)
            if it < 5:
                res = _dot(pwb, jnp.concatenate([a[x].astype(BF16), pwb], axis=1))
                a[x] = a[x] + res[:, :pr]
                pw[x] = res[:, pr:]
            else:
                a[x] = a[x] + _dot(pwb, a[x].astype(BF16))
    egc = {x: jnp.exp(gcp[x]) for x in hp}
    sol = {x: _dot(a[x].astype(BF16),
                   jnp.concatenate([tile(vb_ref, *x), (kbp[x].astype(F32) * egc[x]).astype(BF16)], axis=1))
           for x in hp}
    qk = {x: (_dot_nt(tile(qn_ref, *x), kpb[x]) * dec[x]).astype(BF16) for x in hp}
    qg = {x: tile(qn_ref, *x).astype(F32) * egc[x] for x in hp}

    s_cur = {x: s_scr[x] for x in seqs}
    o_rows = {x: [] for x in seqs}
    for p in range(npair):
        for cc in range(2):
            c0 = cc * c
            ws = {}
            for b, h in seqs:
                wq = jnp.concatenate([sol[b, h, p][c0:c0 + c, dk:], qg[b, h, p][c0:c0 + c]], axis=0)
                ws[b, h] = _dot(wq.astype(BF16), s_cur[b, h].astype(BF16))
            for b, h in seqs:
                g_c0 = gcp[b, h, p][c0:c0 + c]
                glast = g_c0[c - 1:c, :]
                kd = kpb[b, h, p][c0:c0 + c].astype(F32) * jnp.exp(glast - g_c0)
                v_new = (sol[b, h, p][c0:c0 + c, :dk] - ws[b, h][:c]).astype(BF16)
                s_cur[b, h] = s_cur[b, h] * jnp.exp(glast) + _dot_tn(kd.astype(BF16), v_new)
                zc = jnp.zeros_like(v_new)
                vn_pad = jnp.concatenate([v_new, zc] if cc == 0 else [zc, v_new], axis=0)
                o_rows[b, h].append(ws[b, h][c:] + _dot(qk[b, h, p][c0:c0 + c], vn_pad))
    for b, h in seqs:
        s_scr[b, h] = s_cur[b, h]
        o = jnp.concatenate(o_rows[b, h], axis=0)
        o = o * lax.rsqrt(jnp.mean(o * o, axis=-1, keepdims=True) + EPS) * gn_ref[...]
        og_ref[b, :, h * dk:(h + 1) * dk] = (o * _silu(z_ref[b, :, h * dk:(h + 1) * dk])).astype(og_ref.dtype)

    @pl.when(i == pl.num_programs(0) - 1)
    def _():
        sout_ref[...] = s_scr[...]


def _gdn_prompt(qn, kn, kb, vb, gcol, grow, z, gn, *, nb, seq, tl):
    nh, dk = GDN_HEADS, GDN_DK
    gw = nh * dk
    nt = seq // tl
    tok = lambda i: (0, i, 0)
    wide = pl.BlockSpec((nb, tl, gw), tok)
    per_seq = lambda a: a.reshape(nb, seq, a.shape[-1])
    in_specs = [
        wide, wide, wide, wide, pl.BlockSpec((nb, tl, LANES), tok),
        pl.BlockSpec((nb, SUBLANES, tl), lambda i: (0, 0, i)),
        wide, _const_spec((1, dk)),
    ]
    out_shape = (jax.ShapeDtypeStruct((nb, seq, gw), BF16),
                 jax.ShapeDtypeStruct((nb, nh, dk, dk), F32))
    out_specs = (wide, pl.BlockSpec((nb, nh, dk, dk), lambda i: (0, 0, 0, 0)))
    og, s_out = pl.pallas_call(
        functools.partial(_gdn_prompt_kernel, tl=tl),
        grid=(nt,), in_specs=in_specs, out_specs=out_specs, out_shape=out_shape,
        scratch_shapes=[pltpu.VMEM((nb, nh, dk, dk), F32)],
        compiler_params=_cparams(("arbitrary",)), name="gdn_prompt",
    )(per_seq(qn), per_seq(kn), per_seq(kb), per_seq(vb), per_seq(gcol), grow, per_seq(z), gn)
    return og.reshape(nb * seq, gw), s_out


def _gdn_sample_pre_kernel(qkv_ref, hist_ref, ba_ref, cw_ref, alr_ref, dtr_ref, r_ref, intra_ref, *, nt):
    nh, dk = GDN_HEADS, GDN_DK
    gw = nh * dk
    nb = qkv_ref.shape[1]
    cw = cw_ref[...]
    nhist = GDN_CONV - 1
    xp = [hist_ref[j] for j in range(nhist)] + [qkv_ref[t] for t in range(nt)]
    ys = []
    for t in range(nt):
        y = cw[0:1, :] * xp[t]
        for j in range(1, GDN_CONV):
            y = y + cw[j:j + 1, :] * xp[t + j]
        ys.append(_silu(y))
    beta, gc = [], []
    for t in range(nt):
        ba = ba_ref[t]
        beta_c = jax.nn.sigmoid(ba)
        g_c = -jnp.exp(alr_ref[...]) * _softplus(ba + dtr_ref[...])
        beta.append([jnp.broadcast_to(beta_c[:, h:h + 1], (nb, dk)) for h in range(nh)])
        g_t = [jnp.broadcast_to(g_c[:, nh + h:nh + h + 1], (nb, dk)) for h in range(nh)]
        gc.append(g_t if t == 0 else [gc[t - 1][h] + g_t[h] for h in range(nh)])
    zrow = jnp.zeros((nb, dk), F32)
    for h in range(nh):
        q, k, v = [], [], []
        for t in range(nt):
            qt = ys[t][:, h * dk:(h + 1) * dk]
            kt = ys[t][:, gw + h * dk:gw + (h + 1) * dk]
            q.append(qt * lax.rsqrt(jnp.sum(qt * qt, axis=-1, keepdims=True) + EPS) * (dk ** -0.5))
            k.append(kt * lax.rsqrt(jnp.sum(kt * kt, axis=-1, keepdims=True) + EPS))
            v.append(ys[t][:, 2 * gw + h * dk:2 * gw + (h + 1) * dk])
        g = [gc[t][h] for t in range(nt)]
        us, ws = [], []
        for t in range(nt):
            b_t = beta[t][h]
            u_t = v[t] * b_t
            w_t = k[t] * b_t * jnp.exp(g[t])
            for s in range(t):
                m_ts = b_t * jnp.sum(k[t] * k[s], axis=-1, keepdims=True) * jnp.exp(g[t] - g[s])
                u_t = u_t - m_ts * us[s]
                w_t = w_t - m_ts * ws[s]
            us.append(u_t)
            ws.append(w_t)
        lane = slice(h * dk, (h + 1) * dk)
        for t in range(nt):
            r_ref[t, :, lane] = ws[t]
            r_ref[nt + t, :, lane] = q[t] * jnp.exp(g[t])
            r_ref[2 * nt + t, :, lane] = us[t]
            r_ref[3 * nt + t, :, lane] = k[t] * jnp.exp(g[nt - 1] - g[t])
            for s in range(nt):
                if s <= t:
                    intra_ref[t * nt + s, :, lane] = (jnp.sum(q[t] * k[s], axis=-1, keepdims=True)
                                                       * jnp.exp(g[t] - g[s]))
                else:
                    intra_ref[t * nt + s, :, lane] = zrow
        r_ref[4 * nt, :, lane] = jnp.exp(g[nt - 1])
        for r in range(4 * nt + 1, r_ref.shape[0]):
            r_ref[r, :, lane] = zrow


def _gdn_sample_state_kernel(r_ref, s_ref, ws_ref, sout_ref, *, nt, bb):
    nh = GDN_HEADS
    rows = lax.broadcasted_iota(jnp.int32, (2 * nt, GDN_DK), 0)
    inst = [(bi, h) for bi in range(bb) for h in range(nh)]
    ws = {x: _dot(r_ref[x[0], x[1], 0:2 * nt, :].astype(BF16), s_ref[x].astype(BF16)) for x in inst}
    for x in inst:
        ud = r_ref[x[0], x[1], 2 * nt:4 * nt, :]
        vn = jnp.where(rows < nt, ud - ws[x], 0.0)
        kd = jnp.where(rows < nt, pltpu.roll(ud, nt, axis=0), 0.0)
        ws_ref[x] = ws[x]
        sout_ref[x] = (s_ref[x] * r_ref[x[0], x[1], 4 * nt:4 * nt + 1, :]
                       + _dot_tn(kd.astype(BF16), vn.astype(BF16)))


def _gdn_sample_post_kernel(ws_ref, r_ref, intra_ref, z_ref, gn_ref, og_ref, *, nt):
    nh, dk = GDN_HEADS, GDN_DK
    for h in range(nh):
        lane = slice(h * dk, (h + 1) * dk)
        vn = [r_ref[2 * nt + t, :, lane] - ws_ref[t, :, lane] for t in range(nt)]
        for t in range(nt):
            o = ws_ref[nt + t, :, lane]
            for s in range(t + 1):
                o = o + intra_ref[t * nt + s, :, lane] * vn[s]
            o = o * lax.rsqrt(jnp.mean(o * o, axis=-1, keepdims=True) + EPS) * gn_ref[...]
            og_ref[t, :, lane] = (o * _silu(z_ref[t, :, lane])).astype(og_ref.dtype)


def _gdn_sample(qkv_tm, hist_tm, ba_tm, z_tm, state_s, cw, alr, dtr, gn, *, bb):
    nt, nb, _ = qkv_tm.shape
    nh, dk = GDN_HEADS, GDN_DK
    gw = nh * dk
    nr = 6 * nt
    full = lambda shape: pl.BlockSpec(shape, lambda *_: (0,) * len(shape))
    r_tm, intra = pl.pallas_call(
        functools.partial(_gdn_sample_pre_kernel, nt=nt),
        grid=(1,),
        in_specs=[full(qkv_tm.shape), full(hist_tm.shape), full(ba_tm.shape), full(cw.shape),
                  full(alr.shape), full(dtr.shape)],
        out_specs=(full((nr, nb, gw)), full((nt * nt, nb, gw))),
        out_shape=(jax.ShapeDtypeStruct((nr, nb, gw), F32), jax.ShapeDtypeStruct((nt * nt, nb, gw), F32)),
        compiler_params=_cparams(("arbitrary",)), name="gdn_sample_pre",
    )(qkv_tm, hist_tm, ba_tm, cw, alr, dtr)
    r_bm = jnp.transpose(r_tm.reshape(nr, nb, nh, dk), (1, 2, 0, 3))
    blk = lambda b: (b, 0, 0, 0)
    ws_bm, s_new = pl.pallas_call(
        functools.partial(_gdn_sample_state_kernel, nt=nt, bb=bb),
        grid=(nb // bb,),
        in_specs=[pl.BlockSpec((bb, nh, nr, dk), blk), pl.BlockSpec((bb, nh, dk, dk), blk)],
        out_specs=(pl.BlockSpec((bb, nh, 2 * nt, dk), blk), pl.BlockSpec((bb, nh, dk, dk), blk)),
        out_shape=(jax.ShapeDtypeStruct((nb, nh, 2 * nt, dk), F32), jax.ShapeDtypeStruct((nb, nh, dk, dk), F32)),
        compiler_params=_cparams(("parallel",)), name="gdn_sample_state",
    )(r_bm, state_s)
    ws_tm = jnp.transpose(ws_bm, (2, 0, 1, 3)).reshape(2 * nt, nb, gw)
    og = pl.pallas_call(
        functools.partial(_gdn_sample_post_kernel, nt=nt),
        grid=(1,),
        in_specs=[full(ws_tm.shape), full(r_tm.shape), full(intra.shape), full(z_tm.shape), full(gn.shape)],
        out_specs=full((nt, nb, gw)),
        out_shape=jax.ShapeDtypeStruct((nt, nb, gw), BF16),
        compiler_params=_cparams(("arbitrary",)), name="gdn_sample_post",
    )(ws_tm, r_tm, intra, z_tm, gn)
    return og, s_new


def _lambda(lq1_ref, lk1_ref, lq2_ref, lk2_ref, lam_init):
    s1 = jnp.sum(lq1_ref[...] * lk1_ref[...], axis=-1, keepdims=True)
    s2 = jnp.sum(lq2_ref[...] * lk2_ref[...], axis=-1, keepdims=True)
    return jnp.exp(s1) - jnp.exp(s2) + lam_init


def _subln(o, g_ref, lam_init):
    return o * lax.rsqrt(jnp.mean(o * o, axis=-1, keepdims=True) + EPS) * g_ref[...] * (1.0 - lam_init)


def _attn_prompt_kernel(q_ref, k_ref, vt_ref, lq1_ref, lk1_ref, lq2_ref, lk2_ref, sg_ref, o_ref,
                        sa_scr, sb_scr, m_scr, acc_scr, *, tq, lam_init):
    qi = pl.program_id(1)
    nh, dv, dqk = DIFF_HEADS, DIFF_DV, DIFF_DQK
    lam = _lambda(lq1_ref, lk1_ref, lq2_ref, lk2_ref, lam_init)
    lane = lax.broadcasted_iota(jnp.int32, (tq, dv), 1)
    kidx = lax.broadcasted_iota(jnp.int32, (tq, tq), 0)
    qidx = lax.broadcasted_iota(jnp.int32, (tq, tq), 1)
    causal = kidx <= qidx
    nsum = 2 * SUBLANES
    ones_rows = jnp.ones((nsum, tq), BF16)
    lanes = [slice(h * dv, (h + 1) * dv) for h in range(nh)]
    qs = []
    for h in range(nh):
        q = q_ref[:, lanes[h]]
        zero = jnp.zeros_like(q)
        qs.append((jnp.where(lane < dqk, q, zero), jnp.where(lane >= dqk, q, zero)))

    nvt = tq // vt_ref.shape[-1]
    nchain = 2 * nh

    def scores(dst, j):
        for h in range(nh):
            kb = k_ref[pl.ds(pl.multiple_of(j * tq, tq), tq), lanes[h]]
            for c in range(2):
                dst[2 * h + c] = _dot_nt(kb, qs[h][c])

    def softmax_pv(src, j, masked):
        ps, alphas = [], []
        for x in range(nchain):
            s_x = jnp.where(causal, src[x], NEG) if masked else src[x]
            m_i = m_scr[x]
            m_new = jnp.maximum(m_i, jnp.max(s_x, axis=0, keepdims=True))
            m_scr[x] = m_new
            alphas.append(jnp.exp2(m_i - m_new))
            ps.append(jnp.exp2(s_x - m_new).astype(BF16))
        for h in range(nh):
            vt = jnp.concatenate([vt_ref[j * nvt + t, lanes[h], :] for t in range(nvt)], axis=1)
            vt = jnp.concatenate([vt, ones_rows], axis=0)
            for c in range(2):
                x = 2 * h + c
                acc_scr[x] = alphas[x] * acc_scr[x] + _dot(vt, ps[x])

    m_scr[...] = jnp.full(m_scr.shape, NEG, F32)
    acc_scr[...] = jnp.zeros(acc_scr.shape, F32)
    scores(sa_scr, 0)

    def two_blocks(i, _):
        scores(sb_scr, 2 * i + 1)
        softmax_pv(sa_scr, 2 * i, False)
        scores(sa_scr, 2 * i + 2)
        softmax_pv(sb_scr, 2 * i + 1, False)
        return 0

    lax.fori_loop(0, lax.shift_right_logical(qi, 1), two_blocks, 0)
    odd = lax.rem(qi, 2) == 1

    @pl.when(odd)
    def _():
        scores(sb_scr, qi)
        softmax_pv(sa_scr, qi - 1, False)
        softmax_pv(sb_scr, qi, True)

    @pl.when(jnp.logical_not(odd))
    def _():
        softmax_pv(sa_scr, qi, True)

    for h in range(nh):
        ls = lanes[h]
        a1, a2 = acc_scr[2 * h], acc_scr[2 * h + 1]
        ot = a1[:dv] / a1[dv:dv + 1] - lam * (a2[:dv] / a2[dv:dv + 1])
        ot = ot * lax.rsqrt(jnp.mean(ot * ot, axis=0, keepdims=True) + EPS) * (1.0 - lam_init)
        o_ref[:, ls] = (jnp.transpose(ot) * sg_ref[...]).astype(o_ref.dtype)


def _attn_prompt(qa, ka, vt, lq1, lk1, lq2, lk2, sg, *, nb, seq, tq, lam_init):
    nh, dv = DIFF_HEADS, DIFF_DV
    wd = nh * dv
    nq = seq // tq
    qa3, ka3 = qa.reshape(nb, seq, wd), ka.reshape(nb, seq, wd)
    tv = vt.shape[-1]
    nv = seq // tv
    vt4 = vt.reshape(nb, nv, wd, tv)
    vec = _const_spec((1, DIFF_DQK))
    out = pl.pallas_call(
        functools.partial(_attn_prompt_kernel, tq=tq, lam_init=lam_init),
        grid=(nb, nq),
        in_specs=[pl.BlockSpec((None, tq, wd), lambda b, i: (b, i, 0)),
                  pl.BlockSpec((None, seq, wd), lambda b, i: (b, 0, 0)),
                  pl.BlockSpec((None, nv, wd, tv), lambda b, i: (b, 0, 0, 0)),
                  vec, vec, vec, vec, _const_spec((1, dv))],
        out_specs=pl.BlockSpec((None, tq, wd), lambda b, i: (b, i, 0)),
        out_shape=jax.ShapeDtypeStruct((nb, seq, wd), BF16),
        scratch_shapes=[pltpu.VMEM((2 * nh, tq, tq), F32), pltpu.VMEM((2 * nh, tq, tq), F32),
                        pltpu.VMEM((2 * nh, 1, tq), F32), pltpu.VMEM((2 * nh, dv + 2 * SUBLANES, tq), F32)],
        compiler_params=_cparams(("parallel", "arbitrary")), name="attn_prompt",
    )(qa3, ka3, vt4, lq1, lk1, lq2, lk2, sg)
    return out.reshape(nb * seq, wd)


def _attn_sample_kernel(pt_ref, q_ref, kn_ref, vn_ref, *rest, n_pages, nt, bs, lam_init):
    k_refs = rest[:bs * n_pages]
    v_refs = rest[bs * n_pages:2 * bs * n_pages]
    lq1_ref, lk1_ref, lq2_ref, lk2_ref, sg_ref, o_ref = rest[2 * bs * n_pages:]
    del pt_ref
    nh, dv, dqk = DIFF_HEADS, DIFF_DV, DIFF_DQK
    page = k_refs[0].shape[0] // nh
    nr = 2 * nt
    lam = _lambda(lq1_ref, lk1_ref, lq2_ref, lk2_ref, lam_init)
    row = lax.broadcasted_iota(jnp.int32, (nr, dv), 0)
    lane = lax.broadcasted_iota(jnp.int32, (nr, dv), 1)
    first = row < nt
    keep = jnp.logical_xor(lane >= dqk, first)
    tpos = jnp.where(first, row, row - nt)
    new_ok = (lane < nt) & (lane <= tpos)
    zpad = jnp.zeros((page - nr, dv), F32)
    lanes = [slice(h * dv, (h + 1) * dv) for h in range(nh)]
    head_rows = [pl.ds(h, page, stride=nh) for h in range(nh)]
    groups = [list(range(j, min(j + 2, n_pages))) for j in range(0, n_pages, 2)]

    chains = [(i, h) for i in range(bs) for h in range(nh)]

    def past(refs, i, h, grp):
        return jnp.concatenate([refs[i * n_pages + j][head_rows[h], :] for j in grp], axis=0).astype(BF16)

    s_all = {}
    for i, h in chains:
        qh = q_ref[i, :, lanes[h]]
        qz = jnp.where(keep, qh, jnp.zeros_like(qh))
        s_ih = [_dot_nt(qz, past(k_refs, i, h, grp)) for grp in groups]
        knew = jnp.concatenate([kn_ref[i, :, lanes[h]], zpad], axis=0)
        s_ih.append(jnp.where(new_ok, _dot_nt(qz, knew.astype(BF16)), NEG))
        s_all[i, h] = s_ih
    p_all, l_all = {}, {}
    for x in chains:
        m = jnp.max(s_all[x][0], axis=-1, keepdims=True)
        for s in s_all[x][1:]:
            m = jnp.maximum(m, jnp.max(s, axis=-1, keepdims=True))
        ps = [jnp.exp2(s - m) for s in s_all[x]]
        l = jnp.sum(ps[0], axis=-1, keepdims=True)
        for p in ps[1:]:
            l = l + jnp.sum(p, axis=-1, keepdims=True)
        p_all[x] = [p.astype(BF16) for p in ps]
        l_all[x] = l
    for i, h in chains:
        acc = _dot(p_all[i, h][-1], jnp.concatenate([vn_ref[i, :, lanes[h]], zpad], axis=0).astype(BF16))
        for gi, grp in enumerate(groups):
            acc = acc + _dot(p_all[i, h][gi], past(v_refs, i, h, grp))
        o2 = acc / l_all[i, h]
        o = o2 - lam * pltpu.roll(o2, nt, axis=0)
        o_ref[i, :, lanes[h]] = _subln(o, sg_ref, lam_init)


def _attn_sample(page_table, q2, kn8, vn8, cache_k, cache_v, lq1, lk1, lq2, lk2, sg, *, nt, bs, lam_init):
    nb, n_pages = page_table.shape
    nh, dv = DIFF_HEADS, DIFF_DV
    page = cache_k.shape[1]
    wd = nh * dv
    ck = cache_k.reshape(cache_k.shape[0], page * nh, dv)
    cv = cache_v.reshape(cache_v.shape[0], page * nh, dv)
    nr = 2 * nt
    small = pl.BlockSpec((bs, nr, wd), lambda b, pt: (b, 0, 0))

    def page_spec(i, j):
        return pl.BlockSpec((None, page * nh, dv), lambda b, pt: (pt[b * bs + i, j], 0, 0))

    pages = [page_spec(i, j) for i in range(bs) for j in range(n_pages)]
    vec = pl.BlockSpec((1, DIFF_DQK), lambda b, pt: (0, 0))
    grid_spec = pltpu.PrefetchScalarGridSpec(
        num_scalar_prefetch=1, grid=(nb // bs,),
        in_specs=[small, small, small] + pages * 2
        + [vec, vec, vec, vec, pl.BlockSpec((1, dv), lambda b, pt: (0, 0))],
        out_specs=small)
    return pl.pallas_call(
        functools.partial(_attn_sample_kernel, n_pages=n_pages, nt=nt, bs=bs, lam_init=lam_init),
        grid_spec=grid_spec, out_shape=jax.ShapeDtypeStruct((nb, nr, wd), F32),
        compiler_params=_cparams(("parallel",)), name="attn_sample",
    )(page_table, q2, kn8, vn8, *([ck] * (bs * n_pages)), *([cv] * (bs * n_pages)), lq1, lk1, lq2, lk2, sg)


def _post_kernel(x_ref, og_ref, od_ref, hist_ref, wo_ref, gf_ref, wu_ref, fcw_ref, fcb_ref, wd_ref,
                 y_ref, tail_ref, ubuf, *, tm, shift, hr, dff, ncol):
    i = pl.program_id(1)
    gw = og_ref.shape[1]

    @pl.when(i == 0)
    def _():
        ubuf[0:hr, :] = hist_ref[...]

    hres = x_ref[...] + _dot(og_ref[...], wo_ref[0:gw, :]) + _dot(od_ref[...], wo_ref[gw:, :])
    hn = (hres * lax.rsqrt(jnp.mean(hres * hres, axis=-1, keepdims=True) + EPS) * gf_ref[...]).astype(BF16)
    fcw = fcw_ref[...]
    fcb = fcb_ref[...]

    def conv(cols):
        ubuf[hr:hr + tm, cols] = _dot(hn, wu_ref[:, cols])
        u = fcw[FFN_CONV - 1:FFN_CONV, cols] * ubuf[hr:hr + tm, cols] + fcb[:, cols]
        for j in range(1, FFN_CONV):
            u = u + fcw[FFN_CONV - 1 - j:FFN_CONV - j, cols] * ubuf[hr - j * shift:hr - j * shift + tm, cols]
        return u

    y = hres
    wc = dff // ncol
    for c in range(ncol):
        gate = conv(slice(c * wc, (c + 1) * wc))
        val = conv(slice(dff + c * wc, dff + (c + 1) * wc))
        act = (_silu(gate) * val).astype(BF16)
        y = y + _dot(act, wd_ref[c * wc:(c + 1) * wc, :])
    y_ref[...] = y
    tail = ubuf[tm:tm + hr, :]
    tail_ref[...] = tail
    ubuf[0:hr, :] = tail


def _post(x2d, og, od, hist, w_out, g_ffn, w_up, fcw, fcb, w_down, *, nseq, tm, shift, hr, ncol):
    n, d = x2d.shape
    gw = og.shape[1]
    dff2 = w_up.shape[1]
    nt = n // (nseq * tm)
    row = lambda b, i: (b * nt + i, 0)
    in_specs = [
        pl.BlockSpec((tm, d), row), pl.BlockSpec((tm, gw), row), pl.BlockSpec((tm, od.shape[1]), row),
        pl.BlockSpec((None, hr, dff2), lambda b, i: (b, 0, 0)),
        _const_spec(w_out.shape), _const_spec((1, d)), _const_spec(w_up.shape),
        _const_spec((FFN_CONV, dff2)), _const_spec((1, dff2)), _const_spec(w_down.shape),
    ]
    out_shape = (jax.ShapeDtypeStruct((n, d), F32), jax.ShapeDtypeStruct((nseq, hr, dff2), F32))
    out_specs = (pl.BlockSpec((tm, d), row), pl.BlockSpec((None, hr, dff2), lambda b, i: (b, 0, 0)))
    return pl.pallas_call(
        functools.partial(_post_kernel, tm=tm, shift=shift, hr=hr, dff=dff2 // 2, ncol=ncol),
        grid=(nseq, nt), in_specs=in_specs, out_specs=out_specs, out_shape=out_shape,
        scratch_shapes=[pltpu.VMEM((hr + tm, dff2), F32)],
        compiler_params=_cparams(("parallel", "arbitrary")), name="post",
    )(x2d, og, od, hist, w_out, g_ffn, w_up, fcw, fcb, w_down)


def _lane_pad(vec, offset):
    out = jnp.zeros((1, LANES), F32)
    return lax.dynamic_update_slice(out, vec.reshape(1, -1).astype(F32), (0, offset))


def _layer(l, x_prompt, x_sample, state_gdn_conv, state_gdn_s, cache_k, cache_v, page_table, state_ffn_conv, wl):
    (attn_norm_g, w_in, gdn_conv_w, gdn_a_log, gdn_dt_bias, gdn_out_norm_g, diff_q_norm_g, diff_k_norm_g,
     lq1, lk1, lq2, lk2, diff_subln_g, w_out, ffn_norm_g, w_up, ffn_conv_w, ffn_conv_b, w_down) = wl
    nbp, seq, d = x_prompt.shape
    nbs, nts, _ = x_sample.shape
    nh, dk = GDN_HEADS, GDN_DK
    gw = nh * dk
    dw = DIFF_HEADS * DIFF_DV
    lam_init = 0.8 - 0.6 * math.exp(-0.3 * l)

    c_b = 4 * gw
    c_d = c_b + 2 * nh
    w_g = w_in[:, :c_b].astype(BF16)
    w_d = w_in[:, c_d:].astype(BF16)
    w_t = jnp.transpose(jnp.concatenate(
        [w_in[:, c_b:c_d], jnp.zeros((d, LANES - 2 * nh), w_in.dtype)], axis=1)).astype(BF16)
    reps = dw // DIFF_DQK
    gq_t = jnp.tile(diff_q_norm_g.reshape(1, -1), (1, reps))
    gk_t = jnp.tile(diff_k_norm_g.reshape(1, -1), (1, reps))
    g_attn = attn_norm_g.reshape(1, d)
    alr, dtr = _lane_pad(gdn_a_log, nh), _lane_pad(gdn_dt_bias, nh)
    alc = jnp.transpose(alr[:, :SUBLANES])
    dtc = jnp.transpose(dtr[:, :SUBLANES])
    gn = gdn_out_norm_g.reshape(1, dk)
    vecs = [v.reshape(1, -1) for v in (lq1, lk1, lq2, lk2)]
    sg = diff_subln_g.reshape(1, -1)
    w_out_b, w_up_b, w_down_b = w_out.astype(BF16), w_up.astype(BF16), w_down.astype(BF16)
    g_ffn = ffn_norm_g.reshape(1, d)
    fcb = ffn_conv_b.reshape(1, -1)
    dff2 = w_up.shape[1]

    xp2 = x_prompt.reshape(nbp * seq, d)
    (z, gcol, grow, qn, kn, kb, vb, tail_g, knew, vnew, qa, ka, vt) = _inproj_gdn(
        xp2, g_attn, w_g, w_d, w_t, gq_t, gk_t, gdn_conv_w, alr, dtr, alc, dtc,
        nb=nbp, tm=min(_Tiles.inproj_rows, seq), gw=gw, dw=dw)
    og, s_prompt = _gdn_prompt(qn, kn, kb, vb, gcol, grow, z, gn, nb=nbp, seq=seq,
                               tl=min(_Tiles.gdn_tokens, seq))
    od = _attn_prompt(qa, ka, vt, *vecs, sg, nb=nbp, seq=seq, tq=min(_Tiles.attn_tile, seq), lam_init=lam_init)
    hr_p = SUBLANES
    y_p, tail_p = _post(xp2, og, od, jnp.zeros((nbp, hr_p, dff2), F32), w_out_b, g_ffn, w_up_b,
                        ffn_conv_w, fcb, w_down_b, nseq=nbp, tm=min(_Tiles.post_rows, seq), shift=1, hr=hr_p,
                        ncol=_Tiles.post_col_groups)
    out_p = (y_p.reshape(nbp, seq, d),
             tail_g[:, SUBLANES - (GDN_CONV - 1):, :],
             s_prompt,
             knew.reshape(nbp, seq, DIFF_HEADS, DIFF_DV),
             vnew.reshape(nbp, seq, DIFF_HEADS, DIFF_DV),
             tail_p[:, hr_p - (FFN_CONV - 1):, :])

    xs2 = jnp.transpose(x_sample, (1, 0, 2)).reshape(nts * nbs, d)
    tm_s = min(_Tiles.sample_rows, nts * nbs)
    assert tm_s % nbs == 0 and nbs % _Tiles.paged_seqs == 0
    qkv, z, ba, knew, vnew, qa, _, _ = _inproj(xs2, g_attn, w_g, w_d, w_t, gq_t, gk_t, tm=tm_s, gw=gw, dw=dw)
    qkv_tm = qkv.reshape(nts, nbs, 3 * gw)
    hist_tm = jnp.transpose(state_gdn_conv, (1, 0, 2))
    og_tm, s_sample = _gdn_sample(qkv_tm, hist_tm, ba.reshape(nts, nbs, LANES), z.reshape(nts, nbs, gw),
                                  state_gdn_s, gdn_conv_w, alr, dtr, gn, bb=min(_Tiles.state_seqs, nbs))
    conv_all = jnp.concatenate([hist_tm, qkv_tm], axis=0)
    conv_s = jnp.transpose(conv_all[nts:], (1, 0, 2))
    to_bm = lambda a: jnp.transpose(a.reshape(nts, nbs, -1), (1, 0, 2))
    q_bm = to_bm(qa)
    q2 = jnp.concatenate([q_bm, q_bm], axis=1)
    pad = jnp.zeros((nbs, nts, dw), F32)
    kn_bm, vn_bm = to_bm(knew), to_bm(vnew)
    od_bm = _attn_sample(page_table, q2, jnp.concatenate([kn_bm, pad], axis=1),
                         jnp.concatenate([vn_bm, pad], axis=1), cache_k, cache_v, *vecs, sg,
                         nt=nts, bs=_Tiles.paged_seqs, lam_init=lam_init)
    od_tm = jnp.transpose(od_bm[:, :nts, :], (1, 0, 2)).reshape(nts * nbs, dw).astype(BF16)
    hr_s = (FFN_CONV - 1) * nbs
    hist_f = jnp.transpose(state_ffn_conv, (1, 0, 2)).reshape(1, hr_s, dff2)
    y_s, tail_s = _post(xs2, og_tm.reshape(nts * nbs, gw), od_tm, hist_f, w_out_b, g_ffn, w_up_b,
                        ffn_conv_w, fcb, w_down_b, nseq=1, tm=tm_s, shift=nbs, hr=hr_s,
                        ncol=_Tiles.post_col_groups)
    out_s = (jnp.transpose(y_s.reshape(nts, nbs, d), (1, 0, 2)),
             conv_s,
             s_sample,
             kn_bm.reshape(nbs, nts, DIFF_HEADS, DIFF_DV),
             vn_bm.reshape(nbs, nts, DIFF_HEADS, DIFF_DV),
             jnp.transpose(tail_s.reshape(FFN_CONV - 1, nbs, dff2), (1, 0, 2)))
    return out_p, out_s


def kernel(x_prompt, x_sample, state_gdn_conv, state_gdn_S, cache_k, cache_v, page_table, state_ffn_conv, attn_norm_g, w_in, gdn_conv_w, gdn_A_log, gdn_dt_bias, gdn_out_norm_g, diff_q_norm_g, diff_k_norm_g, diff_lambda_q1, diff_lambda_k1, diff_lambda_q2, diff_lambda_k2, diff_subln_g, w_out, ffn_norm_g, w_up, ffn_conv_w, ffn_conv_b, w_down):
    depth = w_in.shape[0]
    hp, hs = x_prompt, x_sample
    outs_p, outs_s = [], []
    for l in range(depth):
        wl = (attn_norm_g[l], w_in[l], gdn_conv_w[l], gdn_A_log[l], gdn_dt_bias[l], gdn_out_norm_g[l],
              diff_q_norm_g[l], diff_k_norm_g[l], diff_lambda_q1[l], diff_lambda_k1[l], diff_lambda_q2[l],
              diff_lambda_k2[l], diff_subln_g[l], w_out[l], ffn_norm_g[l], w_up[l], ffn_conv_w[l],
              ffn_conv_b[l], w_down[l])
        out_p, out_s = _layer(l, hp, hs, state_gdn_conv[l], state_gdn_S[l], cache_k[l], cache_v[l],
                              page_table, state_ffn_conv[l], wl)
        hp, hs = out_p[0], out_s[0]
        outs_p.append(out_p[1:])
        outs_s.append(out_s[1:])
    stack = lambda outs, i: jnp.stack([o[i] for o in outs])
    return (hp, hs) + tuple(stack(outs_p, i) for i in range(5)) + tuple(stack(outs_s, i) for i in range(5))
```

```python
import functools
import math

import jax
import jax.numpy as jnp
from jax import lax
from jax.experimental import pallas as pl
from jax.experimental.pallas import tpu as pltpu

F32 = jnp.float32
BF16 = jnp.bfloat16
EPS = 1e-6
NEG = -1e30

GDN_HEADS = 4
GDN_DK = 128
GDN_CONV = 4
GDN_CHUNK = 64
DIFF_HEADS = 4
DIFF_DV = 128
DIFF_DQK = 64
FFN_CONV = 3
LANES = 128
SUBLANES = 8
V7X_VMEM_BYTES = 64 * 1024 * 1024
VMEM_LIMIT = V7X_VMEM_BYTES - 8 * 1024 * 1024


class _Tiles:
    inproj_rows = 512
    gdn_tokens = 256
    attn_tile = 512
    post_rows = 512
    post_col_groups = 1
    sample_rows = 256
    state_seqs = 16
    paged_seqs = 2


def _cparams(sem):
    return pltpu.CompilerParams(dimension_semantics=sem, vmem_limit_bytes=VMEM_LIMIT)


def _const_spec(shape):
    nd = len(shape)
    return pl.BlockSpec(shape, lambda *_: (0,) * nd, pipeline_mode=pl.Buffered(1))


def _dot(a, b):
    return jnp.dot(a, b, preferred_element_type=F32)


def _dot_nt(a, b):
    return lax.dot_general(a, b, (((1,), (1,)), ((), ())), preferred_element_type=F32)


def _dot_tn(a, b):
    return lax.dot_general(a, b, (((0,), (0,)), ((), ())), preferred_element_type=F32)


def _softplus(x):
    return jnp.maximum(x, 0.0) + jnp.log1p(jnp.exp(-jnp.abs(x)))


def _silu(x):
    return x * jax.nn.sigmoid(x)


def _split3(x):
    hi = x.astype(BF16)
    r = x - hi.astype(F32)
    mid = r.astype(BF16)
    lo = (r - mid.astype(F32)).astype(BF16)
    return hi, mid, lo


def _inproj_core(x_ref, g_ref, wg_ref, wd_ref, wt_ref, gq_ref, gk_ref,
                 knew_ref, vnew_ref, qa_ref, ka_ref, vt_ref, gw, dw):
    tm = x_ref.shape[0]
    x = x_ref[...]
    xn = x * lax.rsqrt(jnp.mean(x * x, axis=-1, keepdims=True) + EPS) * g_ref[...]
    xb = xn.astype(BF16)
    proj = _dot(xb, wg_ref[...])
    pd = _dot(xb, wd_ref[...])
    tr = _dot_nt(wt_ref[...], xb)
    dq = pd[:, :dw]
    dk = pd[:, dw:2 * dw]
    dv = pd[:, 2 * dw:]
    vt_ref[...] = jnp.transpose(dv).astype(BF16)
    low =lax.broadcasted_iota(jnp.int32, (tm, DIFF_DV), 1) < DIFF_DQK

    def half_rms(t):
        out = []
        for h in range(DIFF_HEADS):
            th = t[:, h * DIFF_DV:(h + 1) * DIFF_DV]
            sq = th * th
            s1 = jnp.sum(jnp.where(low, sq, 0.0), axis=-1, keepdims=True)
            s2 = jnp.sum(jnp.where(low, 0.0, sq), axis=-1, keepdims=True)
            r1 = lax.rsqrt(s1 * (1.0 / DIFF_DQK) + EPS)
            r2 = lax.rsqrt(s2 * (1.0 / DIFF_DQK) + EPS)
            out.append(th * jnp.where(low, r1, r2))
        return jnp.concatenate(out, axis=1)

    dqn = half_rms(dq) * gq_ref[...]
    dkn = half_rms(dk) * gk_ref[...]
    for h in range(DIFF_HEADS):
        head_rows = pl.ds(h, tm, stride=DIFF_HEADS)
        knew_ref[head_rows, :] = dkn[:, h * DIFF_DV:(h + 1) * DIFF_DV]
        vnew_ref[head_rows, :] = dv[:, h * DIFF_DV:(h + 1) * DIFF_DV]
    qa_ref[...] = (dqn * (DIFF_DQK ** -0.5 * math.log2(math.e))).astype(BF16)
    ka_ref[...] = dkn.astype(BF16)
    return proj, tr


def _inproj_kernel(x_ref, g_ref, wg_ref, wd_ref, wt_ref, gq_ref, gk_ref,
                   qkv_ref, z_ref, ba_ref, knew_ref, vnew_ref, qa_ref, ka_ref, vt_ref, *, gw, dw):
    proj, trow = _inproj_core(x_ref, g_ref, wg_ref, wd_ref, wt_ref, gq_ref, gk_ref,
                              knew_ref, vnew_ref, qa_ref, ka_ref, vt_ref, gw, dw)
    qkv_ref[...] = proj[:, :3 * gw]
    z_ref[...] = proj[:, 3 * gw:]
    ba_ref[...] = jnp.transpose(trow)


def _inproj_gdn_kernel(x_ref, g_ref, wg_ref, wd_ref, wt_ref, gq_ref, gk_ref,
                       cw_ref, alr_ref, dtr_ref, alc_ref, dtc_ref,
                       z_ref, gcol_ref, grow_ref, qn_ref, kn_ref, kb_ref, vb_ref, tail_ref,
                       knew_ref, vnew_ref, qa_ref, ka_ref, vt_ref, xbuf, *, gw, dw):
    i = pl.program_id(1)
    tm = x_ref.shape[0]
    nh, dk = GDN_HEADS, GDN_DK
    hr = SUBLANES

    @pl.when(i == 0)
    def _():
        xbuf[0:hr, :] = jnp.zeros((hr, 3 * gw), F32)

    proj, trow = _inproj_core(x_ref, g_ref, wg_ref, wd_ref, wt_ref, gq_ref, gk_ref,
                              knew_ref, vnew_ref, qa_ref, ka_ref, vt_ref, gw, dw)
    z_ref[...] = proj[:, 3 * gw:]
    xq = proj[:, :3 * gw]
    prev = xbuf[...]
    tail = xq[tm - hr:, :]
    tail_ref[...] = tail
    xbuf[...] = tail
    cw = cw_ref[...]
    first8 = lax.broadcasted_iota(jnp.int32, (hr, 3 * gw), 0)
    y = cw[GDN_CONV - 1:GDN_CONV, :] * xq
    for j in range(1, GDN_CONV):
        sh = pltpu.roll(xq, j, axis=0)
        head = jnp.where(first8 < j, pltpu.roll(prev, j, axis=0), sh[:hr])
        y = y + cw[GDN_CONV - 1 - j:GDN_CONV - j, :] * jnp.concatenate([head, sh[hr:]], axis=0)
    y = _silu(y)
    ba = jnp.transpose(trow)
    beta_c = jax.nn.sigmoid(ba)
    gcol_ref[...] = -jnp.exp(alr_ref[...]) * _softplus(ba + dtr_ref[...])
    grow_ref[...] = -jnp.exp(alc_ref[...]) * _softplus(trow[:SUBLANES] + dtc_ref[...])
    for h in range(nh):
        ls = slice(h * dk, (h + 1) * dk)
        q = y[:, h * dk:(h + 1) * dk]
        k = y[:, gw + h * dk:gw + (h + 1) * dk]
        v = y[:, 2 * gw + h * dk:2 * gw + (h + 1) * dk]
        qn = q * lax.rsqrt(jnp.sum(q * q, axis=-1, keepdims=True) + EPS) * (dk ** -0.5)
        kn = k * lax.rsqrt(jnp.sum(k * k, axis=-1, keepdims=True) + EPS)
        beta = jnp.broadcast_to(beta_c[:, h:h + 1], (tm, dk))
        qn_ref[:, ls] = qn.astype(BF16)
        kn_ref[:, ls] = kn.astype(BF16)
        kb_ref[:, ls] = (kn * beta).astype(BF16)
        vb_ref[:, ls] = (v * beta).astype(BF16)


def _attn_out(n, tm, dw):
    return (
        jax.ShapeDtypeStruct((n * DIFF_HEADS, DIFF_DV), F32),
        jax.ShapeDtypeStruct((n * DIFF_HEADS, DIFF_DV), F32),
        jax.ShapeDtypeStruct((n, dw), BF16),
        jax.ShapeDtypeStruct((n, dw), BF16),
        jax.ShapeDtypeStruct((n // tm, dw, tm), BF16),
    )


def _inproj(x2d, g_attn, w_g, w_d, w_t, gq_t, gk_t, *, tm, gw, dw):
    n, d = x2d.shape
    row = lambda i: (i, 0)
    out_shape = (
        jax.ShapeDtypeStruct((n, 3 * gw), F32),
        jax.ShapeDtypeStruct((n, gw), F32),
        jax.ShapeDtypeStruct((n, LANES), F32),
    ) + _attn_out(n, tm, dw)
    out_specs = (
        pl.BlockSpec((tm, 3 * gw), row), pl.BlockSpec((tm, gw), row), pl.BlockSpec((tm, LANES), row),
        pl.BlockSpec((tm * DIFF_HEADS, DIFF_DV), row), pl.BlockSpec((tm * DIFF_HEADS, DIFF_DV), row),
        pl.BlockSpec((tm, dw), row), pl.BlockSpec((tm, dw), row),
        pl.BlockSpec((None, dw, tm), lambda i: (i, 0, 0)),
    )
    in_specs = [
        pl.BlockSpec((tm, d), row), _const_spec((1, d)), _const_spec(w_g.shape), _const_spec(w_d.shape),
        _const_spec(w_t.shape), _const_spec((1, dw)), _const_spec((1, dw)),
    ]
    return pl.pallas_call(
        functools.partial(_inproj_kernel, gw=gw, dw=dw),
        grid=(n // tm,), in_specs=in_specs, out_specs=out_specs, out_shape=out_shape,
        compiler_params=_cparams(("parallel",)), name="inproj",
    )(x2d, g_attn, w_g, w_d, w_t, gq_t, gk_t)


def _inproj_gdn(x2d, g_attn, w_g, w_d, w_t, gq_t, gk_t, cw, alr, dtr, alc, dtc, *, nb, tm, gw, dw):
    n, d = x2d.shape
    nt = n // (nb * tm)
    row = lambda b, i: (b * nt + i, 0)
    out_shape = (
        jax.ShapeDtypeStruct((n, gw), F32),
        jax.ShapeDtypeStruct((n, LANES), F32),
        jax.ShapeDtypeStruct((nb, SUBLANES, n // nb), F32),
        jax.ShapeDtypeStruct((n, gw), BF16),
        jax.ShapeDtypeStruct((n, gw), BF16),
        jax.ShapeDtypeStruct((n, gw), BF16),
        jax.ShapeDtypeStruct((n, gw), BF16),
        jax.ShapeDtypeStruct((nb, SUBLANES, 3 * gw), F32),
    ) + _attn_out(n, tm, dw)
    out_specs = (
        pl.BlockSpec((tm, gw), row), pl.BlockSpec((tm, LANES), row),
        pl.BlockSpec((None, SUBLANES, tm), lambda b, i: (b, 0, i)),
        pl.BlockSpec((tm, gw), row), pl.BlockSpec((tm, gw), row), pl.BlockSpec((tm, gw), row),
        pl.BlockSpec((tm, gw), row),
        pl.BlockSpec((None, SUBLANES, 3 * gw), lambda b, i: (b, 0, 0)),
        pl.BlockSpec((tm * DIFF_HEADS, DIFF_DV), row), pl.BlockSpec((tm * DIFF_HEADS, DIFF_DV), row),
        pl.BlockSpec((tm, dw), row), pl.BlockSpec((tm, dw), row),
        pl.BlockSpec((None, dw, tm), lambda b, i: (b * nt + i, 0, 0)),
    )
    in_specs = [
        pl.BlockSpec((tm, d), row), _const_spec((1, d)), _const_spec(w_g.shape), _const_spec(w_d.shape),
        _const_spec(w_t.shape), _const_spec((1, dw)), _const_spec((1, dw)),
        _const_spec(cw.shape), _const_spec((1, LANES)), _const_spec((1, LANES)),
        _const_spec((SUBLANES, 1)), _const_spec((SUBLANES, 1)),
    ]
    return pl.pallas_call(
        functools.partial(_inproj_gdn_kernel, gw=gw, dw=dw),
        grid=(nb, nt), in_specs=in_specs, out_specs=out_specs, out_shape=out_shape,
        scratch_shapes=[pltpu.VMEM((SUBLANES, 3 * gw), F32)],
        compiler_params=_cparams(("parallel", "arbitrary")), name="inproj_gdn",
    )(x2d, g_attn, w_g, w_d, w_t, gq_t, gk_t, cw, alr, dtr, alc, dtc)


def _gdn_prompt_kernel(qn_ref, kn_ref, kb_ref, vb_ref, gcol_ref, grow_ref, z_ref, gn_ref,
                       og_ref, sout_ref, s_scr, *, tl):
    i = pl.program_id(0)
    nb = qn_ref.shape[0]
    nh, dk, c = GDN_HEADS, GDN_DK, GDN_CHUNK
    gw = nh * dk

    @pl.when(i == 0)
    def _():
        s_scr[...] = jnp.zeros_like(s_scr)

    g_rep = jnp.concatenate([jnp.broadcast_to(gcol_ref[b][:, nh + h:nh + h + 1], (tl, dk))
                             for b in range(nb) for h in range(nh)], axis=1)
    g_r = jnp.concatenate([grow_ref[b] for b in range(nb)], axis=0)

    ii = lax.broadcasted_iota(jnp.int32, (tl, tl), 0)
    jj = lax.broadcasted_iota(jnp.int32, (tl, tl), 1)
    same = lax.shift_right_logical(ii, 6) == lax.shift_right_logical(jj, 6)
    tri = jnp.where(same & (ii >= jj), 1.0, 0.0).astype(BF16)
    triu = jnp.where(same & (ii <= jj), 1.0, 0.0).astype(BF16)
    gc_rep = sum(_dot(tri, p) for p in _split3(g_rep))
    gc_r = sum(_dot(p, triu) for p in _split3(g_r))

    pr = 2 * c
    pi = lax.broadcasted_iota(jnp.int32, (pr, pr), 0)
    pj = lax.broadcasted_iota(jnp.int32, (pr, pr), 1)
    psame = lax.shift_right_logical(pi, 6) == lax.shift_right_logical(pj, 6)
    lower = psame & (pi >= pj)
    strict = psame & (pi > pj)
    eye = jnp.where(pi == pj, 1.0, 0.0).astype(F32)
    zeros_c = jnp.zeros((c, dk), F32)

    npair = tl // pr
    seqs = [(b, h) for b in range(nb) for h in range(nh)]
    hp = [(b, h, p) for b, h in seqs for p in range(npair)]

    def tile(ref, b, h, p):
        return ref[b, p * pr:(p + 1) * pr, h * dk:(h + 1) * dk]

    gcp = {(b, h, p): gc_rep[p * pr:(p + 1) * pr, b * gw + h * dk:b * gw + (h + 1) * dk] for b, h, p in hp}
    dec = {(b, h, p): jnp.exp(jnp.where(
        lower, gcp[b, h, p] - gc_r[b * SUBLANES + nh + h:b * SUBLANES + nh + h + 1, p * pr:(p + 1) * pr], NEG))
        for b, h, p in hp}
    kpb = {x: tile(kn_ref, *x) for x in hp}
    kbp = {x: tile(kb_ref, *x) for x in hp}
    m = {x: -jnp.where(strict, _dot_nt(kbp[x], kpb[x]) * dec[x], 0.0) for x in hp}
    a = {x: eye + m[x] for x in hp}
    pw = {x: _dot(m[x].astype(BF16), m[x].astype(BF16)) for x in hp}
    for it in range(1, 6):
        for x in hp:
            pwb = pw[x].astype(BF16)
            if it < 5:
                res = _dot(pwb, jnp.concatenate([a[x].astype(BF16), pwb], axis=1))
                a[x] = a[x] + res[:, :pr]
                pw[x] = res[:, pr:]
            else:
                a[x] = a[x] + _dot(pwb, a[x].astype(BF16))
    egc = {x: jnp.exp(gcp[x]) for x in hp}
    sol = {x: _dot(a[x].astype(BF16),
                   jnp.concatenate([tile(vb_ref, *x), (kbp[x].astype(F32) * egc[x]).astype(BF16)], axis=1))
           for x in hp}
    qk = {x: (_dot_nt(tile(qn_ref, *x), kpb[x]) * dec[x]).astype(BF16) for x in hp}
    qg = {x: tile(qn_ref, *x).astype(F32) * egc[x] for x in hp}

    s_cur = {x: s_scr[x] for x in seqs}
    o_rows = {x: [] for x in seqs}
    for p in range(npair):
        for cc in range(2):
            c0 = cc * c
            ws = {}
            for b, h in seqs:
                wq = jnp.concatenate([sol[b, h, p][c0:c0 + c, dk:], qg[b, h, p][c0:c0 + c]], axis=0)
                ws[b, h] = _dot(wq.astype(BF16), s_cur[b, h].astype(BF16))
            for b, h in seqs:
                g_c0 = gcp[b, h, p][c0:c0 + c]
                glast = g_c0[c - 1:c, :]
                kd = kpb[b, h, p][c0:c0 + c].astype(F32) * jnp.exp(glast - g_c0)
                v_new = (sol[b, h, p][c0:c0 + c, :dk] - ws[b, h][:c]).astype(BF16)
                s_cur[b, h] = s_cur[b, h] * jnp.exp(glast) + _dot_tn(kd.astype(BF16), v_new)
                zc = jnp.zeros_like(v_new)
                vn_pad = jnp.concatenate([v_new, zc] if cc == 0 else [zc, v_new], axis=0)
                o_rows[b, h].append(ws[b, h][c:] + _dot(qk[b, h, p][c0:c0 + c], vn_pad))
    for b, h in seqs:
        s_scr[b, h] = s_cur[b, h]
        o = jnp.concatenate(o_rows[b, h], axis=0)
        o = o * lax.rsqrt(jnp.mean(o * o, axis=-1, keepdims=True) + EPS) * gn_ref[...]
        og_ref[b, :, h * dk:(h + 1) * dk] = (o * _silu(z_ref[b, :, h * dk:(h + 1) * dk])).astype(og_ref.dtype)

    @pl.when(i == pl.num_programs(0) - 1)
    def _():
        sout_ref[...] = s_scr[...]


def _gdn_prompt(qn, kn, kb, vb, gcol, grow, z, gn, *, nb, seq, tl):
    nh, dk = GDN_HEADS, GDN_DK
    gw = nh * dk
    nt = seq // tl
    tok = lambda i: (0, i, 0)
    wide = pl.BlockSpec((nb, tl, gw), tok)
    per_seq = lambda a: a.reshape(nb, seq, a.shape[-1])
    in_specs = [
        wide, wide, wide, wide, pl.BlockSpec((nb, tl, LANES), tok),
        pl.BlockSpec((nb, SUBLANES, tl), lambda i: (0, 0, i)),
        wide, _const_spec((1, dk)),
    ]
    out_shape = (jax.ShapeDtypeStruct((nb, seq, gw), BF16),
                 jax.ShapeDtypeStruct((nb, nh, dk, dk), F32))
    out_specs = (wide, pl.BlockSpec((nb, nh, dk, dk), lambda i: (0, 0, 0, 0)))
    og, s_out = pl.pallas_call(
        functools.partial(_gdn_prompt_kernel, tl=tl),
        grid=(nt,), in_specs=in_specs, out_specs=out_specs, out_shape=out_shape,
        scratch_shapes=[pltpu.VMEM((nb, nh, dk, dk), F32)],
        compiler_params=_cparams(("arbitrary",)), name="gdn_prompt",
    )(per_seq(qn), per_seq(kn), per_seq(kb), per_seq(vb), per_seq(gcol), grow, per_seq(z), gn)
    return og.reshape(nb * seq, gw), s_out


def _gdn_sample_pre_kernel(qkv_ref, hist_ref, ba_ref, cw_ref, alr_ref, dtr_ref, r_ref, intra_ref, *, nt):
    nh, dk = GDN_HEADS, GDN_DK
    gw = nh * dk
    nb = qkv_ref.shape[1]
    cw = cw_ref[...]
    nhist = GDN_CONV - 1
    xp = [hist_ref[j] for j in range(nhist)] + [qkv_ref[t] for t in range(nt)]
    ys = []
    for t in range(nt):
        y = cw[0:1, :] * xp[t]
        for j in range(1, GDN_CONV):
            y = y + cw[j:j + 1, :] * xp[t + j]
        ys.append(_silu(y))
    beta, gc = [], []
    for t in range(nt):
        ba = ba_ref[t]
        beta_c = jax.nn.sigmoid(ba)
        g_c = -jnp.exp(alr_ref[...]) * _softplus(ba + dtr_ref[...])
        beta.append([jnp.broadcast_to(beta_c[:, h:h + 1], (nb, dk)) for h in range(nh)])
        g_t = [jnp.broadcast_to(g_c[:, nh + h:nh + h + 1], (nb, dk)) for h in range(nh)]
        gc.append(g_t if t == 0 else [gc[t - 1][h] + g_t[h] for h in range(nh)])
    zrow = jnp.zeros((nb, dk), F32)
    for h in range(nh):
        q, k, v = [], [], []
        for t in range(nt):
            qt = ys[t][:, h * dk:(h + 1) * dk]
            kt = ys[t][:, gw + h * dk:gw + (h + 1) * dk]
            q.append(qt * lax.rsqrt(jnp.sum(qt * qt, axis=-1, keepdims=True) + EPS) * (dk ** -0.5))
            k.append(kt * lax.rsqrt(jnp.sum(kt * kt, axis=-1, keepdims=True) + EPS))
            v.append(ys[t][:, 2 * gw + h * dk:2 * gw + (h + 1) * dk])
        g = [gc[t][h] for t in range(nt)]
        us, ws = [], []
        for t in range(nt):
            b_t = beta[t][h]
            u_t = v[t] * b_t
            w_t = k[t] * b_t * jnp.exp(g[t])
            for s in range(t):
                m_ts = b_t * jnp.sum(k[t] * k[s], axis=-1, keepdims=True) * jnp.exp(g[t] - g[s])
                u_t = u_t - m_ts * us[s]
                w_t = w_t - m_ts * ws[s]
            us.append(u_t)
            ws.append(w_t)
        lane = slice(h * dk, (h + 1) * dk)
        for t in range(nt):
            r_ref[t, :, lane] = ws[t]
            r_ref[nt + t, :, lane] = q[t] * jnp.exp(g[t])
            r_ref[2 * nt + t, :, lane] = us[t]
            r_ref[3 * nt + t, :, lane] = k[t] * jnp.exp(g[nt - 1] - g[t])
            for s in range(nt):
                if s <= t:
                    intra_ref[t * nt + s, :, lane] = (jnp.sum(q[t] * k[s], axis=-1, keepdims=True)
                                                       * jnp.exp(g[t] - g[s]))
                else:
                    intra_ref[t * nt + s, :, lane] = zrow
        r_ref[4 * nt, :, lane] = jnp.exp(g[nt - 1])
        for r in range(4 * nt + 1, r_ref.shape[0]):
            r_ref[r, :, lane] = zrow


def _gdn_sample_state_kernel(r_ref, s_ref, ws_ref, sout_ref, *, nt, bb):
    nh = GDN_HEADS
    rows = lax.broadcasted_iota(jnp.int32, (2 * nt, GDN_DK), 0)
    inst = [(bi, h) for bi in range(bb) for h in range(nh)]
    ws = {x: _dot(r_ref[x[0], x[1], 0:2 * nt, :].astype(BF16), s_ref[x].astype(BF16)) for x in inst}
    for x in inst:
        ud = r_ref[x[0], x[1], 2 * nt:4 * nt, :]
        vn = jnp.where(rows < nt, ud - ws[x], 0.0)
        kd = jnp.where(rows < nt, pltpu.roll(ud, nt, axis=0), 0.0)
        ws_ref[x] = ws[x]
        sout_ref[x] = (s_ref[x] * r_ref[x[0], x[1], 4 * nt:4 * nt + 1, :]
                       + _dot_tn(kd.astype(BF16), vn.astype(BF16)))


def _gdn_sample_post_kernel(ws_ref, r_ref, intra_ref, z_ref, gn_ref, og_ref, *, nt):
    nh, dk = GDN_HEADS, GDN_DK
    for h in range(nh):
        lane = slice(h * dk, (h + 1) * dk)
        vn = [r_ref[2 * nt + t, :, lane] - ws_ref[t, :, lane] for t in range(nt)]
        for t in range(nt):
            o = ws_ref[nt + t, :, lane]
            for s in range(t + 1):
                o = o + intra_ref[t * nt + s, :, lane] * vn[s]
            o = o * lax.rsqrt(jnp.mean(o * o, axis=-1, keepdims=True) + EPS) * gn_ref[...]
            og_ref[t, :, lane] = (o * _silu(z_ref[t, :, lane])).astype(og_ref.dtype)


def _gdn_sample(qkv_tm, hist_tm, ba_tm, z_tm, state_s, cw, alr, dtr, gn, *, bb):
    nt, nb, _ = qkv_tm.shape
    nh, dk = GDN_HEADS, GDN_DK
    gw = nh * dk
    nr = 6 * nt
    full = lambda shape: pl.BlockSpec(shape, lambda *_: (0,) * len(shape))
    r_tm, intra = pl.pallas_call(
        functools.partial(_gdn_sample_pre_kernel, nt=nt),
        grid=(1,),
        in_specs=[full(qkv_tm.shape), full(hist_tm.shape), full(ba_tm.shape), full(cw.shape),
                  full(alr.shape), full(dtr.shape)],
        out_specs=(full((nr, nb, gw)), full((nt * nt, nb, gw))),
        out_shape=(jax.ShapeDtypeStruct((nr, nb, gw), F32), jax.ShapeDtypeStruct((nt * nt, nb, gw), F32)),
        compiler_params=_cparams(("arbitrary",)), name="gdn_sample_pre",
    )(qkv_tm, hist_tm, ba_tm, cw, alr, dtr)
    r_bm = jnp.transpose(r_tm.reshape(nr, nb, nh, dk), (1, 2, 0, 3))
    blk = lambda b: (b, 0, 0, 0)
    ws_bm, s_new = pl.pallas_call(
        functools.partial(_gdn_sample_state_kernel, nt=nt, bb=bb),
        grid=(nb // bb,),
        in_specs=[pl.BlockSpec((bb, nh, nr, dk), blk), pl.BlockSpec((bb, nh, dk, dk), blk)],
        out_specs=(pl.BlockSpec((bb, nh, 2 * nt, dk), blk), pl.BlockSpec((bb, nh, dk, dk), blk)),
        out_shape=(jax.ShapeDtypeStruct((nb, nh, 2 * nt, dk), F32), jax.ShapeDtypeStruct((nb, nh, dk, dk), F32)),
        compiler_params=_cparams(("parallel",)), name="gdn_sample_state",
    )(r_bm, state_s)
    ws_tm = jnp.transpose(ws_bm, (2, 0, 1, 3)).reshape(2 * nt, nb, gw)
    og = pl.pallas_call(
        functools.partial(_gdn_sample_post_kernel, nt=nt),
        grid=(1,),
        in_specs=[full(ws_tm.shape), full(r_tm.shape), full(intra.shape), full(z_tm.shape), full(gn.shape)],
        out_specs=full((nt, nb, gw)),
        out_shape=jax.ShapeDtypeStruct((nt, nb, gw), BF16),
        compiler_params=_cparams(("arbitrary",)), name="gdn_sample_post",
    )(ws_tm, r_tm, intra, z_tm, gn)
    return og, s_new


def _lambda(lq1_ref, lk1_ref, lq2_ref, lk2_ref, lam_init):
    s1 = jnp.sum(lq1_ref[...] * lk1_ref[...], axis=-1, keepdims=True)
    s2 = jnp.sum(lq2_ref[...] * lk2_ref[...], axis=-1, keepdims=True)
    return jnp.exp(s1) - jnp.exp(s2) + lam_init


def _subln(o, g_ref, lam_init):
    return o * lax.rsqrt(jnp.mean(o * o, axis=-1, keepdims=True) + EPS) * g_ref[...] * (1.0 - lam_init)


def _attn_prompt_kernel(q_ref, k_ref, vt_ref, lq1_ref, lk1_ref, lq2_ref, lk2_ref, sg_ref, o_ref,
                        sa_scr, sb_scr, m_scr, acc_scr, *, tq, lam_init):
    qi = pl.program_id(1)
    nh, dv, dqk = DIFF_HEADS, DIFF_DV, DIFF_DQK
    lam = _lambda(lq1_ref, lk1_ref, lq2_ref, lk2_ref, lam_init)
    lane = lax.broadcasted_iota(jnp.int32, (tq, dv), 1)
    kidx = lax.broadcasted_iota(jnp.int32, (tq, tq), 0)
    qidx = lax.broadcasted_iota(jnp.int32, (tq, tq), 1)
    causal = kidx <= qidx
    nsum = 2 * SUBLANES
    ones_rows = jnp.ones((nsum, tq), BF16)
    lanes = [slice(h * dv, (h + 1) * dv) for h in range(nh)]
    qs = []
    for h in range(nh):
        q = q_ref[:, lanes[h]]
        zero = jnp.zeros_like(q)
        qs.append((jnp.where(lane < dqk, q, zero), jnp.where(lane >= dqk, q, zero)))

    nvt = tq // vt_ref.shape[-1]
    nchain = 2 * nh

    def scores(dst, j):
        for h in range(nh):
            kb = k_ref[pl.ds(pl.multiple_of(j * tq, tq), tq), lanes[h]]
            for c in range(2):
                dst[2 * h + c] = _dot_nt(kb, qs[h][c])

    def softmax_pv(src, j, masked):
        ps, alphas = [], []
        for x in range(nchain):
            s_x = jnp.where(causal, src[x], NEG) if masked else src[x]
            m_i = m_scr[x]
            m_new = jnp.maximum(m_i, jnp.max(s_x, axis=0, keepdims=True))
            m_scr[x] = m_new
            alphas.append(jnp.exp2(m_i - m_new))
            ps.append(jnp.exp2(s_x - m_new).astype(BF16))
        for h in range(nh):
            vt = jnp.concatenate([vt_ref[j * nvt + t, lanes[h], :] for t in range(nvt)], axis=1)
            vt = jnp.concatenate([vt, ones_rows], axis=0)
            for c in range(2):
                x = 2 * h + c
                acc_scr[x] = alphas[x] * acc_scr[x] + _dot(vt, ps[x])

    m_scr[...] = jnp.full(m_scr.shape, NEG, F32)
    acc_scr[...] = jnp.zeros(acc_scr.shape, F32)
    scores(sa_scr, 0)

    def two_blocks(i, _):
        scores(sb_scr, 2 * i + 1)
        softmax_pv(sa_scr, 2 * i, False)
        scores(sa_scr, 2 * i + 2)
        softmax_pv(sb_scr, 2 * i + 1, False)
        return 0

    lax.fori_loop(0, lax.shift_right_logical(qi, 1), two_blocks, 0)
    odd = lax.rem(qi, 2) == 1

    @pl.when(odd)
    def _():
        scores(sb_scr, qi)
        softmax_pv(sa_scr, qi - 1, False)
        softmax_pv(sb_scr, qi, True)

    @pl.when(jnp.logical_not(odd))
    def _():
        softmax_pv(sa_scr, qi, True)

    for h in range(nh):
        ls = lanes[h]
        a1, a2 = acc_scr[2 * h], acc_scr[2 * h + 1]
        ot = a1[:dv] / a1[dv:dv + 1] - lam * (a2[:dv] / a2[dv:dv + 1])
        ot = ot * lax.rsqrt(jnp.mean(ot * ot, axis=0, keepdims=True) + EPS) * (1.0 - lam_init)
        o_ref[:, ls] = (jnp.transpose(ot) * sg_ref[...]).astype(o_ref.dtype)


def _attn_prompt(qa, ka, vt, lq1, lk1, lq2, lk2, sg, *, nb, seq, tq, lam_init):
    nh, dv = DIFF_HEADS, DIFF_DV
    wd = nh * dv
    nq = seq // tq
    qa3, ka3 = qa.reshape(nb, seq, wd), ka.reshape(nb, seq, wd)
    tv = vt.shape[-1]
    nv = seq // tv
    vt4 = vt.reshape(nb, nv, wd, tv)
    vec = _const_spec((1, DIFF_DQK))
    out = pl.pallas_call(
        functools.partial(_attn_prompt_kernel, tq=tq, lam_init=lam_init),
        grid=(nb, nq),
        in_specs=[pl.BlockSpec((None, tq, wd), lambda b, i: (b, i, 0)),
                  pl.BlockSpec((None, seq, wd), lambda b, i: (b, 0, 0)),
                  pl.BlockSpec((None, nv, wd, tv), lambda b, i: (b, 0, 0, 0)),
                  vec, vec, vec, vec, _const_spec((1, dv))],
        out_specs=pl.BlockSpec((None, tq, wd), lambda b, i: (b, i, 0)),
        out_shape=jax.ShapeDtypeStruct((nb, seq, wd), BF16),
        scratch_shapes=[pltpu.VMEM((2 * nh, tq, tq), F32), pltpu.VMEM((2 * nh, tq, tq), F32),
                        pltpu.VMEM((2 * nh, 1, tq), F32), pltpu.VMEM((2 * nh, dv + 2 * SUBLANES, tq), F32)],
        compiler_params=_cparams(("parallel", "arbitrary")), name="attn_prompt",
    )(qa3, ka3, vt4, lq1, lk1, lq2, lk2, sg)
    return out.reshape(nb * seq, wd)


def _attn_sample_kernel(pt_ref, q_ref, kn_ref, vn_ref, *rest, n_pages, nt, bs, lam_init):
    k_refs = rest[:bs * n_pages]
    v_refs = rest[bs * n_pages:2 * bs * n_pages]
    lq1_ref, lk1_ref, lq2_ref, lk2_ref, sg_ref, o_ref = rest[2 * bs * n_pages:]
    del pt_ref
    nh, dv, dqk = DIFF_HEADS, DIFF_DV, DIFF_DQK
    page = k_refs[0].shape[0] // nh
    nr = 2 * nt
    lam = _lambda(lq1_ref, lk1_ref, lq2_ref, lk2_ref, lam_init)
    row = lax.broadcasted_iota(jnp.int32, (nr, dv), 0)
    lane = lax.broadcasted_iota(jnp.int32, (nr, dv), 1)
    first = row < nt
    keep = jnp.logical_xor(lane >= dqk, first)
    tpos = jnp.where(first, row, row - nt)
    new_ok = (lane < nt) & (lane <= tpos)
    zpad = jnp.zeros((page - nr, dv), F32)
    lanes = [slice(h * dv, (h + 1) * dv) for h in range(nh)]
    head_rows = [pl.ds(h, page, stride=nh) for h in range(nh)]
    groups = [list(range(j, min(j + 2, n_pages))) for j in range(0, n_pages, 2)]

    chains = [(i, h) for i in range(bs) for h in range(nh)]

    def past(refs, i, h, grp):
        return jnp.concatenate([refs[i * n_pages + j][head_rows[h], :] for j in grp], axis=0).astype(BF16)

    s_all = {}
    for i, h in chains:
        qh = q_ref[i, :, lanes[h]]
        qz = jnp.where(keep, qh, jnp.zeros_like(qh))
        s_ih = [_dot_nt(qz, past(k_refs, i, h, grp)) for grp in groups]
        knew = jnp.concatenate([kn_ref[i, :, lanes[h]], zpad], axis=0)
        s_ih.append(jnp.where(new_ok, _dot_nt(qz, knew.astype(BF16)), NEG))
        s_all[i, h] = s_ih
    p_all, l_all = {}, {}
    for x in chains:
        m = jnp.max(s_all[x][0], axis=-1, keepdims=True)
        for s in s_all[x][1:]:
            m = jnp.maximum(m, jnp.max(s, axis=-1, keepdims=True))
        ps = [jnp.exp2(s - m) for s in s_all[x]]
        l = jnp.sum(ps[0], axis=-1, keepdims=True)
        for p in ps[1:]:
            l = l + jnp.sum(p, axis=-1, keepdims=True)
        p_all[x] = [p.astype(BF16) for p in ps]
        l_all[x] = l
    for i, h in chains:
        acc = _dot(p_all[i, h][-1], jnp.concatenate([vn_ref[i, :, lanes[h]], zpad], axis=0).astype(BF16))
        for gi, grp in enumerate(groups):
            acc = acc + _dot(p_all[i, h][gi], past(v_refs, i, h, grp))
        o2 = acc / l_all[i, h]
        o = o2 - lam * pltpu.roll(o2, nt, axis=0)
        o_ref[i, :, lanes[h]] = _subln(o, sg_ref, lam_init)


def _attn_sample(page_table, q2, kn8, vn8, cache_k, cache_v, lq1, lk1, lq2, lk2, sg, *, nt, bs, lam_init):
    nb, n_pages = page_table.shape
    nh, dv = DIFF_HEADS, DIFF_DV
    page = cache_k.shape[1]
    wd = nh * dv
    ck = cache_k.reshape(cache_k.shape[0], page * nh, dv)
    cv = cache_v.reshape(cache_v.shape[0], page * nh, dv)
    nr = 2 * nt
    small = pl.BlockSpec((bs, nr, wd), lambda b, pt: (b, 0, 0))

    def page_spec(i, j):
        return pl.BlockSpec((None, page * nh, dv), lambda b, pt: (pt[b * bs + i, j], 0, 0))

    pages = [page_spec(i, j) for i in range(bs) for j in range(n_pages)]
    vec = pl.BlockSpec((1, DIFF_DQK), lambda b, pt: (0, 0))
    grid_spec = pltpu.PrefetchScalarGridSpec(
        num_scalar_prefetch=1, grid=(nb // bs,),
        in_specs=[small, small, small] + pages * 2
        + [vec, vec, vec, vec, pl.BlockSpec((1, dv), lambda b, pt: (0, 0))],
        out_specs=small)
    return pl.pallas_call(
        functools.partial(_attn_sample_kernel, n_pages=n_pages, nt=nt, bs=bs, lam_init=lam_init),
        grid_spec=grid_spec, out_shape=jax.ShapeDtypeStruct((nb, nr, wd), F32),
        compiler_params=_cparams(("parallel",)), name="attn_sample",
    )(page_table, q2, kn8, vn8, *([ck] * (bs * n_pages)), *([cv] * (bs * n_pages)), lq1, lk1, lq2, lk2, sg)


def _post_kernel(x_ref, og_ref, od_ref, hist_ref, wo_ref, gf_ref, wu_ref, fcw_ref, fcb_ref, wd_ref,
                 y_ref, tail_ref, ubuf, *, tm, shift, hr, dff, ncol):
    i = pl.program_id(1)
    gw = og_ref.shape[1]

    @pl.when(i == 0)
    def _():
        ubuf[0:hr, :] = hist_ref[...]

    hres = x_ref[...] + _dot(og_ref[...], wo_ref[0:gw, :]) + _dot(od_ref[...], wo_ref[gw:, :])
    hn = (hres * lax.rsqrt(jnp.mean(hres * hres, axis=-1, keepdims=True) + EPS) * gf_ref[...]).astype(BF16)
    fcw = fcw_ref[...]
    fcb = fcb_ref[...]

    def conv(cols):
        ubuf[hr:hr + tm, cols] = _dot(hn, wu_ref[:, cols])
        u = fcw[FFN_CONV - 1:FFN_CONV, cols] * ubuf[hr:hr + tm, cols] + fcb[:, cols]
        for j in range(1, FFN_CONV):
            u = u + fcw[FFN_CONV - 1 - j:FFN_CONV - j, cols] * ubuf[hr - j * shift:hr - j * shift + tm, cols]
        return u

    y = hres
    wc = dff // ncol
    for c in range(ncol):
        gate = conv(slice(c * wc, (c + 1) * wc))
        val = conv(slice(dff + c * wc, dff + (c + 1) * wc))
        act = (_silu(gate) * val).astype(BF16)
        y = y + _dot(act, wd_ref[c * wc:(c + 1) * wc, :])
    y_ref[...] = y
    tail = ubuf[tm:tm + hr, :]
    tail_ref[...] = tail
    ubuf[0:hr, :] = tail


def _post(x2d, og, od, hist, w_out, g_ffn, w_up, fcw, fcb, w_down, *, nseq, tm, shift, hr, ncol):
    n, d = x2d.shape
    gw = og.shape[1]
    dff2 = w_up.shape[1]
    nt = n // (nseq * tm)
    row = lambda b, i: (b * nt + i, 0)
    in_specs = [
        pl.BlockSpec((tm, d), row), pl.BlockSpec((tm, gw), row), pl.BlockSpec((tm, od.shape[1]), row),
        pl.BlockSpec((None, hr, dff2), lambda b, i: (b, 0, 0)),
        _const_spec(w_out.shape), _const_spec((1, d)), _const_spec(w_up.shape),
        _const_spec((FFN_CONV, dff2)), _const_spec((1, dff2)), _const_spec(w_down.shape),
    ]
    out_shape = (jax.ShapeDtypeStruct((n, d), F32), jax.ShapeDtypeStruct((nseq, hr, dff2), F32))
    out_specs = (pl.BlockSpec((tm, d), row), pl.BlockSpec((None, hr, dff2), lambda b, i: (b, 0, 0)))
    return pl.pallas_call(
        functools.partial(_post_kernel, tm=tm, shift=shift, hr=hr, dff=dff2 // 2, ncol=ncol),
        grid=(nseq, nt), in_specs=in_specs, out_specs=out_specs, out_shape=out_shape,
        scratch_shapes=[pltpu.VMEM((hr + tm, dff2), F32)],
        compiler_params=_cparams(("parallel", "arbitrary")), name="post",
    )(x2d, og, od, hist, w_out, g_ffn, w_up, fcw, fcb, w_down)


def _lane_pad(vec, offset):
    out = jnp.zeros((1, LANES), F32)
    return lax.dynamic_update_slice(out, vec.reshape(1, -1).astype(F32), (0, offset))


def _layer(l, x_prompt, x_sample, state_gdn_conv, state_gdn_s, cache_k, cache_v, page_table, state_ffn_conv, wl):
    (attn_norm_g, w_in, gdn_conv_w, gdn_a_log, gdn_dt_bias, gdn_out_norm_g, diff_q_norm_g, diff_k_norm_g,
     lq1, lk1, lq2, lk2, diff_subln_g, w_out, ffn_norm_g, w_up, ffn_conv_w, ffn_conv_b, w_down) = wl
    nbp, seq, d = x_prompt.shape
    nbs, nts, _ = x_sample.shape
    nh, dk = GDN_HEADS, GDN_DK
    gw = nh * dk
    dw = DIFF_HEADS * DIFF_DV
    lam_init = 0.8 - 0.6 * math.exp(-0.3 * l)

    c_b = 4 * gw
    c_d = c_b + 2 * nh
    w_g = w_in[:, :c_b].astype(BF16)
    w_d = w_in[:, c_d:].astype(BF16)
    w_t = jnp.transpose(jnp.concatenate(
        [w_in[:, c_b:c_d], jnp.zeros((d, LANES - 2 * nh), w_in.dtype)], axis=1)).astype(BF16)
    reps = dw // DIFF_DQK
    gq_t = jnp.tile(diff_q_norm_g.reshape(1, -1), (1, reps))
    gk_t = jnp.tile(diff_k_norm_g.reshape(1, -1), (1, reps))
    g_attn = attn_norm_g.reshape(1, d)
    alr, dtr = _lane_pad(gdn_a_log, nh), _lane_pad(gdn_dt_bias, nh)
    alc = jnp.transpose(alr[:, :SUBLANES])
    dtc = jnp.transpose(dtr[:, :SUBLANES])
    gn = gdn_out_norm_g.reshape(1, dk)
    vecs = [v.reshape(1, -1) for v in (lq1, lk1, lq2, lk2)]
    sg = diff_subln_g.reshape(1, -1)
    w_out_b, w_up_b, w_down_b = w_out.astype(BF16), w_up.astype(BF16), w_down.astype(BF16)
    g_ffn = ffn_norm_g.reshape(1, d)
    fcb = ffn_conv_b.reshape(1, -1)
    dff2 = w_up.shape[1]

    xp2 = x_prompt.reshape(nbp * seq, d)
    (z, gcol, grow, qn, kn, kb, vb, tail_g, knew, vnew, qa, ka, vt) = _inproj_gdn(
        xp2, g_attn, w_g, w_d, w_t, gq_t, gk_t, gdn_conv_w, alr, dtr, alc, dtc,
        nb=nbp, tm=min(_Tiles.inproj_rows, seq), gw=gw, dw=dw)
    og, s_prompt = _gdn_prompt(qn, kn, kb, vb, gcol, grow, z, gn, nb=nbp, seq=seq,
                               tl=min(_Tiles.gdn_tokens, seq))
    od = _attn_prompt(qa, ka, vt, *vecs, sg, nb=nbp, seq=seq, tq=min(_Tiles.attn_tile, seq), lam_init=lam_init)
    hr_p = SUBLANES
    y_p, tail_p = _post(xp2, og, od, jnp.zeros((nbp, hr_p, dff2), F32), w_out_b, g_ffn, w_up_b,
                        ffn_conv_w, fcb, w_down_b, nseq=nbp, tm=min(_Tiles.post_rows, seq), shift=1, hr=hr_p,
                        ncol=_Tiles.post_col_groups)
    out_p = (y_p.reshape(nbp, seq, d),
             tail_g[:, SUBLANES - (GDN_CONV - 1):, :],
             s_prompt,
             knew.reshape(nbp, seq, DIFF_HEADS, DIFF_DV),
             vnew.reshape(nbp, seq, DIFF_HEADS, DIFF_DV),
             tail_p[:, hr_p - (FFN_CONV - 1):, :])

    xs2 = jnp.transpose(x_sample, (1, 0, 2)).reshape(nts * nbs, d)
    tm_s = min(_Tiles.sample_rows, nts * nbs)
    assert tm_s % nbs == 0 and nbs % _Tiles.paged_seqs == 0
    qkv, z, ba, knew, vnew, qa, _, _ = _inproj(xs2, g_attn, w_g, w_d, w_t, gq_t, gk_t, tm=tm_s, gw=gw, dw=dw)
    qkv_tm = qkv.reshape(nts, nbs, 3 * gw)
    hist_tm = jnp.transpose(state_gdn_conv, (1, 0, 2))
    og_tm, s_sample = _gdn_sample(qkv_tm, hist_tm, ba.reshape(nts, nbs, LANES), z.reshape(nts, nbs, gw),
                                  state_gdn_s, gdn_conv_w, alr, dtr, gn, bb=min(_Tiles.state_seqs, nbs))
    conv_all = jnp.concatenate([hist_tm, qkv_tm], axis=0)
    conv_s = jnp.transpose(conv_all[nts:], (1, 0, 2))
    to_bm = lambda a: jnp.transpose(a.reshape(nts, nbs, -1), (1, 0, 2))
    q_bm = to_bm(qa)
    q2 = jnp.concatenate([q_bm, q_bm], axis=1)
    pad = jnp.zeros((nbs, nts, dw), F32)
    kn_bm, vn_bm = to_bm(knew), to_bm(vnew)
    od_bm = _attn_sample(page_table, q2, jnp.concatenate([kn_bm, pad], axis=1),
                         jnp.concatenate([vn_bm, pad], axis=1), cache_k, cache_v, *vecs, sg,
                         nt=nts, bs=_Tiles.paged_seqs, lam_init=lam_init)
    od_tm = jnp.transpose(od_bm[:, :nts, :], (1, 0, 2)).reshape(nts * nbs, dw).astype(BF16)
    hr_s = (FFN_CONV - 1) * nbs
    hist_f = jnp.transpose(state_ffn_conv, (1, 0, 2)).reshape(1, hr_s, dff2)
    y_s, tail_s = _post(xs2, og_tm.reshape(nts * nbs, gw), od_tm, hist_f, w_out_b, g_ffn, w_up_b,
                        ffn_conv_w, fcb, w_down_b, nseq=1, tm=tm_s, shift=nbs, hr=hr_s,
                        ncol=_Tiles.post_col_groups)
    out_s = (jnp.transpose(y_s.reshape(nts, nbs, d), (1, 0, 2)),
             conv_s,
             s_sample,
             kn_bm.reshape(nbs, nts, DIFF_HEADS, DIFF_DV),
             vn_bm.reshape(nbs, nts, DIFF_HEADS, DIFF_DV),
             jnp.transpose(tail_s.reshape(FFN_CONV - 1, nbs, dff2), (1, 0, 2)))
    return out_p, out_s


def kernel(x_prompt, x_sample, state_gdn_conv, state_gdn_S, cache_k, cache_v, page_table, state_ffn_conv, attn_norm_g, w_in, gdn_conv_w, gdn_A_log, gdn_dt_bias, gdn_out_norm_g, diff_q_norm_g, diff_k_norm_g, diff_lambda_q1, diff_lambda_k1, diff_lambda_q2, diff_lambda_k2, diff_subln_g, w_out, ffn_norm_g, w_up, ffn_conv_w, ffn_conv_b, w_down):
    depth = w_in.shape[0]
    hp, hs = x_prompt, x_sample
    outs_p, outs_s = [], []
    for l in range(depth):
        wl = (attn_norm_g[l], w_in[l], gdn_conv_w[l], gdn_A_log[l], gdn_dt_bias[l], gdn_out_norm_g[l],
              diff_q_norm_g[l], diff_k_norm_g[l], diff_lambda_q1[l], diff_lambda_k1[l], diff_lambda_q2[l],
              diff_lambda_k2[l], diff_subln_g[l], w_out[l], ffn_norm_g[l], w_up[l], ffn_conv_w[l],
              ffn_conv_b[l], w_down[l])
        out_p, out_s = _layer(l, hp, hs, state_gdn_conv[l], state_gdn_S[l], cache_k[l], cache_v[l],
                              page_table, state_ffn_conv[l], wl)
        hp, hs = out_p[0], out_s[0]
        outs_p.append(out_p[1:])
        outs_s.append(out_s[1:])
    stack = lambda outs, i: jnp.stack([o[i] for o in outs])
    return (hp, hs) + tuple(stack(outs_p, i) for i in range(5)) + tuple(stack(outs_s, i) for i in range(5))
```

```python
import functools
import math

import jax
import jax.numpy as jnp
from jax import lax
from jax.experimental import pallas as pl
from jax.experimental.pallas import tpu as pltpu

F32 = jnp.float32
BF16 = jnp.bfloat16
EPS = 1e-6
NEG = -1e30

GDN_HEADS = 4
GDN_DK = 128
GDN_CONV = 4
GDN_CHUNK = 64
DIFF_HEADS = 4
DIFF_DV = 128
DIFF_DQK = 64
FFN_CONV = 3
LANES = 128
SUBLANES = 8
V7X_VMEM_BYTES = 64 * 1024 * 1024
VMEM_LIMIT = V7X_VMEM_BYTES - 8 * 1024 * 1024


class _Tiles:
    inproj_rows = 512
    gdn_tokens = 256
    attn_tile = 512
    post_rows = 512
    post_col_groups = 1
    sample_rows = 256
    state_seqs = 16
    paged_seqs = 2


def _cparams(sem):
    return pltpu.CompilerParams(dimension_semantics=sem, vmem_limit_bytes=VMEM_LIMIT)


def _const_spec(shape):
    nd = len(shape)
    return pl.BlockSpec(shape, lambda *_: (0,) * nd, pipeline_mode=pl.Buffered(1))


def _dot(a, b):
    return jnp.dot(a, b, preferred_element_type=F32)


def _dot_nt(a, b):
    return lax.dot_general(a, b, (((1,), (1,)), ((), ())), preferred_element_type=F32)


def _dot_tn(a, b):
    return lax.dot_general(a, b, (((0,), (0,)), ((), ())), preferred_element_type=F32)


def _softplus(x):
    return jnp.maximum(x, 0.0) + jnp.log1p(jnp.exp(-jnp.abs(x)))


def _silu(x):
    return x * jax.nn.sigmoid(x)


def _split3(x):
    hi = x.astype(BF16)
    r = x - hi.astype(F32)
    mid = r.astype(BF16)
    lo = (r - mid.astype(F32)).astype(BF16)
    return hi, mid, lo


def _inproj_core(x_ref, g_ref, wg_ref, wd_ref, wt_ref, gq_ref, gk_ref,
                 knew_ref, vnew_ref, qa_ref, ka_ref, vt_ref, gw, dw):
    tm = x_ref.shape[0]
    x = x_ref[...]
    xn = x * lax.rsqrt(jnp.mean(x * x, axis=-1, keepdims=True) + EPS) * g_ref[...]
    xb = xn.astype(BF16)
    proj = _dot(xb, wg_ref[...])
    pd = _dot(xb, wd_ref[...])
    tr = _dot_nt(wt_ref[...], xb)
    dq = pd[:, :dw]
    dk = pd[:, dw:2 * dw]
    dv = pd[:, 2 * dw:]
    vt_ref[...] = jnp.transpose(dv).astype(BF16)
    low =lax.broadcasted_iota(jnp.int32, (tm, DIFF_DV), 1) < DIFF_DQK

    def half_rms(t):
        out = []
        for h in range(DIFF_HEADS):
            th = t[:, h * DIFF_DV:(h + 1) * DIFF_DV]
            sq = th * th
            s1 = jnp.sum(jnp.where(low, sq, 0.0), axis=-1, keepdims=True)
            s2 = jnp.sum(jnp.where(low, 0.0, sq), axis=-1, keepdims=True)
            r1 = lax.rsqrt(s1 * (1.0 / DIFF_DQK) + EPS)
            r2 = lax.rsqrt(s2 * (1.0 / DIFF_DQK) + EPS)
            out.append(th * jnp.where(low, r1, r2))
        return jnp.concatenate(out, axis=1)

    dqn = half_rms(dq) * gq_ref[...]
    dkn = half_rms(dk) * gk_ref[...]
    for h in range(DIFF_HEADS):
        head_rows = pl.ds(h, tm, stride=DIFF_HEADS)
        knew_ref[head_rows, :] = dkn[:, h * DIFF_DV:(h + 1) * DIFF_DV]
        vnew_ref[head_rows, :] = dv[:, h * DIFF_DV:(h + 1) * DIFF_DV]
    qa_ref[...] = (dqn * (DIFF_DQK ** -0.5 * math.log2(math.e))).astype(BF16)
    ka_ref[...] = dkn.astype(BF16)
    return proj, tr


def _inproj_kernel(x_ref, g_ref, wg_ref, wd_ref, wt_ref, gq_ref, gk_ref,
                   qkv_ref, z_ref, ba_ref, knew_ref, vnew_ref, qa_ref, ka_ref, vt_ref, *, gw, dw):
    proj, trow = _inproj_core(x_ref, g_ref, wg_ref, wd_ref, wt_ref, gq_ref, gk_ref,
                              knew_ref, vnew_ref, qa_ref, ka_ref, vt_ref, gw, dw)
    qkv_ref[...] = proj[:, :3 * gw]
    z_ref[...] = proj[:, 3 * gw:]
    ba_ref[...] = jnp.transpose(trow)


def _inproj_gdn_kernel(x_ref, g_ref, wg_ref, wd_ref, wt_ref, gq_ref, gk_ref,
                       cw_ref, alr_ref, dtr_ref, alc_ref, dtc_ref,
                       z_ref, gcol_ref, grow_ref, qn_ref, kn_ref, kb_ref, vb_ref, tail_ref,
                       knew_ref, vnew_ref, qa_ref, ka_ref, vt_ref, xbuf, *, gw, dw):
    i = pl.program_id(1)
    tm = x_ref.shape[0]
    nh, dk = GDN_HEADS, GDN_DK
    hr = SUBLANES

    @pl.when(i == 0)
    def _():
        xbuf[0:hr, :] = jnp.zeros((hr, 3 * gw), F32)

    proj, trow = _inproj_core(x_ref, g_ref, wg_ref, wd_ref, wt_ref, gq_ref, gk_ref,
                              knew_ref, vnew_ref, qa_ref, ka_ref, vt_ref, gw, dw)
    z_ref[...] = proj[:, 3 * gw:]
    xq = proj[:, :3 * gw]
    prev = xbuf[...]
    tail = xq[tm - hr:, :]
    tail_ref[...] = tail
    xbuf[...] = tail
    cw = cw_ref[...]
    first8 = lax.broadcasted_iota(jnp.int32, (hr, 3 * gw), 0)
    y = cw[GDN_CONV - 1:GDN_CONV, :] * xq
    for j in range(1, GDN_CONV):
        sh = pltpu.roll(xq, j, axis=0)
        head = jnp.where(first8 < j, pltpu.roll(prev, j, axis=0), sh[:hr])
        y = y + cw[GDN_CONV - 1 - j:GDN_CONV - j, :] * jnp.concatenate([head, sh[hr:]], axis=0)
    y = _silu(y)
    ba = jnp.transpose(trow)
    beta_c = jax.nn.sigmoid(ba)
    gcol_ref[...] = -jnp.exp(alr_ref[...]) * _softplus(ba + dtr_ref[...])
    grow_ref[...] = -jnp.exp(alc_ref[...]) * _softplus(trow[:SUBLANES] + dtc_ref[...])
    for h in range(nh):
        ls = slice(h * dk, (h + 1) * dk)
        q = y[:, h * dk:(h + 1) * dk]
        k = y[:, gw + h * dk:gw + (h + 1) * dk]
        v = y[:, 2 * gw + h * dk:2 * gw + (h + 1) * dk]
        qn = q * lax.rsqrt(jnp.sum(q * q, axis=-1, keepdims=True) + EPS) * (dk ** -0.5)
        kn = k * lax.rsqrt(jnp.sum(k * k, axis=-1, keepdims=True) + EPS)
        beta = jnp.broadcast_to(beta_c[:, h:h + 1], (tm, dk))
        qn_ref[:, ls] = qn.astype(BF16)
        kn_ref[:, ls] = kn.astype(BF16)
        kb_ref[:, ls] = (kn * beta).astype(BF16)
        vb_ref[:, ls] = (v * beta).astype(BF16)


def _attn_out(n, tm, dw):
    return (
        jax.ShapeDtypeStruct((n * DIFF_HEADS, DIFF_DV), F32),
        jax.ShapeDtypeStruct((n * DIFF_HEADS, DIFF_DV), F32),
        jax.ShapeDtypeStruct((n, dw), BF16),
        jax.ShapeDtypeStruct((n, dw), BF16),
        jax.ShapeDtypeStruct((n // tm, dw, tm), BF16),
    )


def _inproj(x2d, g_attn, w_g, w_d, w_t, gq_t, gk_t, *, tm, gw, dw):
    n, d = x2d.shape
    row = lambda i: (i, 0)
    out_shape = (
        jax.ShapeDtypeStruct((n, 3 * gw), F32),
        jax.ShapeDtypeStruct((n, gw), F32),
        jax.ShapeDtypeStruct((n, LANES), F32),
    ) + _attn_out(n, tm, dw)
    out_specs = (
        pl.BlockSpec((tm, 3 * gw), row), pl.BlockSpec((tm, gw), row), pl.BlockSpec((tm, LANES), row),
        pl.BlockSpec((tm * DIFF_HEADS, DIFF_DV), row), pl.BlockSpec((tm * DIFF_HEADS, DIFF_DV), row),
        pl.BlockSpec((tm, dw), row), pl.BlockSpec((tm, dw), row),
        pl.BlockSpec((None, dw, tm), lambda i: (i, 0, 0)),
    )
    in_specs = [
        pl.BlockSpec((tm, d), row), _const_spec((1, d)), _const_spec(w_g.shape), _const_spec(w_d.shape),
        _const_spec(w_t.shape), _const_spec((1, dw)), _const_spec((1, dw)),
    ]
    return pl.pallas_call(
        functools.partial(_inproj_kernel, gw=gw, dw=dw),
        grid=(n // tm,), in_specs=in_specs, out_specs=out_specs, out_shape=out_shape,
        compiler_params=_cparams(("parallel",)), name="inproj",
    )(x2d, g_attn, w_g, w_d, w_t, gq_t, gk_t)


def _inproj_gdn(x2d, g_attn, w_g, w_d, w_t, gq_t, gk_t, cw, alr, dtr, alc, dtc, *, nb, tm, gw, dw):
    n, d = x2d.shape
    nt = n // (nb * tm)
    row = lambda b, i: (b * nt + i, 0)
    out_shape = (
        jax.ShapeDtypeStruct((n, gw), F32),
        jax.ShapeDtypeStruct((n, LANES), F32),
        jax.ShapeDtypeStruct((nb, SUBLANES, n // nb), F32),
        jax.ShapeDtypeStruct((n, gw), BF16),
        jax.ShapeDtypeStruct((n, gw), BF16),
        jax.ShapeDtypeStruct((n, gw), BF16),
        jax.ShapeDtypeStruct((n, gw), BF16),
        jax.ShapeDtypeStruct((nb, SUBLANES, 3 * gw), F32),
    ) + _attn_out(n, tm, dw)
    out_specs = (
        pl.BlockSpec((tm, gw), row), pl.BlockSpec((tm, LANES), row),
        pl.BlockSpec((None, SUBLANES, tm), lambda b, i: (b, 0, i)),
        pl.BlockSpec((tm, gw), row), pl.BlockSpec((tm, gw), row), pl.BlockSpec((tm, gw), row),
        pl.BlockSpec((tm, gw), row),
        pl.BlockSpec((None, SUBLANES, 3 * gw), lambda b, i: (b, 0, 0)),
        pl.BlockSpec((tm * DIFF_HEADS, DIFF_DV), row), pl.BlockSpec((tm * DIFF_HEADS, DIFF_DV), row),
        pl.BlockSpec((tm, dw), row), pl.BlockSpec((tm, dw), row),
        pl.BlockSpec((None, dw, tm), lambda b, i: (b * nt + i, 0, 0)),
    )
    in_specs = [
        pl.BlockSpec((tm, d), row), _const_spec((1, d)), _const_spec(w_g.shape), _const_spec(w_d.shape),
        _const_spec(w_t.shape), _const_spec((1, dw)), _const_spec((1, dw)),
        _const_spec(cw.shape), _const_spec((1, LANES)), _const_spec((1, LANES)),
        _const_spec((SUBLANES, 1)), _const_spec((SUBLANES, 1)),
    ]
    return pl.pallas_call(
        functools.partial(_inproj_gdn_kernel, gw=gw, dw=dw),
        grid=(nb, nt), in_specs=in_specs, out_specs=out_specs, out_shape=out_shape,
        scratch_shapes=[pltpu.VMEM((SUBLANES, 3 * gw), F32)],
        compiler_params=_cparams(("parallel", "arbitrary")), name="inproj_gdn",
    )(x2d, g_attn, w_g, w_d, w_t, gq_t, gk_t, cw, alr, dtr, alc, dtc)


def _gdn_prompt_kernel(qn_ref, kn_ref, kb_ref, vb_ref, gcol_ref, grow_ref, z_ref, gn_ref,
                       og_ref, sout_ref, s_scr, *, tl):
    i = pl.program_id(0)
    nb = qn_ref.shape[0]
    nh, dk, c = GDN_HEADS, GDN_DK, GDN_CHUNK
    gw = nh * dk

    @pl.when(i == 0)
    def _():
        s_scr[...] = jnp.zeros_like(s_scr)

    g_r = jnp.concatenate([grow_ref[b] for b in range(nb)], axis=0)

    ii = lax.broadcasted_iota(jnp.int32, (tl, tl), 0)
    jj = lax.broadcasted_iota(jnp.int32, (tl, tl), 1)
    same = lax.shift_right_logical(ii, 6) == lax.shift_right_logical(jj, 6)
    tri = jnp.where(same & (ii >= jj), 1.0, 0.0).astype(BF16)
    triu = jnp.where(same & (ii <= jj), 1.0, 0.0).astype(BF16)
    gc_c = [sum(_dot(tri, p) for p in _split3(gcol_ref[b])) for b in range(nb)]
    gc_r = sum(_dot(p, triu) for p in _split3(g_r))

    pr = 2 * c
    pi = lax.broadcasted_iota(jnp.int32, (pr, pr), 0)
    pj = lax.broadcasted_iota(jnp.int32, (pr, pr), 1)
    psame = lax.shift_right_logical(pi, 6) == lax.shift_right_logical(pj, 6)
    lower = psame & (pi >= pj)
    strict = psame & (pi > pj)
    eye = jnp.where(pi == pj, 1.0, 0.0).astype(F32)
    zeros_c = jnp.zeros((c, dk), F32)

    npair = tl // pr
    seqs = [(b, h) for b in range(nb) for h in range(nh)]
    hp = [(b, h, p) for b, h in seqs for p in range(npair)]

    def tile(ref, b, h, p):
        return ref[b, p * pr:(p + 1) * pr, h * dk:(h + 1) * dk]

    gcp = {(b, h, p): jnp.broadcast_to(gc_c[b][p * pr:(p + 1) * pr, nh + h:nh + h + 1], (pr, dk))
           for b, h, p in hp}
    dec = {(b, h, p): jnp.exp(jnp.where(
        lower, gcp[b, h, p] - gc_r[b * SUBLANES + nh + h:b * SUBLANES + nh + h + 1, p * pr:(p + 1) * pr], NEG))
        for b, h, p in hp}
    kpb = {x: tile(kn_ref, *x) for x in hp}
    kbp = {x: tile(kb_ref, *x) for x in hp}
    m = {x: -jnp.where(strict, _dot_nt(kbp[x], kpb[x]) * dec[x], 0.0) for x in hp}
    a = {x: eye + m[x] for x in hp}
    pw = {x: _dot(m[x].astype(BF16), m[x].astype(BF16)) for x in hp}
    for it in range(1, 6):
        for x in hp:
            pwb = pw[x].astype(BF16)
            if it < 5:
                res = _dot(pwb, jnp.concatenate([a[x].astype(BF16), pwb], axis=1))
                a[x] = a[x] + res[:, :pr]
                pw[x] = res[:, pr:]
            else:
                a[x] = a[x] + _dot(pwb, a[x].astype(BF16))
    egc = {x: jnp.exp(gcp[x]) for x in hp}
    sol = {x: _dot(a[x].astype(BF16),
                   jnp.concatenate([tile(vb_ref, *x), (kbp[x].astype(F32) * egc[x]).astype(BF16)], axis=1))
           for x in hp}
    qk = {x: (_dot_nt(tile(qn_ref, *x), kpb[x]) * dec[x]).astype(BF16) for x in hp}
    qg = {x: tile(qn_ref, *x).astype(F32) * egc[x] for x in hp}

    s_cur = {x: s_scr[x] for x in seqs}
    o_rows = {x: [] for x in seqs}
    for p in range(npair):
        for cc in range(2):
            c0 = cc * c
            ws = {}
            for b, h in seqs:
                wq = jnp.concatenate([sol[b, h, p][c0:c0 + c, dk:], qg[b, h, p][c0:c0 + c]], axis=0)
                ws[b, h] = _dot(wq.astype(BF16), s_cur[b, h].astype(BF16))
            for b, h in seqs:
                g_c0 = gcp[b, h, p][c0:c0 + c]
                glast = g_c0[c - 1:c, :]
                kd = kpb[b, h, p][c0:c0 + c].astype(F32) * jnp.exp(glast - g_c0)
                v_new = (sol[b, h, p][c0:c0 + c, :dk] - ws[b, h][:c]).astype(BF16)
                s_cur[b, h] = s_cur[b, h] * jnp.exp(glast) + _dot_tn(kd.astype(BF16), v_new)
                zc = jnp.zeros_like(v_new)
                vn_pad = jnp.concatenate([v_new, zc] if cc == 0 else [zc, v_new], axis=0)
                o_rows[b, h].append(ws[b, h][c:] + _dot(qk[b, h, p][c0:c0 + c], vn_pad))
    for b, h in seqs:
        s_scr[b, h] = s_cur[b, h]
        o = jnp.concatenate(o_rows[b, h], axis=0)
        o = o * lax.rsqrt(jnp.mean(o * o, axis=-1, keepdims=True) + EPS) * gn_ref[...]
        og_ref[b, :, h * dk:(h + 1) * dk] = (o * _silu(z_ref[b, :, h * dk:(h + 1) * dk])).astype(og_ref.dtype)

    @pl.when(i == pl.num_programs(0) - 1)
    def _():
        sout_ref[...] = s_scr[...]


def _gdn_prompt(qn, kn, kb, vb, gcol, grow, z, gn, *, nb, seq, tl):
    nh, dk = GDN_HEADS, GDN_DK
    gw = nh * dk
    nt = seq // tl
    tok = lambda i: (0, i, 0)
    wide = pl.BlockSpec((nb, tl, gw), tok)
    per_seq = lambda a: a.reshape(nb, seq, a.shape[-1])
    in_specs = [
        wide, wide, wide, wide, pl.BlockSpec((nb, tl, LANES), tok),
        pl.BlockSpec((nb, SUBLANES, tl), lambda i: (0, 0, i)),
        wide, _const_spec((1, dk)),
    ]
    out_shape = (jax.ShapeDtypeStruct((nb, seq, gw), BF16),
                 jax.ShapeDtypeStruct((nb, nh, dk, dk), F32))
    out_specs = (wide, pl.BlockSpec((nb, nh, dk, dk), lambda i: (0, 0, 0, 0)))
    og, s_out = pl.pallas_call(
        functools.partial(_gdn_prompt_kernel, tl=tl),
        grid=(nt,), in_specs=in_specs, out_specs=out_specs, out_shape=out_shape,
        scratch_shapes=[pltpu.VMEM((nb, nh, dk, dk), F32)],
        compiler_params=_cparams(("arbitrary",)), name="gdn_prompt",
    )(per_seq(qn), per_seq(kn), per_seq(kb), per_seq(vb), per_seq(gcol), grow, per_seq(z), gn)
    return og.reshape(nb * seq, gw), s_out


def _gdn_sample_pre_kernel(qkv_ref, hist_ref, ba_ref, cw_ref, alr_ref, dtr_ref, r_ref, intra_ref, *, nt):
    nh, dk = GDN_HEADS, GDN_DK
    gw = nh * dk
    nb = qkv_ref.shape[1]
    cw = cw_ref[...]
    nhist = GDN_CONV - 1
    xp = [hist_ref[j] for j in range(nhist)] + [qkv_ref[t] for t in range(nt)]
    ys = []
    for t in range(nt):
        y = cw[0:1, :] * xp[t]
        for j in range(1, GDN_CONV):
            y = y + cw[j:j + 1, :] * xp[t + j]
        ys.append(_silu(y))
    beta, gc = [], []
    for t in range(nt):
        ba = ba_ref[t]
        beta_c = jax.nn.sigmoid(ba)
        g_c = -jnp.exp(alr_ref[...]) * _softplus(ba + dtr_ref[...])
        beta.append([jnp.broadcast_to(beta_c[:, h:h + 1], (nb, dk)) for h in range(nh)])
        g_t = [jnp.broadcast_to(g_c[:, nh + h:nh + h + 1], (nb, dk)) for h in range(nh)]
        gc.append(g_t if t == 0 else [gc[t - 1][h] + g_t[h] for h in range(nh)])
    zrow = jnp.zeros((nb, dk), F32)
    for h in range(nh):
        q, k, v = [], [], []
        for t in range(nt):
            qt = ys[t][:, h * dk:(h + 1) * dk]
            kt = ys[t][:, gw + h * dk:gw + (h + 1) * dk]
            q.append(qt * lax.rsqrt(jnp.sum(qt * qt, axis=-1, keepdims=True) + EPS) * (dk ** -0.5))
            k.append(kt * lax.rsqrt(jnp.sum(kt * kt, axis=-1, keepdims=True) + EPS))
            v.append(ys[t][:, 2 * gw + h * dk:2 * gw + (h + 1) * dk])
        g = [gc[t][h] for t in range(nt)]
        us, ws = [], []
        for t in range(nt):
            b_t = beta[t][h]
            u_t = v[t] * b_t
            w_t = k[t] * b_t * jnp.exp(g[t])
            for s in range(t):
                m_ts = b_t * jnp.sum(k[t] * k[s], axis=-1, keepdims=True) * jnp.exp(g[t] - g[s])
                u_t = u_t - m_ts * us[s]
                w_t = w_t - m_ts * ws[s]
            us.append(u_t)
            ws.append(w_t)
        lane = slice(h * dk, (h + 1) * dk)
        for t in range(nt):
            r_ref[t, :, lane] = ws[t]
            r_ref[nt + t, :, lane] = q[t] * jnp.exp(g[t])
            r_ref[2 * nt + t, :, lane] = us[t]
            r_ref[3 * nt + t, :, lane] = k[t] * jnp.exp(g[nt - 1] - g[t])
            for s in range(nt):
                if s <= t:
                    intra_ref[t * nt + s, :, lane] = (jnp.sum(q[t] * k[s], axis=-1, keepdims=True)
                                                       * jnp.exp(g[t] - g[s]))
                else:
                    intra_ref[t * nt + s, :, lane] = zrow
        r_ref[4 * nt, :, lane] = jnp.exp(g[nt - 1])
        for r in range(4 * nt + 1, r_ref.shape[0]):
            r_ref[r, :, lane] = zrow


def _gdn_sample_state_kernel(r_ref, s_ref, ws_ref, sout_ref, *, nt, bb):
    nh = GDN_HEADS
    rows = lax.broadcasted_iota(jnp.int32, (2 * nt, GDN_DK), 0)
    inst = [(bi, h) for bi in range(bb) for h in range(nh)]
    ws = {x: _dot(r_ref[x[0], x[1], 0:2 * nt, :].astype(BF16), s_ref[x].astype(BF16)) for x in inst}
    for x in inst:
        ud = r_ref[x[0], x[1], 2 * nt:4 * nt, :]
        vn = jnp.where(rows < nt, ud - ws[x], 0.0)
        kd = jnp.where(rows < nt, pltpu.roll(ud, nt, axis=0), 0.0)
        ws_ref[x] = ws[x]
        sout_ref[x] = (s_ref[x] * r_ref[x[0], x[1], 4 * nt:4 * nt + 1, :]
                       + _dot_tn(kd.astype(BF16), vn.astype(BF16)))


def _gdn_sample_post_kernel(ws_ref, r_ref, intra_ref, z_ref, gn_ref, og_ref, *, nt):
    nh, dk = GDN_HEADS, GDN_DK
    for h in range(nh):
        lane = slice(h * dk, (h + 1) * dk)
        vn = [r_ref[2 * nt + t, :, lane] - ws_ref[t, :, lane] for t in range(nt)]
        for t in range(nt):
            o = ws_ref[nt + t, :, lane]
            for s in range(t + 1):
                o = o + intra_ref[t * nt + s, :, lane] * vn[s]
            o = o * lax.rsqrt(jnp.mean(o * o, axis=-1, keepdims=True) + EPS) * gn_ref[...]
            og_ref[t, :, lane] = (o * _silu(z_ref[t, :, lane])).astype(og_ref.dtype)


def _gdn_sample(qkv_tm, hist_tm, ba_tm, z_tm, state_s, cw, alr, dtr, gn, *, bb):
    nt, nb, _ = qkv_tm.shape
    nh, dk = GDN_HEADS, GDN_DK
    gw = nh * dk
    nr = 6 * nt
    full = lambda shape: pl.BlockSpec(shape, lambda *_: (0,) * len(shape))
    r_tm, intra = pl.pallas_call(
        functools.partial(_gdn_sample_pre_kernel, nt=nt),
        grid=(1,),
        in_specs=[full(qkv_tm.shape), full(hist_tm.shape), full(ba_tm.shape), full(cw.shape),
                  full(alr.shape), full(dtr.shape)],
        out_specs=(full((nr, nb, gw)), full((nt * nt, nb, gw))),
        out_shape=(jax.ShapeDtypeStruct((nr, nb, gw), F32), jax.ShapeDtypeStruct((nt * nt, nb, gw), F32)),
        compiler_params=_cparams(("arbitrary",)), name="gdn_sample_pre",
    )(qkv_tm, hist_tm, ba_tm, cw, alr, dtr)
    r_bm = jnp.transpose(r_tm.reshape(nr, nb, nh, dk), (1, 2, 0, 3))
    blk = lambda b: (b, 0, 0, 0)
    ws_bm, s_new = pl.pallas_call(
        functools.partial(_gdn_sample_state_kernel, nt=nt, bb=bb),
        grid=(nb // bb,),
        in_specs=[pl.BlockSpec((bb, nh, nr, dk), blk), pl.BlockSpec((bb, nh, dk, dk), blk)],
        out_specs=(pl.BlockSpec((bb, nh, 2 * nt, dk), blk), pl.BlockSpec((bb, nh, dk, dk), blk)),
        out_shape=(jax.ShapeDtypeStruct((nb, nh, 2 * nt, dk), F32), jax.ShapeDtypeStruct((nb, nh, dk, dk), F32)),
        compiler_params=_cparams(("parallel",)), name="gdn_sample_state",
    )(r_bm, state_s)
    ws_tm = jnp.transpose(ws_bm, (2, 0, 1, 3)).reshape(2 * nt, nb, gw)
    og = pl.pallas_call(
        functools.partial(_gdn_sample_post_kernel, nt=nt),
        grid=(1,),
        in_specs=[full(ws_tm.shape), full(r_tm.shape), full(intra.shape), full(z_tm.shape), full(gn.shape)],
        out_specs=full((nt, nb, gw)),
        out_shape=jax.ShapeDtypeStruct((nt, nb, gw), BF16),
        compiler_params=_cparams(("arbitrary",)), name="gdn_sample_post",
    )(ws_tm, r_tm, intra, z_tm, gn)
    return og, s_new


def _lambda(lq1_ref, lk1_ref, lq2_ref, lk2_ref, lam_init):
    s1 = jnp.sum(lq1_ref[...] * lk1_ref[...], axis=-1, keepdims=True)
    s2 = jnp.sum(lq2_ref[...] * lk2_ref[...], axis=-1, keepdims=True)
    return jnp.exp(s1) - jnp.exp(s2) + lam_init


def _subln(o, g_ref, lam_init):
    return o * lax.rsqrt(jnp.mean(o * o, axis=-1, keepdims=True) + EPS) * g_ref[...] * (1.0 - lam_init)


def _attn_prompt_kernel(q_ref, k_ref, vt_ref, lq1_ref, lk1_ref, lq2_ref, lk2_ref, sg_ref, o_ref,
                        sa_scr, sb_scr, m_scr, acc_scr, *, tq, lam_init):
    qi = pl.program_id(1)
    nh, dv, dqk = DIFF_HEADS, DIFF_DV, DIFF_DQK
    lam = _lambda(lq1_ref, lk1_ref, lq2_ref, lk2_ref, lam_init)
    lane = lax.broadcasted_iota(jnp.int32, (tq, dv), 1)
    kidx = lax.broadcasted_iota(jnp.int32, (tq, tq), 0)
    qidx = lax.broadcasted_iota(jnp.int32, (tq, tq), 1)
    causal = kidx <= qidx
    nsum = 2 * SUBLANES
    ones_rows = jnp.ones((nsum, tq), BF16)
    lanes = [slice(h * dv, (h + 1) * dv) for h in range(nh)]
    qs = []
    for h in range(nh):
        q = q_ref[:, lanes[h]]
        zero = jnp.zeros_like(q)
        qs.append((jnp.where(lane < dqk, q, zero), jnp.where(lane >= dqk, q, zero)))

    nvt = tq // vt_ref.shape[-1]
    nchain = 2 * nh

    def scores(dst, j):
        for h in range(nh):
            kb = k_ref[pl.ds(pl.multiple_of(j * tq, tq), tq), lanes[h]]
            for c in range(2):
                dst[2 * h + c] = _dot_nt(kb, qs[h][c])

    def softmax_pv(src, j, masked):
        ps, alphas = [], []
        for x in range(nchain):
            s_x = jnp.where(causal, src[x], NEG) if masked else src[x]
            m_i = m_scr[x]
            m_new = jnp.maximum(m_i, jnp.max(s_x, axis=0, keepdims=True))
            m_scr[x] = m_new
            alphas.append(jnp.exp2(m_i - m_new))
            ps.append(jnp.exp2(s_x - m_new).astype(BF16))
        for h in range(nh):
            vt = jnp.concatenate([vt_ref[j * nvt + t, lanes[h], :] for t in range(nvt)], axis=1)
            vt = jnp.concatenate([vt, ones_rows], axis=0)
            for c in range(2):
                x = 2 * h + c
                acc_scr[x] = alphas[x] * acc_scr[x] + _dot(vt, ps[x])

    m_scr[...] = jnp.full(m_scr.shape, NEG, F32)
    acc_scr[...] = jnp.zeros(acc_scr.shape, F32)
    scores(sa_scr, 0)

    def two_blocks(i, _):
        scores(sb_scr, 2 * i + 1)
        softmax_pv(sa_scr, 2 * i, False)
        scores(sa_scr, 2 * i + 2)
        softmax_pv(sb_scr, 2 * i + 1, False)
        return 0

    lax.fori_loop(0, lax.shift_right_logical(qi, 1), two_blocks, 0)
    odd = lax.rem(qi, 2) == 1

    @pl.when(odd)
    def _():
        scores(sb_scr, qi)
        softmax_pv(sa_scr, qi - 1, False)
        softmax_pv(sb_scr, qi, True)

    @pl.when(jnp.logical_not(odd))
    def _():
        softmax_pv(sa_scr, qi, True)

    for h in range(nh):
        ls = lanes[h]
        a1, a2 = acc_scr[2 * h], acc_scr[2 * h + 1]
        ot = a1[:dv] / a1[dv:dv + 1] - lam * (a2[:dv] / a2[dv:dv + 1])
        ot = ot * lax.rsqrt(jnp.mean(ot * ot, axis=0, keepdims=True) + EPS) * (1.0 - lam_init)
        o_ref[:, ls] = (jnp.transpose(ot) * sg_ref[...]).astype(o_ref.dtype)


def _attn_prompt(qa, ka, vt, lq1, lk1, lq2, lk2, sg, *, nb, seq, tq, lam_init):
    nh, dv = DIFF_HEADS, DIFF_DV
    wd = nh * dv
    nq = seq // tq
    qa3, ka3 = qa.reshape(nb, seq, wd), ka.reshape(nb, seq, wd)
    tv = vt.shape[-1]
    nv = seq // tv
    vt4 = vt.reshape(nb, nv, wd, tv)
    vec = _const_spec((1, DIFF_DQK))
    out = pl.pallas_call(
        functools.partial(_attn_prompt_kernel, tq=tq, lam_init=lam_init),
        grid=(nb, nq),
        in_specs=[pl.BlockSpec((None, tq, wd), lambda b, i: (b, i, 0)),
                  pl.BlockSpec((None, seq, wd), lambda b, i: (b, 0, 0)),
                  pl.BlockSpec((None, nv, wd, tv), lambda b, i: (b, 0, 0, 0)),
                  vec, vec, vec, vec, _const_spec((1, dv))],
        out_specs=pl.BlockSpec((None, tq, wd), lambda b, i: (b, i, 0)),
        out_shape=jax.ShapeDtypeStruct((nb, seq, wd), BF16),
        scratch_shapes=[pltpu.VMEM((2 * nh, tq, tq), F32), pltpu.VMEM((2 * nh, tq, tq), F32),
                        pltpu.VMEM((2 * nh, 1, tq), F32), pltpu.VMEM((2 * nh, dv + 2 * SUBLANES, tq), F32)],
        compiler_params=_cparams(("parallel", "arbitrary")), name="attn_prompt",
    )(qa3, ka3, vt4, lq1, lk1, lq2, lk2, sg)
    return out.reshape(nb * seq, wd)


def _attn_sample_kernel(pt_ref, q_ref, kn_ref, vn_ref, *rest, n_pages, nt, bs, lam_init):
    k_refs = rest[:bs * n_pages]
    v_refs = rest[bs * n_pages:2 * bs * n_pages]
    lq1_ref, lk1_ref, lq2_ref, lk2_ref, sg_ref, o_ref = rest[2 * bs * n_pages:]
    del pt_ref
    nh, dv, dqk = DIFF_HEADS, DIFF_DV, DIFF_DQK
    page = k_refs[0].shape[0] // nh
    nr = 2 * nt
    lam = _lambda(lq1_ref, lk1_ref, lq2_ref, lk2_ref, lam_init)
    row = lax.broadcasted_iota(jnp.int32, (nr, dv), 0)
    lane = lax.broadcasted_iota(jnp.int32, (nr, dv), 1)
    first = row < nt
    keep = jnp.logical_xor(lane >= dqk, first)
    tpos = jnp.where(first, row, row - nt)
    new_ok = (lane < nt) & (lane <= tpos)
    zpad = jnp.zeros((page - nr, dv), F32)
    lanes = [slice(h * dv, (h + 1) * dv) for h in range(nh)]
    head_rows = [pl.ds(h, page, stride=nh) for h in range(nh)]
    groups = [list(range(j, min(j + 2, n_pages))) for j in range(0, n_pages, 2)]

    chains = [(i, h) for i in range(bs) for h in range(nh)]

    def past(refs, i, h, grp):
        return jnp.concatenate([refs[i * n_pages + j][head_rows[h], :] for j in grp], axis=0).astype(BF16)

    s_all = {}
    for i, h in chains:
        qh = q_ref[i, :, lanes[h]]
        qz = jnp.where(keep, qh, jnp.zeros_like(qh))
        s_ih = [_dot_nt(qz, past(k_refs, i, h, grp)) for grp in groups]
        knew = jnp.concatenate([kn_ref[i, :, lanes[h]], zpad], axis=0)
        s_ih.append(jnp.where(new_ok, _dot_nt(qz, knew.astype(BF16)), NEG))
        s_all[i, h] = s_ih
    p_all, l_all = {}, {}
    for x in chains:
        m = jnp.max(s_all[x][0], axis=-1, keepdims=True)
        for s in s_all[x][1:]:
            m = jnp.maximum(m, jnp.max(s, axis=-1, keepdims=True))
        ps = [jnp.exp2(s - m) for s in s_all[x]]
        l = jnp.sum(ps[0], axis=-1, keepdims=True)
        for p in ps[1:]:
            l = l + jnp.sum(p, axis=-1, keepdims=True)
        p_all[x] = [p.astype(BF16) for p in ps]
        l_all[x] = l
    for i, h in chains:
        acc = _dot(p_all[i, h][-1], jnp.concatenate([vn_ref[i, :, lanes[h]], zpad], axis=0).astype(BF16))
        for gi, grp in enumerate(groups):
            acc = acc + _dot(p_all[i, h][gi], past(v_refs, i, h, grp))
        o2 = acc / l_all[i, h]
        o = o2 - lam * pltpu.roll(o2, nt, axis=0)
        o_ref[i, :, lanes[h]] = _subln(o, sg_ref, lam_init)


def _attn_sample(page_table, q2, kn8, vn8, cache_k, cache_v, lq1, lk1, lq2, lk2, sg, *, nt, bs, lam_init):
    nb, n_pages = page_table.shape
    nh, dv = DIFF_HEADS, DIFF_DV
    page = cache_k.shape[1]
    wd = nh * dv
    ck = cache_k.reshape(cache_k.shape[0], page * nh, dv)
    cv = cache_v.reshape(cache_v.shape[0], page * nh, dv)
    nr = 2 * nt
    small = pl.BlockSpec((bs, nr, wd), lambda b, pt: (b, 0, 0))

    def page_spec(i, j):
        return pl.BlockSpec((None, page * nh, dv), lambda b, pt: (pt[b * bs + i, j], 0, 0))

    pages = [page_spec(i, j) for i in range(bs) for j in range(n_pages)]
    vec = pl.BlockSpec((1, DIFF_DQK), lambda b, pt: (0, 0))
    grid_spec = pltpu.PrefetchScalarGridSpec(
        num_scalar_prefetch=1, grid=(nb // bs,),
        in_specs=[small, small, small] + pages * 2
        + [vec, vec, vec, vec, pl.BlockSpec((1, dv), lambda b, pt: (0, 0))],
        out_specs=small)
    return pl.pallas_call(
        functools.partial(_attn_sample_kernel, n_pages=n_pages, nt=nt, bs=bs, lam_init=lam_init),
        grid_spec=grid_spec, out_shape=jax.ShapeDtypeStruct((nb, nr, wd), F32),
        compiler_params=_cparams(("parallel",)), name="attn_sample",
    )(page_table, q2, kn8, vn8, *([ck] * (bs * n_pages)), *([cv] * (bs * n_pages)), lq1, lk1, lq2, lk2, sg)


def _post_kernel(x_ref, og_ref, od_ref, hist_ref, wo_ref, gf_ref, wu_ref, fcw_ref, fcb_ref, wd_ref,
                 y_ref, tail_ref, ubuf, *, tm, shift, hr, dff, ncol):
    i = pl.program_id(1)
    gw = og_ref.shape[1]

    @pl.when(i == 0)
    def _():
        ubuf[0:hr, :] = hist_ref[...]

    hres = x_ref[...] + _dot(og_ref[...], wo_ref[0:gw, :]) + _dot(od_ref[...], wo_ref[gw:, :])
    hn = (hres * lax.rsqrt(jnp.mean(hres * hres, axis=-1, keepdims=True) + EPS) * gf_ref[...]).astype(BF16)
    fcw = fcw_ref[...]
    fcb = fcb_ref[...]

    def conv(cols):
        ubuf[hr:hr + tm, cols] = _dot(hn, wu_ref[:, cols])
        u = fcw[FFN_CONV - 1:FFN_CONV, cols] * ubuf[hr:hr + tm, cols] + fcb[:, cols]
        for j in range(1, FFN_CONV):
            u = u + fcw[FFN_CONV - 1 - j:FFN_CONV - j, cols] * ubuf[hr - j * shift:hr - j * shift + tm, cols]
        return u

    y = hres
    wc = dff // ncol
    for c in range(ncol):
        gate = conv(slice(c * wc, (c + 1) * wc))
        val = conv(slice(dff + c * wc, dff + (c + 1) * wc))
        act = (_silu(gate) * val).astype(BF16)
        y = y + _dot(act, wd_ref[c * wc:(c + 1) * wc, :])
    y_ref[...] = y
    tail = ubuf[tm:tm + hr, :]
    tail_ref[...] = tail
    ubuf[0:hr, :] = tail


def _post(x2d, og, od, hist, w_out, g_ffn, w_up, fcw, fcb, w_down, *, nseq, tm, shift, hr, ncol):
    n, d = x2d.shape
    gw = og.shape[1]
    dff2 = w_up.shape[1]
    nt = n // (nseq * tm)
    row = lambda b, i: (b * nt + i, 0)
    in_specs = [
        pl.BlockSpec((tm, d), row), pl.BlockSpec((tm, gw), row), pl.BlockSpec((tm, od.shape[1]), row),
        pl.BlockSpec((None, hr, dff2), lambda b, i: (b, 0, 0)),
        _const_spec(w_out.shape), _const_spec((1, d)), _const_spec(w_up.shape),
        _const_spec((FFN_CONV, dff2)), _const_spec((1, dff2)), _const_spec(w_down.shape),
    ]
    out_shape = (jax.ShapeDtypeStruct((n, d), F32), jax.ShapeDtypeStruct((nseq, hr, dff2), F32))
    out_specs = (pl.BlockSpec((tm, d), row), pl.BlockSpec((None, hr, dff2), lambda b, i: (b, 0, 0)))
    return pl.pallas_call(
        functools.partial(_post_kernel, tm=tm, shift=shift, hr=hr, dff=dff2 // 2, ncol=ncol),
        grid=(nseq, nt), in_specs=in_specs, out_specs=out_specs, out_shape=out_shape,
        scratch_shapes=[pltpu.VMEM((hr + tm, dff2), F32)],
        compiler_params=_cparams(("parallel", "arbitrary")), name="post",
    )(x2d, og, od, hist, w_out, g_ffn, w_up, fcw, fcb, w_down)


def _lane_pad(vec, offset):
    out = jnp.zeros((1, LANES), F32)
    return lax.dynamic_update_slice(out, vec.reshape(1, -1).astype(F32), (0, offset))


def _layer(l, x_prompt, x_sample, state_gdn_conv, state_gdn_s, cache_k, cache_v, page_table, state_ffn_conv, wl):
    (attn_norm_g, w_in, gdn_conv_w, gdn_a_log, gdn_dt_bias, gdn_out_norm_g, diff_q_norm_g, diff_k_norm_g,
     lq1, lk1, lq2, lk2, diff_subln_g, w_out, ffn_norm_g, w_up, ffn_conv_w, ffn_conv_b, w_down) = wl
    nbp, seq, d = x_prompt.shape
    nbs, nts, _ = x_sample.shape
    nh, dk = GDN_HEADS, GDN_DK
    gw = nh * dk
    dw = DIFF_HEADS * DIFF_DV
    lam_init = 0.8 - 0.6 * math.exp(-0.3 * l)

    c_b = 4 * gw
    c_d = c_b + 2 * nh
    w_g = w_in[:, :c_b].astype(BF16)
    w_d = w_in[:, c_d:].astype(BF16)
    w_t = jnp.transpose(jnp.concatenate(
        [w_in[:, c_b:c_d], jnp.zeros((d, LANES - 2 * nh), w_in.dtype)], axis=1)).astype(BF16)
    reps = dw // DIFF_DQK
    gq_t = jnp.tile(diff_q_norm_g.reshape(1, -1), (1, reps))
    gk_t = jnp.tile(diff_k_norm_g.reshape(1, -1), (1, reps))
    g_attn = attn_norm_g.reshape(1, d)
    alr, dtr = _lane_pad(gdn_a_log, nh), _lane_pad(gdn_dt_bias, nh)
    alc = jnp.transpose(alr[:, :SUBLANES])
    dtc = jnp.transpose(dtr[:, :SUBLANES])
    gn = gdn_out_norm_g.reshape(1, dk)
    vecs = [v.reshape(1, -1) for v in (lq1, lk1, lq2, lk2)]
    sg = diff_subln_g.reshape(1, -1)
    w_out_b, w_up_b, w_down_b = w_out.astype(BF16), w_up.astype(BF16), w_down.astype(BF16)
    g_ffn = ffn_norm_g.reshape(1, d)
    fcb = ffn_conv_b.reshape(1, -1)
    dff2 = w_up.shape[1]

    xp2 = x_prompt.reshape(nbp * seq, d)
    (z, gcol, grow, qn, kn, kb, vb, tail_g, knew, vnew, qa, ka, vt) = _inproj_gdn(
        xp2, g_attn, w_g, w_d, w_t, gq_t, gk_t, gdn_conv_w, alr, dtr, alc, dtc,
        nb=nbp, tm=min(_Tiles.inproj_rows, seq), gw=gw, dw=dw)
    og, s_prompt = _gdn_prompt(qn, kn, kb, vb, gcol, grow, z, gn, nb=nbp, seq=seq,
                               tl=min(_Tiles.gdn_tokens, seq))
    od = _attn_prompt(qa, ka, vt, *vecs, sg, nb=nbp, seq=seq, tq=min(_Tiles.attn_tile, seq), lam_init=lam_init)
    hr_p = SUBLANES
    y_p, tail_p = _post(xp2, og, od, jnp.zeros((nbp, hr_p, dff2), F32), w_out_b, g_ffn, w_up_b,
                        ffn_conv_w, fcb, w_down_b, nseq=nbp, tm=min(_Tiles.post_rows, seq), shift=1, hr=hr_p,
                        ncol=_Tiles.post_col_groups)
    out_p = (y_p.reshape(nbp, seq, d),
             tail_g[:, SUBLANES - (GDN_CONV - 1):, :],
             s_prompt,
             knew.reshape(nbp, seq, DIFF_HEADS, DIFF_DV),
             vnew.reshape(nbp, seq, DIFF_HEADS, DIFF_DV),
             tail_p[:, hr_p - (FFN_CONV - 1):, :])

    xs2 = jnp.transpose(x_sample, (1, 0, 2)).reshape(nts * nbs, d)
    tm_s = min(_Tiles.sample_rows, nts * nbs)
    assert tm_s % nbs == 0 and nbs % _Tiles.paged_seqs == 0
    qkv, z, ba, knew, vnew, qa, _, _ = _inproj(xs2, g_attn, w_g, w_d, w_t, gq_t, gk_t, tm=tm_s, gw=gw, dw=dw)
    qkv_tm = qkv.reshape(nts, nbs, 3 * gw)
    hist_tm = jnp.transpose(state_gdn_conv, (1, 0, 2))
    og_tm, s_sample = _gdn_sample(qkv_tm, hist_tm, ba.reshape(nts, nbs, LANES), z.reshape(nts, nbs, gw),
                                  state_gdn_s, gdn_conv_w, alr, dtr, gn, bb=min(_Tiles.state_seqs, nbs))
    conv_all = jnp.concatenate([hist_tm, qkv_tm], axis=0)
    conv_s = jnp.transpose(conv_all[nts:], (1, 0, 2))
    to_bm = lambda a: jnp.transpose(a.reshape(nts, nbs, -1), (1, 0, 2))
    q_bm = to_bm(qa)
    q2 = jnp.concatenate([q_bm, q_bm], axis=1)
    pad = jnp.zeros((nbs, nts, dw), F32)
    kn_bm, vn_bm = to_bm(knew), to_bm(vnew)
    od_bm = _attn_sample(page_table, q2, jnp.concatenate([kn_bm, pad], axis=1),
                         jnp.concatenate([vn_bm, pad], axis=1), cache_k, cache_v, *vecs, sg,
                         nt=nts, bs=_Tiles.paged_seqs, lam_init=lam_init)
    od_tm = jnp.transpose(od_bm[:, :nts, :], (1, 0, 2)).reshape(nts * nbs, dw).astype(BF16)
    hr_s = (FFN_CONV - 1) * nbs
    hist_f = jnp.transpose(state_ffn_conv, (1, 0, 2)).reshape(1, hr_s, dff2)
    y_s, tail_s = _post(xs2, og_tm.reshape(nts * nbs, gw), od_tm, hist_f, w_out_b, g_ffn, w_up_b,
                        ffn_conv_w, fcb, w_down_b, nseq=1, tm=tm_s, shift=nbs, hr=hr_s,
                        ncol=_Tiles.post_col_groups)
    out_s = (jnp.transpose(y_s.reshape(nts, nbs, d), (1, 0, 2)),
             conv_s,
             s_sample,
             kn_bm.reshape(nbs, nts, DIFF_HEADS, DIFF_DV),
             vn_bm.reshape(nbs, nts, DIFF_HEADS, DIFF_DV),
             jnp.transpose(tail_s.reshape(FFN_CONV - 1, nbs, dff2), (1, 0, 2)))
    return out_p, out_s


def kernel(x_prompt, x_sample, state_gdn_conv, state_gdn_S, cache_k, cache_v, page_table, state_ffn_conv, attn_norm_g, w_in, gdn_conv_w, gdn_A_log, gdn_dt_bias, gdn_out_norm_g, diff_q_norm_g, diff_k_norm_g, diff_lambda_q1, diff_lambda_k1, diff_lambda_q2, diff_lambda_k2, diff_subln_g, w_out, ffn_norm_g, w_up, ffn_conv_w, ffn_conv_b, w_down):
    depth = w_in.shape[0]
    hp, hs = x_prompt, x_sample
    outs_p, outs_s = [], []
    for l in range(depth):
        wl = (attn_norm_g[l], w_in[l], gdn_conv_w[l], gdn_A_log[l], gdn_dt_bias[l], gdn_out_norm_g[l],
              diff_q_norm_g[l], diff_k_norm_g[l], diff_lambda_q1[l], diff_lambda_k1[l], diff_lambda_q2[l],
              diff_lambda_k2[l], diff_subln_g[l], w_out[l], ffn_norm_g[l], w_up[l], ffn_conv_w[l],
              ffn_conv_b[l], w_down[l])
        out_p, out_s = _layer(l, hp, hs, state_gdn_conv[l], state_gdn_S[l], cache_k[l], cache_v[l],
                              page_table, state_ffn_conv[l], wl)
        hp, hs = out_p[0], out_s[0]
        outs_p.append(out_p[1:])
        outs_s.append(out_s[1:])
    stack = lambda outs, i: jnp.stack([o[i] for o in outs])
    return (hp, hs) + tuple(stack(outs_p, i) for i in range(5)) + tuple(stack(outs_s, i) for i in range(5))
```

```python
import functools
import math

import jax
import jax.numpy as jnp
from jax import lax
from jax.experimental import pallas as pl
from jax.experimental.pallas import tpu as pltpu

F32 = jnp.float32
BF16 = jnp.bfloat16
EPS = 1e-6
NEG = -1e30

GDN_HEADS = 4
GDN_DK = 128
GDN_CONV = 4
GDN_CHUNK = 64
DIFF_HEADS = 4
DIFF_DV = 128
DIFF_DQK = 64
FFN_CONV = 3
LANES = 128
SUBLANES = 8
V7X_VMEM_BYTES = 64 * 1024 * 1024
VMEM_LIMIT = V7X_VMEM_BYTES - 8 * 1024 * 1024


class _Tiles:
    inproj_rows = 512
    gdn_tokens = 256
    attn_tile = 512
    post_rows = 512
    post_col_groups = 1
    sample_rows = 256
    state_seqs = 16
    paged_seqs = 2


def _cparams(sem):
    return pltpu.CompilerParams(dimension_semantics=sem, vmem_limit_bytes=VMEM_LIMIT)


def _const_spec(shape):
    nd = len(shape)
    return pl.BlockSpec(shape, lambda *_: (0,) * nd, pipeline_mode=pl.Buffered(1))


def _dot(a, b):
    return jnp.dot(a, b, preferred_element_type=F32)


def _dot_nt(a, b):
    return lax.dot_general(a, b, (((1,), (1,)), ((), ())), preferred_element_type=F32)


def _dot_tn(a, b):
    return lax.dot_general(a, b, (((0,), (0,)), ((), ())), preferred_element_type=F32)


def _softplus(x):
    return jnp.maximum(x, 0.0) + jnp.log1p(jnp.exp(-jnp.abs(x)))


def _silu(x):
    return x * jax.nn.sigmoid(x)


def _split3(x):
    hi = x.astype(BF16)
    r = x - hi.astype(F32)
    mid = r.astype(BF16)
    lo = (r - mid.astype(F32)).astype(BF16)
    return hi, mid, lo


def _inproj_core(x_ref, g_ref, wg_ref, wd_ref, wt_ref, gq_ref, gk_ref,
                 knew_ref, vnew_ref, qa_ref, ka_ref, vt_ref, gw, dw):
    tm = x_ref.shape[0]
    x = x_ref[...]
    xn = x * lax.rsqrt(jnp.mean(x * x, axis=-1, keepdims=True) + EPS) * g_ref[...]
    xb = xn.astype(BF16)
    proj = _dot(xb, wg_ref[...])
    pd = _dot(xb, wd_ref[...])
    tr = _dot_nt(wt_ref[...], xb)
    dq = pd[:, :dw]
    dk = pd[:, dw:2 * dw]
    dv = pd[:, 2 * dw:]
    vt_ref[...] = jnp.transpose(dv).astype(BF16)
    low =lax.broadcasted_iota(jnp.int32, (tm, DIFF_DV), 1) < DIFF_DQK

    def half_rms(t):
        out = []
        for h in range(DIFF_HEADS):
            th = t[:, h * DIFF_DV:(h + 1) * DIFF_DV]
            sq = th * th
            s1 = jnp.sum(jnp.where(low, sq, 0.0), axis=-1, keepdims=True)
            s2 = jnp.sum(jnp.where(low, 0.0, sq), axis=-1, keepdims=True)
            r1 = lax.rsqrt(s1 * (1.0 / DIFF_DQK) + EPS)
            r2 = lax.rsqrt(s2 * (1.0 / DIFF_DQK) + EPS)
            out.append(th * jnp.where(low, r1, r2))
        return jnp.concatenate(out, axis=1)

    dqn = half_rms(dq) * gq_ref[...]
    dkn = half_rms(dk) * gk_ref[...]
    for h in range(DIFF_HEADS):
        head_rows = pl.ds(h, tm, stride=DIFF_HEADS)
        knew_ref[head_rows, :] = dkn[:, h * DIFF_DV:(h + 1) * DIFF_DV]
        vnew_ref[head_rows, :] = dv[:, h * DIFF_DV:(h + 1) * DIFF_DV]
    qa_ref[...] = (dqn * (DIFF_DQK ** -0.5 * math.log2(math.e))).astype(BF16)
    ka_ref[...] = dkn.astype(BF16)
    return proj, tr


def _inproj_kernel(x_ref, g_ref, wg_ref, wd_ref, wt_ref, gq_ref, gk_ref,
                   qkv_ref, z_ref, ba_ref, knew_ref, vnew_ref, qa_ref, ka_ref, vt_ref, *, gw, dw):
    proj, trow = _inproj_core(x_ref, g_ref, wg_ref, wd_ref, wt_ref, gq_ref, gk_ref,
                              knew_ref, vnew_ref, qa_ref, ka_ref, vt_ref, gw, dw)
    qkv_ref[...] = proj[:, :3 * gw]
    z_ref[...] = proj[:, 3 * gw:]
    ba_ref[...] = jnp.transpose(trow)


def _inproj_gdn_kernel(x_ref, g_ref, wg_ref, wd_ref, wt_ref, gq_ref, gk_ref,
                       cw_ref, alr_ref, dtr_ref, alc_ref, dtc_ref,
                       z_ref, gcol_ref, grow_ref, qn_ref, kn_ref, kb_ref, vb_ref, tail_ref,
                       knew_ref, vnew_ref, qa_ref, ka_ref, vt_ref, xbuf, *, gw, dw):
    i = pl.program_id(1)
    tm = x_ref.shape[0]
    nh, dk = GDN_HEADS, GDN_DK
    hr = SUBLANES

    @pl.when(i == 0)
    def _():
        xbuf[0:hr, :] = jnp.zeros((hr, 3 * gw), F32)

    proj, trow = _inproj_core(x_ref, g_ref, wg_ref, wd_ref, wt_ref, gq_ref, gk_ref,
                              knew_ref, vnew_ref, qa_ref, ka_ref, vt_ref, gw, dw)
    z_ref[...] = proj[:, 3 * gw:]
    xq = proj[:, :3 * gw]
    prev = xbuf[...]
    tail = xq[tm - hr:, :]
    tail_ref[...] = tail
    xbuf[...] = tail
    cw = cw_ref[...]
    first8 = lax.broadcasted_iota(jnp.int32, (hr, 3 * gw), 0)
    y = cw[GDN_CONV - 1:GDN_CONV, :] * xq
    for j in range(1, GDN_CONV):
        sh = pltpu.roll(xq, j, axis=0)
        head = jnp.where(first8 < j, pltpu.roll(prev, j, axis=0), sh[:hr])
        y = y + cw[GDN_CONV - 1 - j:GDN_CONV - j, :] * jnp.concatenate([head, sh[hr:]], axis=0)
    y = _silu(y)
    ba = jnp.transpose(trow)
    beta_c = jax.nn.sigmoid(ba)
    gcol_ref[...] = -jnp.exp(alr_ref[...]) * _softplus(ba + dtr_ref[...])
    grow_ref[...] = -jnp.exp(alc_ref[...]) * _softplus(trow[:SUBLANES] + dtc_ref[...])
    for h in range(nh):
        ls = slice(h * dk, (h + 1) * dk)
        q = y[:, h * dk:(h + 1) * dk]
        k = y[:, gw + h * dk:gw + (h + 1) * dk]
        v = y[:, 2 * gw + h * dk:2 * gw + (h + 1) * dk]
        qn = q * lax.rsqrt(jnp.sum(q * q, axis=-1, keepdims=True) + EPS) * (dk ** -0.5)
        kn = k * lax.rsqrt(jnp.sum(k * k, axis=-1, keepdims=True) + EPS)
        beta = jnp.broadcast_to(beta_c[:, h:h + 1], (tm, dk))
        qn_ref[:, ls] = qn.astype(BF16)
        kn_ref[:, ls] = kn.astype(BF16)
        kb_ref[:, ls] = (kn * beta).astype(BF16)
        vb_ref[:, ls] = (v * beta).astype(BF16)


def _attn_out(n, tm, dw):
    return (
        jax.ShapeDtypeStruct((n * DIFF_HEADS, DIFF_DV), F32),
        jax.ShapeDtypeStruct((n * DIFF_HEADS, DIFF_DV), F32),
        jax.ShapeDtypeStruct((n, dw), BF16),
        jax.ShapeDtypeStruct((n, dw), BF16),
        jax.ShapeDtypeStruct((n // tm, dw, tm), BF16),
    )


def _inproj(x2d, g_attn, w_g, w_d, w_t, gq_t, gk_t, *, tm, gw, dw):
    n, d = x2d.shape
    row = lambda i: (i, 0)
    out_shape = (
        jax.ShapeDtypeStruct((n, 3 * gw), F32),
        jax.ShapeDtypeStruct((n, gw), F32),
        jax.ShapeDtypeStruct((n, LANES), F32),
    ) + _attn_out(n, tm, dw)
    out_specs = (
        pl.BlockSpec((tm, 3 * gw), row), pl.BlockSpec((tm, gw), row), pl.BlockSpec((tm, LANES), row),
        pl.BlockSpec((tm * DIFF_HEADS, DIFF_DV), row), pl.BlockSpec((tm * DIFF_HEADS, DIFF_DV), row),
        pl.BlockSpec((tm, dw), row), pl.BlockSpec((tm, dw), row),
        pl.BlockSpec((None, dw, tm), lambda i: (i, 0, 0)),
    )
    in_specs = [
        pl.BlockSpec((tm, d), row), _const_spec((1, d)), _const_spec(w_g.shape), _const_spec(w_d.shape),
        _const_spec(w_t.shape), _const_spec((1, dw)), _const_spec((1, dw)),
    ]
    return pl.pallas_call(
        functools.partial(_inproj_kernel, gw=gw, dw=dw),
        grid=(n // tm,), in_specs=in_specs, out_specs=out_specs, out_shape=out_shape,
        compiler_params=_cparams(("parallel",)), name="inproj",
    )(x2d, g_attn, w_g, w_d, w_t, gq_t, gk_t)


def _inproj_gdn(x2d, g_attn, w_g, w_d, w_t, gq_t, gk_t, cw, alr, dtr, alc, dtc, *, nb, tm, gw, dw):
    n, d = x2d.shape
    nt = n // (nb * tm)
    row = lambda b, i: (b * nt + i, 0)
    out_shape = (
        jax.ShapeDtypeStruct((n, gw), F32),
        jax.ShapeDtypeStruct((n, LANES), F32),
        jax.ShapeDtypeStruct((nb, SUBLANES, n // nb), F32),
        jax.ShapeDtypeStruct((n, gw), BF16),
        jax.ShapeDtypeStruct((n, gw), BF16),
        jax.ShapeDtypeStruct((n, gw), BF16),
        jax.ShapeDtypeStruct((n, gw), BF16),
        jax.ShapeDtypeStruct((nb, SUBLANES, 3 * gw), F32),
    ) + _attn_out(n, tm, dw)
    out_specs = (
        pl.BlockSpec((tm, gw), row), pl.BlockSpec((tm, LANES), row),
        pl.BlockSpec((None, SUBLANES, tm), lambda b, i: (b, 0, i)),
        pl.BlockSpec((tm, gw), row), pl.BlockSpec((tm, gw), row), pl.BlockSpec((tm, gw), row),
        pl.BlockSpec((tm, gw), row),
        pl.BlockSpec((None, SUBLANES, 3 * gw), lambda b, i: (b, 0, 0)),
        pl.BlockSpec((tm * DIFF_HEADS, DIFF_DV), row), pl.BlockSpec((tm * DIFF_HEADS, DIFF_DV), row),
        pl.BlockSpec((tm, dw), row), pl.BlockSpec((tm, dw), row),
        pl.BlockSpec((None, dw, tm), lambda b, i: (b * nt + i, 0, 0)),
    )
    in_specs = [
        pl.BlockSpec((tm, d), row), _const_spec((1, d)), _const_spec(w_g.shape), _const_spec(w_d.shape),
        _const_spec(w_t.shape), _const_spec((1, dw)), _const_spec((1, dw)),
        _const_spec(cw.shape), _const_spec((1, LANES)), _const_spec((1, LANES)),
        _const_spec((SUBLANES, 1)), _const_spec((SUBLANES, 1)),
    ]
    return pl.pallas_call(
        functools.partial(_inproj_gdn_kernel, gw=gw, dw=dw),
        grid=(nb, nt), in_specs=in_specs, out_specs=out_specs, out_shape=out_shape,
        scratch_shapes=[pltpu.VMEM((SUBLANES, 3 * gw), F32)],
        compiler_params=_cparams(("parallel", "arbitrary")), name="inproj_gdn",
    )(x2d, g_attn, w_g, w_d, w_t, gq_t, gk_t, cw, alr, dtr, alc, dtc)


def _gdn_prompt_kernel(qn_ref, kn_ref, kb_ref, vb_ref, gcol_ref, grow_ref, z_ref, gn_ref,
                       og_ref, sout_ref, s_scr, *, tl):
    i = pl.program_id(0)
    nb = qn_ref.shape[0]
    nh, dk, c = GDN_HEADS, GDN_DK, GDN_CHUNK
    gw = nh * dk

    @pl.when(i == 0)
    def _():
        s_scr[...] = jnp.zeros_like(s_scr)

    g_r = jnp.concatenate([grow_ref[b] for b in range(nb)], axis=0)

    ii = lax.broadcasted_iota(jnp.int32, (tl, tl), 0)
    jj = lax.broadcasted_iota(jnp.int32, (tl, tl), 1)
    same = lax.shift_right_logical(ii, 6) == lax.shift_right_logical(jj, 6)
    tri = jnp.where(same & (ii >= jj), 1.0, 0.0).astype(BF16)
    triu = jnp.where(same & (ii <= jj), 1.0, 0.0).astype(BF16)
    gc_c = [sum(_dot(tri, p) for p in _split3(gcol_ref[b])) for b in range(nb)]
    gc_r = sum(_dot(p, triu) for p in _split3(g_r))

    pr = 2 * c
    pi = lax.broadcasted_iota(jnp.int32, (pr, pr), 0)
    pj = lax.broadcasted_iota(jnp.int32, (pr, pr), 1)
    psame = lax.shift_right_logical(pi, 6) == lax.shift_right_logical(pj, 6)
    lower = psame & (pi >= pj)
    strict = psame & (pi > pj)
    eye = jnp.where(pi == pj, 1.0, 0.0).astype(F32)
    zeros_c = jnp.zeros((c, dk), F32)

    npair = tl // pr
    seqs = [(b, h) for b in range(nb) for h in range(nh)]

    def tile(ref, b, h, p):
        return ref[b, p * pr:(p + 1) * pr, h * dk:(h + 1) * dk]

    gcp, kpb, sol, qk, qg = {}, {}, {}, {}, {}

    def prepare(p):
        hp = [(b, h, p) for b, h in seqs]
        for b, h, _ in hp:
            gcp[b, h, p] = jnp.broadcast_to(gc_c[b][p * pr:(p + 1) * pr, nh + h:nh + h + 1], (pr, dk))
        dec = {(b, h, p): jnp.exp(jnp.where(
            lower, gcp[b, h, p] - gc_r[b * SUBLANES + nh + h:b * SUBLANES + nh + h + 1, p * pr:(p + 1) * pr], NEG))
            for b, h, _ in hp}
        kbp = {x: tile(kb_ref, *x) for x in hp}
        for x in hp:
            kpb[x] = tile(kn_ref, *x)
        m = {x: -jnp.where(strict, _dot_nt(kbp[x], kpb[x]) * dec[x], 0.0) for x in hp}
        a = {x: eye + m[x] for x in hp}
        pw = {x: _dot(m[x].astype(BF16), m[x].astype(BF16)) for x in hp}
        for it in range(1, 6):
            for x in hp:
                pwb = pw[x].astype(BF16)
                if it < 5:
                    res = _dot(pwb, jnp.concatenate([a[x].astype(BF16), pwb], axis=1))
                    a[x] = a[x] + res[:, :pr]
                    pw[x] = res[:, pr:]
                else:
                    a[x] = a[x] + _dot(pwb, a[x].astype(BF16))
        egc = {x: jnp.exp(gcp[x]) for x in hp}
        for x in hp:
            sol[x] = _dot(a[x].astype(BF16), jnp.concatenate(
                [tile(vb_ref, *x), (kbp[x].astype(F32) * egc[x]).astype(BF16)], axis=1))
            qk[x] = (_dot_nt(tile(qn_ref, *x), kpb[x]) * dec[x]).astype(BF16)
            qg[x] = tile(qn_ref, *x).astype(F32) * egc[x]

    s_cur = {x: s_scr[x] for x in seqs}
    o_rows = {x: [] for x in seqs}
    for p in range(npair):
        prepare(p)
        for cc in range(2):
            c0 = cc * c
            ws = {}
            for b, h in seqs:
                wq = jnp.concatenate([sol[b, h, p][c0:c0 + c, dk:], qg[b, h, p][c0:c0 + c]], axis=0)
                ws[b, h] = _dot(wq.astype(BF16), s_cur[b, h].astype(BF16))
            for b, h in seqs:
                g_c0 = gcp[b, h, p][c0:c0 + c]
                glast = g_c0[c - 1:c, :]
                kd = kpb[b, h, p][c0:c0 + c].astype(F32) * jnp.exp(glast - g_c0)
                v_new = (sol[b, h, p][c0:c0 + c, :dk] - ws[b, h][:c]).astype(BF16)
                s_cur[b, h] = s_cur[b, h] * jnp.exp(glast) + _dot_tn(kd.astype(BF16), v_new)
                zc = jnp.zeros_like(v_new)
                vn_pad = jnp.concatenate([v_new, zc] if cc == 0 else [zc, v_new], axis=0)
                o_rows[b, h].append(ws[b, h][c:] + _dot(qk[b, h, p][c0:c0 + c], vn_pad))
    for b, h in seqs:
        s_scr[b, h] = s_cur[b, h]
        o = jnp.concatenate(o_rows[b, h], axis=0)
        o = o * lax.rsqrt(jnp.mean(o * o, axis=-1, keepdims=True) + EPS) * gn_ref[...]
        og_ref[b, :, h * dk:(h + 1) * dk] = (o * _silu(z_ref[b, :, h * dk:(h + 1) * dk])).astype(og_ref.dtype)

    @pl.when(i == pl.num_programs(0) - 1)
    def _():
        sout_ref[...] = s_scr[...]


def _gdn_prompt(qn, kn, kb, vb, gcol, grow, z, gn, *, nb, seq, tl):
    nh, dk = GDN_HEADS, GDN_DK
    gw = nh * dk
    nt = seq // tl
    tok = lambda i: (0, i, 0)
    wide = pl.BlockSpec((nb, tl, gw), tok)
    per_seq = lambda a: a.reshape(nb, seq, a.shape[-1])
    in_specs = [
        wide, wide, wide, wide, pl.BlockSpec((nb, tl, LANES), tok),
        pl.BlockSpec((nb, SUBLANES, tl), lambda i: (0, 0, i)),
        wide, _const_spec((1, dk)),
    ]
    out_shape = (jax.ShapeDtypeStruct((nb, seq, gw), BF16),
                 jax.ShapeDtypeStruct((nb, nh, dk, dk), F32))
    out_specs = (wide, pl.BlockSpec((nb, nh, dk, dk), lambda i: (0, 0, 0, 0)))
    og, s_out = pl.pallas_call(
        functools.partial(_gdn_prompt_kernel, tl=tl),
        grid=(nt,), in_specs=in_specs, out_specs=out_specs, out_shape=out_shape,
        scratch_shapes=[pltpu.VMEM((nb, nh, dk, dk), F32)],
        compiler_params=_cparams(("arbitrary",)), name="gdn_prompt",
    )(per_seq(qn), per_seq(kn), per_seq(kb), per_seq(vb), per_seq(gcol), grow, per_seq(z), gn)
    return og.reshape(nb * seq, gw), s_out


def _gdn_sample_pre_kernel(qkv_ref, hist_ref, ba_ref, cw_ref, alr_ref, dtr_ref, r_ref, intra_ref, *, nt):
    nh, dk = GDN_HEADS, GDN_DK
    gw = nh * dk
    nb = qkv_ref.shape[1]
    cw = cw_ref[...]
    nhist = GDN_CONV - 1
    xp = [hist_ref[j] for j in range(nhist)] + [qkv_ref[t] for t in range(nt)]
    ys = []
    for t in range(nt):
        y = cw[0:1, :] * xp[t]
        for j in range(1, GDN_CONV):
            y = y + cw[j:j + 1, :] * xp[t + j]
        ys.append(_silu(y))
    beta, gc = [], []
    for t in range(nt):
        ba = ba_ref[t]
        beta_c = jax.nn.sigmoid(ba)
        g_c = -jnp.exp(alr_ref[...]) * _softplus(ba + dtr_ref[...])
        beta.append([jnp.broadcast_to(beta_c[:, h:h + 1], (nb, dk)) for h in range(nh)])
        g_t = [jnp.broadcast_to(g_c[:, nh + h:nh + h + 1], (nb, dk)) for h in range(nh)]
        gc.append(g_t if t == 0 else [gc[t - 1][h] + g_t[h] for h in range(nh)])
    zrow = jnp.zeros((nb, dk), F32)
    for h in range(nh):
        q, k, v = [], [], []
        for t in range(nt):
            qt = ys[t][:, h * dk:(h + 1) * dk]
            kt = ys[t][:, gw + h * dk:gw + (h + 1) * dk]
            q.append(qt * lax.rsqrt(jnp.sum(qt * qt, axis=-1, keepdims=True) + EPS) * (dk ** -0.5))
            k.append(kt * lax.rsqrt(jnp.sum(kt * kt, axis=-1, keepdims=True) + EPS))
            v.append(ys[t][:, 2 * gw + h * dk:2 * gw + (h + 1) * dk])
        g = [gc[t][h] for t in range(nt)]
        us, ws = [], []
        for t in range(nt):
            b_t = beta[t][h]
            u_t = v[t] * b_t
            w_t = k[t] * b_t * jnp.exp(g[t])
            for s in range(t):
                m_ts = b_t * jnp.sum(k[t] * k[s], axis=-1, keepdims=True) * jnp.exp(g[t] - g[s])
                u_t = u_t - m_ts * us[s]
                w_t = w_t - m_ts * ws[s]
            us.append(u_t)
            ws.append(w_t)
        lane = slice(h * dk, (h + 1) * dk)
        for t in range(nt):
            r_ref[t, :, lane] = ws[t]
            r_ref[nt + t, :, lane] = q[t] * jnp.exp(g[t])
            r_ref[2 * nt + t, :, lane] = us[t]
            r_ref[3 * nt + t, :, lane] = k[t] * jnp.exp(g[nt - 1] - g[t])
            for s in range(nt):
                if s <= t:
                    intra_ref[t * nt + s, :, lane] = (jnp.sum(q[t] * k[s], axis=-1, keepdims=True)
                                                       * jnp.exp(g[t] - g[s]))
                else:
                    intra_ref[t * nt + s, :, lane] = zrow
        r_ref[4 * nt, :, lane] = jnp.exp(g[nt - 1])
        for r in range(4 * nt + 1, r_ref.shape[0]):
            r_ref[r, :, lane] = zrow


def _gdn_sample_state_kernel(r_ref, s_ref, ws_ref, sout_ref, *, nt, bb):
    nh = GDN_HEADS
    rows = lax.broadcasted_iota(jnp.int32, (2 * nt, GDN_DK), 0)
    inst = [(bi, h) for bi in range(bb) for h in range(nh)]
    ws = {x: _dot(r_ref[x[0], x[1], 0:2 * nt, :].astype(BF16), s_ref[x].astype(BF16)) for x in inst}
    for x in inst:
        ud = r_ref[x[0], x[1], 2 * nt:4 * nt, :]
        vn = jnp.where(rows < nt, ud - ws[x], 0.0)
        kd = jnp.where(rows < nt, pltpu.roll(ud, nt, axis=0), 0.0)
        ws_ref[x] = ws[x]
        sout_ref[x] = (s_ref[x] * r_ref[x[0], x[1], 4 * nt:4 * nt + 1, :]
                       + _dot_tn(kd.astype(BF16), vn.astype(BF16)))


def _gdn_sample_post_kernel(ws_ref, r_ref, intra_ref, z_ref, gn_ref, og_ref, *, nt):
    nh, dk = GDN_HEADS, GDN_DK
    for h in range(nh):
        lane = slice(h * dk, (h + 1) * dk)
        vn = [r_ref[2 * nt + t, :, lane] - ws_ref[t, :, lane] for t in range(nt)]
        for t in range(nt):
            o = ws_ref[nt + t, :, lane]
            for s in range(t + 1):
                o = o + intra_ref[t * nt + s, :, lane] * vn[s]
            o = o * lax.rsqrt(jnp.mean(o * o, axis=-1, keepdims=True) + EPS) * gn_ref[...]
            og_ref[t, :, lane] = (o * _silu(z_ref[t, :, lane])).astype(og_ref.dtype)


def _gdn_sample(qkv_tm, hist_tm, ba_tm, z_tm, state_s, cw, alr, dtr, gn, *, bb):
    nt, nb, _ = qkv_tm.shape
    nh, dk = GDN_HEADS, GDN_DK
    gw = nh * dk
    nr = 6 * nt
    full = lambda shape: pl.BlockSpec(shape, lambda *_: (0,) * len(shape))
    r_tm, intra = pl.pallas_call(
        functools.partial(_gdn_sample_pre_kernel, nt=nt),
        grid=(1,),
        in_specs=[full(qkv_tm.shape), full(hist_tm.shape), full(ba_tm.shape), full(cw.shape),
                  full(alr.shape), full(dtr.shape)],
        out_specs=(full((nr, nb, gw)), full((nt * nt, nb, gw))),
        out_shape=(jax.ShapeDtypeStruct((nr, nb, gw), F32), jax.ShapeDtypeStruct((nt * nt, nb, gw), F32)),
        compiler_params=_cparams(("arbitrary",)), name="gdn_sample_pre",
    )(qkv_tm, hist_tm, ba_tm, cw, alr, dtr)
    r_bm = jnp.transpose(r_tm.reshape(nr, nb, nh, dk), (1, 2, 0, 3))
    blk = lambda b: (b, 0, 0, 0)
    ws_bm, s_new = pl.pallas_call(
        functools.partial(_gdn_sample_state_kernel, nt=nt, bb=bb),
        grid=(nb // bb,),
        in_specs=[pl.BlockSpec((bb, nh, nr, dk), blk), pl.BlockSpec((bb, nh, dk, dk), blk)],
        out_specs=(pl.BlockSpec((bb, nh, 2 * nt, dk), blk), pl.BlockSpec((bb, nh, dk, dk), blk)),
        out_shape=(jax.ShapeDtypeStruct((nb, nh, 2 * nt, dk), F32), jax.ShapeDtypeStruct((nb, nh, dk, dk), F32)),
        compiler_params=_cparams(("parallel",)), name="gdn_sample_state",
    )(r_bm, state_s)
    ws_tm = jnp.transpose(ws_bm, (2, 0, 1, 3)).reshape(2 * nt, nb, gw)
    og = pl.pallas_call(
        functools.partial(_gdn_sample_post_kernel, nt=nt),
        grid=(1,),
        in_specs=[full(ws_tm.shape), full(r_tm.shape), full(intra.shape), full(z_tm.shape), full(gn.shape)],
        out_specs=full((nt, nb, gw)),
        out_shape=jax.ShapeDtypeStruct((nt, nb, gw), BF16),
        compiler_params=_cparams(("arbitrary",)), name="gdn_sample_post",
    )(ws_tm, r_tm, intra, z_tm, gn)
    return og, s_new


def _lambda(lq1_ref, lk1_ref, lq2_ref, lk2_ref, lam_init):
    s1 = jnp.sum(lq1_ref[...] * lk1_ref[...], axis=-1, keepdims=True)
    s2 = jnp.sum(lq2_ref[...] * lk2_ref[...], axis=-1, keepdims=True)
    return jnp.exp(s1) - jnp.exp(s2) + lam_init


def _subln(o, g_ref, lam_init):
    return o * lax.rsqrt(jnp.mean(o * o, axis=-1, keepdims=True) + EPS) * g_ref[...] * (1.0 - lam_init)


def _attn_prompt_kernel(q_ref, k_ref, vt_ref, lq1_ref, lk1_ref, lq2_ref, lk2_ref, sg_ref, o_ref,
                        sa_scr, sb_scr, m_scr, acc_scr, *, tq, lam_init):
    qi = pl.program_id(1)
    nh, dv, dqk = DIFF_HEADS, DIFF_DV, DIFF_DQK
    lam = _lambda(lq1_ref, lk1_ref, lq2_ref, lk2_ref, lam_init)
    lane = lax.broadcasted_iota(jnp.int32, (tq, dv), 1)
    kidx = lax.broadcasted_iota(jnp.int32, (tq, tq), 0)
    qidx = lax.broadcasted_iota(jnp.int32, (tq, tq), 1)
    causal = kidx <= qidx
    nsum = 2 * SUBLANES
    ones_rows = jnp.ones((nsum, tq), BF16)
    lanes = [slice(h * dv, (h + 1) * dv) for h in range(nh)]
    qs = []
    for h in range(nh):
        q = q_ref[:, lanes[h]]
        zero = jnp.zeros_like(q)
        qs.append((jnp.where(lane < dqk, q, zero), jnp.where(lane >= dqk, q, zero)))

    nvt = tq // vt_ref.shape[-1]
    nchain = 2 * nh

    def scores(dst, j):
        for h in range(nh):
            kb = k_ref[pl.ds(pl.multiple_of(j * tq, tq), tq), lanes[h]]
            for c in range(2):
                dst[2 * h + c] = _dot_nt(kb, qs[h][c])

    def softmax_pv(src, j, masked):
        ps, alphas = [], []
        for x in range(nchain):
            s_x = jnp.where(causal, src[x], NEG) if masked else src[x]
            m_i = m_scr[x]
            m_new = jnp.maximum(m_i, jnp.max(s_x, axis=0, keepdims=True))
            m_scr[x] = m_new
            alphas.append(jnp.exp2(m_i - m_new))
            ps.append(jnp.exp2(s_x - m_new).astype(BF16))
        for h in range(nh):
            vt = jnp.concatenate([vt_ref[j * nvt + t, lanes[h], :] for t in range(nvt)], axis=1)
            vt = jnp.concatenate([vt, ones_rows], axis=0)
            for c in range(2):
                x = 2 * h + c
                acc_scr[x] = alphas[x] * acc_scr[x] + _dot(vt, ps[x])

    m_scr[...] = jnp.full(m_scr.shape, NEG, F32)
    acc_scr[...] = jnp.zeros(acc_scr.shape, F32)
    scores(sa_scr, 0)

    def two_blocks(i, _):
        scores(sb_scr, 2 * i + 1)
        softmax_pv(sa_scr, 2 * i, False)
        scores(sa_scr, 2 * i + 2)
        softmax_pv(sb_scr, 2 * i + 1, False)
        return 0

    lax.fori_loop(0, lax.shift_right_logical(qi, 1), two_blocks, 0)
    odd = lax.rem(qi, 2) == 1

    @pl.when(odd)
    def _():
        scores(sb_scr, qi)
        softmax_pv(sa_scr, qi - 1, False)
        softmax_pv(sb_scr, qi, True)

    @pl.when(jnp.logical_not(odd))
    def _():
        softmax_pv(sa_scr, qi, True)

    for h in range(nh):
        ls = lanes[h]
        a1, a2 = acc_scr[2 * h], acc_scr[2 * h + 1]
        ot = a1[:dv] / a1[dv:dv + 1] - lam * (a2[:dv] / a2[dv:dv + 1])
        ot = ot * lax.rsqrt(jnp.mean(ot * ot, axis=0, keepdims=True) + EPS) * (1.0 - lam_init)
        o_ref[:, ls] = (jnp.transpose(ot) * sg_ref[...]).astype(o_ref.dtype)


def _attn_prompt(qa, ka, vt, lq1, lk1, lq2, lk2, sg, *, nb, seq, tq, lam_init):
    nh, dv = DIFF_HEADS, DIFF_DV
    wd = nh * dv
    nq = seq // tq
    qa3, ka3 = qa.reshape(nb, seq, wd), ka.reshape(nb, seq, wd)
    tv = vt.shape[-1]
    nv = seq // tv
    vt4 = vt.reshape(nb, nv, wd, tv)
    vec = _const_spec((1, DIFF_DQK))
    out = pl.pallas_call(
        functools.partial(_attn_prompt_kernel, tq=tq, lam_init=lam_init),
        grid=(nb, nq),
        in_specs=[pl.BlockSpec((None, tq, wd), lambda b, i: (b, i, 0)),
                  pl.BlockSpec((None, seq, wd), lambda b, i: (b, 0, 0)),
                  pl.BlockSpec((None, nv, wd, tv), lambda b, i: (b, 0, 0, 0)),
                  vec, vec, vec, vec, _const_spec((1, dv))],
        out_specs=pl.BlockSpec((None, tq, wd), lambda b, i: (b, i, 0)),
        out_shape=jax.ShapeDtypeStruct((nb, seq, wd), BF16),
        scratch_shapes=[pltpu.VMEM((2 * nh, tq, tq), F32), pltpu.VMEM((2 * nh, tq, tq), F32),
                        pltpu.VMEM((2 * nh, 1, tq), F32), pltpu.VMEM((2 * nh, dv + 2 * SUBLANES, tq), F32)],
        compiler_params=_cparams(("parallel", "arbitrary")), name="attn_prompt",
    )(qa3, ka3, vt4, lq1, lk1, lq2, lk2, sg)
    return out.reshape(nb * seq, wd)


def _attn_sample_kernel(pt_ref, q_ref, kn_ref, vn_ref, *rest, n_pages, nt, bs, lam_init):
    k_refs = rest[:bs * n_pages]
    v_refs = rest[bs * n_pages:2 * bs * n_pages]
    lq1_ref, lk1_ref, lq2_ref, lk2_ref, sg_ref, o_ref = rest[2 * bs * n_pages:]
    del pt_ref
    nh, dv, dqk = DIFF_HEADS, DIFF_DV, DIFF_DQK
    page = k_refs[0].shape[0] // nh
    nr = 2 * nt
    lam = _lambda(lq1_ref, lk1_ref, lq2_ref, lk2_ref, lam_init)
    row = lax.broadcasted_iota(jnp.int32, (nr, dv), 0)
    lane = lax.broadcasted_iota(jnp.int32, (nr, dv), 1)
    first = row < nt
    keep = jnp.logical_xor(lane >= dqk, first)
    tpos = jnp.where(first, row, row - nt)
    new_ok = (lane < nt) & (lane <= tpos)
    zpad = jnp.zeros((page - nr, dv), F32)
    lanes = [slice(h * dv, (h + 1) * dv) for h in range(nh)]
    head_rows = [pl.ds(h, page, stride=nh) for h in range(nh)]
    groups = [list(range(j, min(j + 2, n_pages))) for j in range(0, n_pages, 2)]

    chains = [(i, h) for i in range(bs) for h in range(nh)]

    def past(refs, i, h, grp):
        return jnp.concatenate([refs[i * n_pages + j][head_rows[h], :] for j in grp], axis=0).astype(BF16)

    s_all = {}
    for i, h in chains:
        qh = q_ref[i, :, lanes[h]]
        qz = jnp.where(keep, qh, jnp.zeros_like(qh))
        s_ih = [_dot_nt(qz, past(k_refs, i, h, grp)) for grp in groups]
        knew = jnp.concatenate([kn_ref[i, :, lanes[h]], zpad], axis=0)
        s_ih.append(jnp.where(new_ok, _dot_nt(qz, knew.astype(BF16)), NEG))
        s_all[i, h] = s_ih
    p_all, l_all = {}, {}
    for x in chains:
        m = jnp.max(s_all[x][0], axis=-1, keepdims=True)
        for s in s_all[x][1:]:
            m = jnp.maximum(m, jnp.max(s, axis=-1, keepdims=True))
        ps = [jnp.exp2(s - m) for s in s_all[x]]
        l = jnp.sum(ps[0], axis=-1, keepdims=True)
        for p in ps[1:]:
            l = l + jnp.sum(p, axis=-1, keepdims=True)
        p_all[x] = [p.astype(BF16) for p in ps]
        l_all[x] = l
    for i, h in chains:
        acc = _dot(p_all[i, h][-1], jnp.concatenate([vn_ref[i, :, lanes[h]], zpad], axis=0).astype(BF16))
        for gi, grp in enumerate(groups):
            acc = acc + _dot(p_all[i, h][gi], past(v_refs, i, h, grp))
        o2 = acc / l_all[i, h]
        o = o2 - lam * pltpu.roll(o2, nt, axis=0)
        o_ref[i, :, lanes[h]] = _subln(o, sg_ref, lam_init)


def _attn_sample(page_table, q2, kn8, vn8, cache_k, cache_v, lq1, lk1, lq2, lk2, sg, *, nt, bs, lam_init):
    nb, n_pages = page_table.shape
    nh, dv = DIFF_HEADS, DIFF_DV
    page = cache_k.shape[1]
    wd = nh * dv
    ck = cache_k.reshape(cache_k.shape[0], page * nh, dv)
    cv = cache_v.reshape(cache_v.shape[0], page * nh, dv)
    nr = 2 * nt
    small = pl.BlockSpec((bs, nr, wd), lambda b, pt: (b, 0, 0))

    def page_spec(i, j):
        return pl.BlockSpec((None, page * nh, dv), lambda b, pt: (pt[b * bs + i, j], 0, 0))

    pages = [page_spec(i, j) for i in range(bs) for j in range(n_pages)]
    vec = pl.BlockSpec((1, DIFF_DQK), lambda b, pt: (0, 0))
    grid_spec = pltpu.PrefetchScalarGridSpec(
        num_scalar_prefetch=1, grid=(nb // bs,),
        in_specs=[small, small, small] + pages * 2
        + [vec, vec, vec, vec, pl.BlockSpec((1, dv), lambda b, pt: (0, 0))],
        out_specs=small)
    return pl.pallas_call(
        functools.partial(_attn_sample_kernel, n_pages=n_pages, nt=nt, bs=bs, lam_init=lam_init),
        grid_spec=grid_spec, out_shape=jax.ShapeDtypeStruct((nb, nr, wd), F32),
        compiler_params=_cparams(("parallel",)), name="attn_sample",
    )(page_table, q2, kn8, vn8, *([ck] * (bs * n_pages)), *([cv] * (bs * n_pages)), lq1, lk1, lq2, lk2, sg)


def _post_kernel(x_ref, og_ref, od_ref, hist_ref, wo_ref, gf_ref, wu_ref, fcw_ref, fcb_ref, wd_ref,
                 y_ref, tail_ref, ubuf, *, tm, shift, hr, dff, ncol):
    i = pl.program_id(1)
    gw = og_ref.shape[1]

    @pl.when(i == 0)
    def _():
        ubuf[0:hr, :] = hist_ref[...]

    hres = x_ref[...] + _dot(og_ref[...], wo_ref[0:gw, :]) + _dot(od_ref[...], wo_ref[gw:, :])
    hn = (hres * lax.rsqrt(jnp.mean(hres * hres, axis=-1, keepdims=True) + EPS) * gf_ref[...]).astype(BF16)
    fcw = fcw_ref[...]
    fcb = fcb_ref[...]

    def conv(cols):
        ubuf[hr:hr + tm, cols] = _dot(hn, wu_ref[:, cols])
        u = fcw[FFN_CONV - 1:FFN_CONV, cols] * ubuf[hr:hr + tm, cols] + fcb[:, cols]
        for j in range(1, FFN_CONV):
            u = u + fcw[FFN_CONV - 1 - j:FFN_CONV - j, cols] * ubuf[hr - j * shift:hr - j * shift + tm, cols]
        return u

    y = hres
    wc = dff // ncol
    for c in range(ncol):
        gate = conv(slice(c * wc, (c + 1) * wc))
        val = conv(slice(dff + c * wc, dff + (c + 1) * wc))
        act = (_silu(gate) * val).astype(BF16)
        y = y + _dot(act, wd_ref[c * wc:(c + 1) * wc, :])
    y_ref[...] = y
    tail = ubuf[tm:tm + hr, :]
    tail_ref[...] = tail
    ubuf[0:hr, :] = tail


def _post(x2d, og, od, hist, w_out, g_ffn, w_up, fcw, fcb, w_down, *, nseq, tm, shift, hr, ncol):
    n, d = x2d.shape
    gw = og.shape[1]
    dff2 = w_up.shape[1]
    nt = n // (nseq * tm)
    row = lambda b, i: (b * nt + i, 0)
    in_specs = [
        pl.BlockSpec((tm, d), row), pl.BlockSpec((tm, gw), row), pl.BlockSpec((tm, od.shape[1]), row),
        pl.BlockSpec((None, hr, dff2), lambda b, i: (b, 0, 0)),
        _const_spec(w_out.shape), _const_spec((1, d)), _const_spec(w_up.shape),
        _const_spec((FFN_CONV, dff2)), _const_spec((1, dff2)), _const_spec(w_down.shape),
    ]
    out_shape = (jax.ShapeDtypeStruct((n, d), F32), jax.ShapeDtypeStruct((nseq, hr, dff2), F32))
    out_specs = (pl.BlockSpec((tm, d), row), pl.BlockSpec((None, hr, dff2), lambda b, i: (b, 0, 0)))
    return pl.pallas_call(
        functools.partial(_post_kernel, tm=tm, shift=shift, hr=hr, dff=dff2 // 2, ncol=ncol),
        grid=(nseq, nt), in_specs=in_specs, out_specs=out_specs, out_shape=out_shape,
        scratch_shapes=[pltpu.VMEM((hr + tm, dff2), F32)],
        compiler_params=_cparams(("parallel", "arbitrary")), name="post",
    )(x2d, og, od, hist, w_out, g_ffn, w_up, fcw, fcb, w_down)


def _lane_pad(vec, offset):
    out = jnp.zeros((1, LANES), F32)
    return lax.dynamic_update_slice(out, vec.reshape(1, -1).astype(F32), (0, offset))


def _layer(l, x_prompt, x_sample, state_gdn_conv, state_gdn_s, cache_k, cache_v, page_table, state_ffn_conv, wl):
    (attn_norm_g, w_in, gdn_conv_w, gdn_a_log, gdn_dt_bias, gdn_out_norm_g, diff_q_norm_g, diff_k_norm_g,
     lq1, lk1, lq2, lk2, diff_subln_g, w_out, ffn_norm_g, w_up, ffn_conv_w, ffn_conv_b, w_down) = wl
    nbp, seq, d = x_prompt.shape
    nbs, nts, _ = x_sample.shape
    nh, dk = GDN_HEADS, GDN_DK
    gw = nh * dk
    dw = DIFF_HEADS * DIFF_DV
    lam_init = 0.8 - 0.6 * math.exp(-0.3 * l)

    c_b = 4 * gw
    c_d = c_b + 2 * nh
    w_g = w_in[:, :c_b].astype(BF16)
    w_d = w_in[:, c_d:].astype(BF16)
    w_t = jnp.transpose(jnp.concatenate(
        [w_in[:, c_b:c_d], jnp.zeros((d, LANES - 2 * nh), w_in.dtype)], axis=1)).astype(BF16)
    reps = dw // DIFF_DQK
    gq_t = jnp.tile(diff_q_norm_g.reshape(1, -1), (1, reps))
    gk_t = jnp.tile(diff_k_norm_g.reshape(1, -1), (1, reps))
    g_attn = attn_norm_g.reshape(1, d)
    alr, dtr = _lane_pad(gdn_a_log, nh), _lane_pad(gdn_dt_bias, nh)
    alc = jnp.transpose(alr[:, :SUBLANES])
    dtc = jnp.transpose(dtr[:, :SUBLANES])
    gn = gdn_out_norm_g.reshape(1, dk)
    vecs = [v.reshape(1, -1) for v in (lq1, lk1, lq2, lk2)]
    sg = diff_subln_g.reshape(1, -1)
    w_out_b, w_up_b, w_down_b = w_out.astype(BF16), w_up.astype(BF16), w_down.astype(BF16)
    g_ffn = ffn_norm_g.reshape(1, d)
    fcb = ffn_conv_b.reshape(1, -1)
    dff2 = w_up.shape[1]

    xp2 = x_prompt.reshape(nbp * seq, d)
    (z, gcol, grow, qn, kn, kb, vb, tail_g, knew, vnew, qa, ka, vt) = _inproj_gdn(
        xp2, g_attn, w_g, w_d, w_t, gq_t, gk_t, gdn_conv_w, alr, dtr, alc, dtc,
        nb=nbp, tm=min(_Tiles.inproj_rows, seq), gw=gw, dw=dw)
    og, s_prompt = _gdn_prompt(qn, kn, kb, vb, gcol, grow, z, gn, nb=nbp, seq=seq,
                               tl=min(_Tiles.gdn_tokens, seq))
    od = _attn_prompt(qa, ka, vt, *vecs, sg, nb=nbp, seq=seq, tq=min(_Tiles.attn_tile, seq), lam_init=lam_init)
    hr_p = SUBLANES
    y_p, tail_p = _post(xp2, og, od, jnp.zeros((nbp, hr_p, dff2), F32), w_out_b, g_ffn, w_up_b,
                        ffn_conv_w, fcb, w_down_b, nseq=nbp, tm=min(_Tiles.post_rows, seq), shift=1, hr=hr_p,
                        ncol=_Tiles.post_col_groups)
    out_p = (y_p.reshape(nbp, seq, d),
             tail_g[:, SUBLANES - (GDN_CONV - 1):, :],
             s_prompt,
             knew.reshape(nbp, seq, DIFF_HEADS, DIFF_DV),
             vnew.reshape(nbp, seq, DIFF_HEADS, DIFF_DV),
             tail_p[:, hr_p - (FFN_CONV - 1):, :])

    xs2 = jnp.transpose(x_sample, (1, 0, 2)).reshape(nts * nbs, d)
    tm_s = min(_Tiles.sample_rows, nts * nbs)
    assert tm_s % nbs == 0 and nbs % _Tiles.paged_seqs == 0
    qkv, z, ba, knew, vnew, qa, _, _ = _inproj(xs2, g_attn, w_g, w_d, w_t, gq_t, gk_t, tm=tm_s, gw=gw, dw=dw)
    qkv_tm = qkv.reshape(nts, nbs, 3 * gw)
    hist_tm = jnp.transpose(state_gdn_conv, (1, 0, 2))
    og_tm, s_sample = _gdn_sample(qkv_tm, hist_tm, ba.reshape(nts, nbs, LANES), z.reshape(nts, nbs, gw),
                                  state_gdn_s, gdn_conv_w, alr, dtr, gn, bb=min(_Tiles.state_seqs, nbs))
    conv_all = jnp.concatenate([hist_tm, qkv_tm], axis=0)
    conv_s = jnp.transpose(conv_all[nts:], (1, 0, 2))
    to_bm = lambda a: jnp.transpose(a.reshape(nts, nbs, -1), (1, 0, 2))
    q_bm = to_bm(qa)
    q2 = jnp.concatenate([q_bm, q_bm], axis=1)
    pad = jnp.zeros((nbs, nts, dw), F32)
    kn_bm, vn_bm = to_bm(knew), to_bm(vnew)
    od_bm = _attn_sample(page_table, q2, jnp.concatenate([kn_bm, pad], axis=1),
                         jnp.concatenate([vn_bm, pad], axis=1), cache_k, cache_v, *vecs, sg,
                         nt=nts, bs=_Tiles.paged_seqs, lam_init=lam_init)
    od_tm = jnp.transpose(od_bm[:, :nts, :], (1, 0, 2)).reshape(nts * nbs, dw).astype(BF16)
    hr_s = (FFN_CONV - 1) * nbs
    hist_f = jnp.transpose(state_ffn_conv, (1, 0, 2)).reshape(1, hr_s, dff2)
    y_s, tail_s = _post(xs2, og_tm.reshape(nts * nbs, gw), od_tm, hist_f, w_out_b, g_ffn, w_up_b,
                        ffn_conv_w, fcb, w_down_b, nseq=1, tm=tm_s, shift=nbs, hr=hr_s,
                        ncol=_Tiles.post_col_groups)
    out_s = (jnp.transpose(y_s.reshape(nts, nbs, d), (1, 0, 2)),
             conv_s,
             s_sample,
             kn_bm.reshape(nbs, nts, DIFF_HEADS, DIFF_DV),
             vn_bm.reshape(nbs, nts, DIFF_HEADS, DIFF_DV),
             jnp.transpose(tail_s.reshape(FFN_CONV - 1, nbs, dff2), (1, 0, 2)))
    return out_p, out_s


def kernel(x_prompt, x_sample, state_gdn_conv, state_gdn_S, cache_k, cache_v, page_table, state_ffn_conv, attn_norm_g, w_in, gdn_conv_w, gdn_A_log, gdn_dt_bias, gdn_out_norm_g, diff_q_norm_g, diff_k_norm_g, diff_lambda_q1, diff_lambda_k1, diff_lambda_q2, diff_lambda_k2, diff_subln_g, w_out, ffn_norm_g, w_up, ffn_conv_w, ffn_conv_b, w_down):
    depth = w_in.shape[0]
    hp, hs = x_prompt, x_sample
    outs_p, outs_s = [], []
    for l in range(depth):
        wl = (attn_norm_g[l], w_in[l], gdn_conv_w[l], gdn_A_log[l], gdn_dt_bias[l], gdn_out_norm_g[l],
              diff_q_norm_g[l], diff_k_norm_g[l], diff_lambda_q1[l], diff_lambda_k1[l], diff_lambda_q2[l],
              diff_lambda_k2[l], diff_subln_g[l], w_out[l], ffn_norm_g[l], w_up[l], ffn_conv_w[l],
              ffn_conv_b[l], w_down[l])
        out_p, out_s = _layer(l, hp, hs, state_gdn_conv[l], state_gdn_S[l], cache_k[l], cache_v[l],
                              page_table, state_ffn_conv[l], wl)
        hp, hs = out_p[0], out_s[0]
        outs_p.append(out_p[1:])
        outs_s.append(out_s[1:])
    stack = lambda outs, i: jnp.stack([o[i] for o in outs])
    return (hp, hs) + tuple(stack(outs_p, i) for i in range(5)) + tuple(stack(outs_s, i) for i in range(5))
```

```python
import functools
import math

import jax
import jax.numpy as jnp
from jax import lax
from jax.experimental import pallas as pl
from jax.experimental.pallas import tpu as pltpu

F32 = jnp.float32
BF16 = jnp.bfloat16
EPS = 1e-6
NEG = -1e30

GDN_HEADS = 4
GDN_DK = 128
GDN_CONV = 4
GDN_CHUNK = 64
DIFF_HEADS = 4
DIFF_DV = 128
DIFF_DQK = 64
FFN_CONV = 3
LANES = 128
SUBLANES = 8
V7X_VMEM_BYTES = 64 * 1024 * 1024
VMEM_LIMIT = V7X_VMEM_BYTES - 8 * 1024 * 1024


class _Tiles:
    inproj_rows = 512
    gdn_tokens = 256
    attn_tile = 512
    post_rows = 512
    post_col_groups = 1
    sample_rows = 256
    state_seqs = 16
    paged_seqs = 2


def _cparams(sem):
    return pltpu.CompilerParams(dimension_semantics=sem, vmem_limit_bytes=VMEM_LIMIT)


def _const_spec(shape):
    nd = len(shape)
    return pl.BlockSpec(shape, lambda *_: (0,) * nd, pipeline_mode=pl.Buffered(1))


def _dot(a, b):
    return jnp.dot(a, b, preferred_element_type=F32)


def _dot_nt(a, b):
    return lax.dot_general(a, b, (((1,), (1,)), ((), ())), preferred_element_type=F32)


def _dot_tn(a, b):
    return lax.dot_general(a, b, (((0,), (0,)), ((), ())), preferred_element_type=F32)


def _softplus(x):
    return jnp.maximum(x, 0.0) + jnp.log1p(jnp.exp(-jnp.abs(x)))


def _silu(x):
    return x * jax.nn.sigmoid(x)


def _split3(x):
    hi = x.astype(BF16)
    r = x - hi.astype(F32)
    mid = r.astype(BF16)
    lo = (r - mid.astype(F32)).astype(BF16)
    return hi, mid, lo


def _inproj_core(x_ref, g_ref, wg_ref, wd_ref, wt_ref, gq_ref, gk_ref,
                 knew_ref, vnew_ref, qa_ref, ka_ref, vt_ref, gw, dw):
    tm = x_ref.shape[0]
    x = x_ref[...]
    xn = x * lax.rsqrt(jnp.mean(x * x, axis=-1, keepdims=True) + EPS) * g_ref[...]
    xb = xn.astype(BF16)
    proj = _dot(xb, wg_ref[...])
    pd = _dot(xb, wd_ref[...])
    tr = _dot_nt(wt_ref[...], xb)
    dq = pd[:, :dw]
    dk = pd[:, dw:2 * dw]
    dv = pd[:, 2 * dw:]
    vt_ref[...] = jnp.transpose(dv).astype(BF16)
    low =lax.broadcasted_iota(jnp.int32, (tm, DIFF_DV), 1) < DIFF_DQK

    def half_rms(t):
        out = []
        for h in range(DIFF_HEADS):
            th = t[:, h * DIFF_DV:(h + 1) * DIFF_DV]
            sq = th * th
            s1 = jnp.sum(jnp.where(low, sq, 0.0), axis=-1, keepdims=True)
            s2 = jnp.sum(jnp.where(low, 0.0, sq), axis=-1, keepdims=True)
            r1 = lax.rsqrt(s1 * (1.0 / DIFF_DQK) + EPS)
            r2 = lax.rsqrt(s2 * (1.0 / DIFF_DQK) + EPS)
            out.append(th * jnp.where(low, r1, r2))
        return jnp.concatenate(out, axis=1)

    dqn = half_rms(dq) * gq_ref[...]
    dkn = half_rms(dk) * gk_ref[...]
    for h in range(DIFF_HEADS):
        head_rows = pl.ds(h, tm, stride=DIFF_HEADS)
        knew_ref[head_rows, :] = dkn[:, h * DIFF_DV:(h + 1) * DIFF_DV]
        vnew_ref[head_rows, :] = dv[:, h * DIFF_DV:(h + 1) * DIFF_DV]
    qa_ref[...] = (dqn * (DIFF_DQK ** -0.5 * math.log2(math.e))).astype(BF16)
    ka_ref[...] = dkn.astype(BF16)
    return proj, tr


def _inproj_kernel(x_ref, g_ref, wg_ref, wd_ref, wt_ref, gq_ref, gk_ref,
                   qkv_ref, z_ref, ba_ref, knew_ref, vnew_ref, qa_ref, ka_ref, vt_ref, *, gw, dw):
    proj, trow = _inproj_core(x_ref, g_ref, wg_ref, wd_ref, wt_ref, gq_ref, gk_ref,
                              knew_ref, vnew_ref, qa_ref, ka_ref, vt_ref, gw, dw)
    qkv_ref[...] = proj[:, :3 * gw]
    z_ref[...] = proj[:, 3 * gw:]
    ba_ref[...] = jnp.transpose(trow)


def _inproj_gdn_kernel(x_ref, g_ref, wg_ref, wd_ref, wt_ref, gq_ref, gk_ref,
                       cw_ref, alr_ref, dtr_ref, alc_ref, dtc_ref,
                       z_ref, gcol_ref, grow_ref, qn_ref, kn_ref, kb_ref, vb_ref, tail_ref,
                       knew_ref, vnew_ref, qa_ref, ka_ref, vt_ref, xbuf, *, gw, dw):
    i = pl.program_id(1)
    tm = x_ref.shape[0]
    nh, dk = GDN_HEADS, GDN_DK
    hr = SUBLANES

    @pl.when(i == 0)
    def _():
        xbuf[0:hr, :] = jnp.zeros((hr, 3 * gw), F32)

    proj, trow = _inproj_core(x_ref, g_ref, wg_ref, wd_ref, wt_ref, gq_ref, gk_ref,
                              knew_ref, vnew_ref, qa_ref, ka_ref, vt_ref, gw, dw)
    z_ref[...] = proj[:, 3 * gw:]
    xq = proj[:, :3 * gw]
    prev = xbuf[...]
    tail = xq[tm - hr:, :]
    tail_ref[...] = tail
    xbuf[...] = tail
    cw = cw_ref[...]
    first8 = lax.broadcasted_iota(jnp.int32, (hr, 3 * gw), 0)
    y = cw[GDN_CONV - 1:GDN_CONV, :] * xq
    for j in range(1, GDN_CONV):
        sh = pltpu.roll(xq, j, axis=0)
        head = jnp.where(first8 < j, pltpu.roll(prev, j, axis=0), sh[:hr])
        y = y + cw[GDN_CONV - 1 - j:GDN_CONV - j, :] * jnp.concatenate([head, sh[hr:]], axis=0)
    y = _silu(y)
    ba = jnp.transpose(trow)
    beta_c = jax.nn.sigmoid(ba)
    gcol_ref[...] = -jnp.exp(alr_ref[...]) * _softplus(ba + dtr_ref[...])
    grow_ref[...] = -jnp.exp(alc_ref[...]) * _softplus(trow[:SUBLANES] + dtc_ref[...])
    for h in range(nh):
        ls = slice(h * dk, (h + 1) * dk)
        q = y[:, h * dk:(h + 1) * dk]
        k = y[:, gw + h * dk:gw + (h + 1) * dk]
        v = y[:, 2 * gw + h * dk:2 * gw + (h + 1) * dk]
        qn = q * lax.rsqrt(jnp.sum(q * q, axis=-1, keepdims=True) + EPS) * (dk ** -0.5)
        kn = k * lax.rsqrt(jnp.sum(k * k, axis=-1, keepdims=True) + EPS)
        beta = jnp.broadcast_to(beta_c[:, h:h + 1], (tm, dk))
        qn_ref[:, ls] = qn.astype(BF16)
        kn_ref[:, ls] = kn.astype(BF16)
        kb_ref[:, ls] = (kn * beta).astype(BF16)
        vb_ref[:, ls] = (v * beta).astype(BF16)


def _attn_out(n, tm, dw):
    return (
        jax.ShapeDtypeStruct((n * DIFF_HEADS, DIFF_DV), F32),
        jax.ShapeDtypeStruct((n * DIFF_HEADS, DIFF_DV), F32),
        jax.ShapeDtypeStruct((n, dw), BF16),
        jax.ShapeDtypeStruct((n, dw), BF16),
        jax.ShapeDtypeStruct((n // tm, dw, tm), BF16),
    )


def _inproj(x2d, g_attn, w_g, w_d, w_t, gq_t, gk_t, *, tm, gw, dw):
    n, d = x2d.shape
    row = lambda i: (i, 0)
    out_shape = (
        jax.ShapeDtypeStruct((n, 3 * gw), F32),
        jax.ShapeDtypeStruct((n, gw), F32),
        jax.ShapeDtypeStruct((n, LANES), F32),
    ) + _attn_out(n, tm, dw)
    out_specs = (
        pl.BlockSpec((tm, 3 * gw), row), pl.BlockSpec((tm, gw), row), pl.BlockSpec((tm, LANES), row),
        pl.BlockSpec((tm * DIFF_HEADS, DIFF_DV), row), pl.BlockSpec((tm * DIFF_HEADS, DIFF_DV), row),
        pl.BlockSpec((tm, dw), row), pl.BlockSpec((tm, dw), row),
        pl.BlockSpec((None, dw, tm), lambda i: (i, 0, 0)),
    )
    in_specs = [
        pl.BlockSpec((tm, d), row), _const_spec((1, d)), _const_spec(w_g.shape), _const_spec(w_d.shape),
        _const_spec(w_t.shape), _const_spec((1, dw)), _const_spec((1, dw)),
    ]
    return pl.pallas_call(
        functools.partial(_inproj_kernel, gw=gw, dw=dw),
        grid=(n // tm,), in_specs=in_specs, out_specs=out_specs, out_shape=out_shape,
        compiler_params=_cparams(("parallel",)), name="inproj",
    )(x2d, g_attn, w_g, w_d, w_t, gq_t, gk_t)


def _inproj_gdn(x2d, g_attn, w_g, w_d, w_t, gq_t, gk_t, cw, alr, dtr, alc, dtc, *, nb, tm, gw, dw):
    n, d = x2d.shape
    nt = n // (nb * tm)
    row = lambda b, i: (b * nt + i, 0)
    out_shape = (
        jax.ShapeDtypeStruct((n, gw), F32),
        jax.ShapeDtypeStruct((n, LANES), F32),
        jax.ShapeDtypeStruct((nb, SUBLANES, n // nb), F32),
        jax.ShapeDtypeStruct((n, gw), BF16),
        jax.ShapeDtypeStruct((n, gw), BF16),
        jax.ShapeDtypeStruct((n, gw), BF16),
        jax.ShapeDtypeStruct((n, gw), BF16),
        jax.ShapeDtypeStruct((nb, SUBLANES, 3 * gw), F32),
    ) + _attn_out(n, tm, dw)
    out_specs = (
        pl.BlockSpec((tm, gw), row), pl.BlockSpec((tm, LANES), row),
        pl.BlockSpec((None, SUBLANES, tm), lambda b, i: (b, 0, i)),
        pl.BlockSpec((tm, gw), row), pl.BlockSpec((tm, gw), row), pl.BlockSpec((tm, gw), row),
        pl.BlockSpec((tm, gw), row),
        pl.BlockSpec((None, SUBLANES, 3 * gw), lambda b, i: (b, 0, 0)),
        pl.BlockSpec((tm * DIFF_HEADS, DIFF_DV), row), pl.BlockSpec((tm * DIFF_HEADS, DIFF_DV), row),
        pl.BlockSpec((tm, dw), row), pl.BlockSpec((tm, dw), row),
        pl.BlockSpec((None, dw, tm), lambda b, i: (b * nt + i, 0, 0)),
    )
    in_specs = [
        pl.BlockSpec((tm, d), row), _const_spec((1, d)), _const_spec(w_g.shape), _const_spec(w_d.shape),
        _const_spec(w_t.shape), _const_spec((1, dw)), _const_spec((1, dw)),
        _const_spec(cw.shape), _const_spec((1, LANES)), _const_spec((1, LANES)),
        _const_spec((SUBLANES, 1)), _const_spec((SUBLANES, 1)),
    ]
    return pl.pallas_call(
        functools.partial(_inproj_gdn_kernel, gw=gw, dw=dw),
        grid=(nb, nt), in_specs=in_specs, out_specs=out_specs, out_shape=out_shape,
        scratch_shapes=[pltpu.VMEM((SUBLANES, 3 * gw), F32)],
        compiler_params=_cparams(("parallel", "arbitrary")), name="inproj_gdn",
    )(x2d, g_attn, w_g, w_d, w_t, gq_t, gk_t, cw, alr, dtr, alc, dtc)


def _gdn_prompt_kernel(qn_ref, kn_ref, kb_ref, vb_ref, gcol_ref, grow_ref, z_ref, gn_ref,
                       og_ref, sout_ref, s_scr, *, tl):
    i = pl.program_id(0)
    nb = qn_ref.shape[0]
    nh, dk, c = GDN_HEADS, GDN_DK, GDN_CHUNK
    gw = nh * dk

    @pl.when(i == 0)
    def _():
        s_scr[...] = jnp.zeros_like(s_scr)

    g_r = jnp.concatenate([grow_ref[b] for b in range(nb)], axis=0)

    ii = lax.broadcasted_iota(jnp.int32, (tl, tl), 0)
    jj = lax.broadcasted_iota(jnp.int32, (tl, tl), 1)
    same = lax.shift_right_logical(ii, 6) == lax.shift_right_logical(jj, 6)
    tri = jnp.where(same & (ii >= jj), 1.0, 0.0).astype(BF16)
    triu = jnp.where(same & (ii <= jj), 1.0, 0.0).astype(BF16)
    gc_c = [sum(_dot(tri, p) for p in _split3(gcol_ref[b])) for b in range(nb)]
    gc_r = sum(_dot(p, triu) for p in _split3(g_r))

    pr = 2 * c
    pi = lax.broadcasted_iota(jnp.int32, (pr, pr), 0)
    pj = lax.broadcasted_iota(jnp.int32, (pr, pr), 1)
    psame = lax.shift_right_logical(pi, 6) == lax.shift_right_logical(pj, 6)
    lower = psame & (pi >= pj)
    strict = psame & (pi > pj)
    eye = jnp.where(pi == pj, 1.0, 0.0).astype(F32)
    zeros_c = jnp.zeros((c, dk), F32)

    npair = tl // pr
    seqs = [(b, h) for b in range(nb) for h in range(nh)]

    def tile(ref, b, h, p):
        return ref[b, p * pr:(p + 1) * pr, h * dk:(h + 1) * dk]

    gcp, kpb, sol, qk, qg = {}, {}, {}, {}, {}

    def prepare(p):
        hp = [(b, h, p) for b, h in seqs]
        for b, h, _ in hp:
            gcp[b, h, p] = jnp.broadcast_to(gc_c[b][p * pr:(p + 1) * pr, nh + h:nh + h + 1], (pr, dk))
        dec = {(b, h, p): jnp.exp(jnp.where(
            lower, gcp[b, h, p] - gc_r[b * SUBLANES + nh + h:b * SUBLANES + nh + h + 1, p * pr:(p + 1) * pr], NEG))
            for b, h, _ in hp}
        kbp = {x: tile(kb_ref, *x) for x in hp}
        for x in hp:
            kpb[x] = tile(kn_ref, *x)
        m = {x: -jnp.where(strict, _dot_nt(kbp[x], kpb[x]) * dec[x], 0.0) for x in hp}
        yield
        a = {x: eye + m[x] for x in hp}
        pw = {x: _dot(m[x].astype(BF16), m[x].astype(BF16)) for x in hp}
        yield
        for it in range(1, 6):
            for x in hp:
                pwb = pw[x].astype(BF16)
                if it < 5:
                    res = _dot(pwb, jnp.concatenate([a[x].astype(BF16), pwb], axis=1))
                    a[x] = a[x] + res[:, :pr]
                    pw[x] = res[:, pr:]
                else:
                    a[x] = a[x] + _dot(pwb, a[x].astype(BF16))
            yield
        egc = {x: jnp.exp(gcp[x]) for x in hp}
        for x in hp:
            sol[x] = _dot(a[x].astype(BF16), jnp.concatenate(
                [tile(vb_ref, *x), (kbp[x].astype(F32) * egc[x]).astype(BF16)], axis=1))
            qk[x] = (_dot_nt(tile(qn_ref, *x), kpb[x]) * dec[x]).astype(BF16)
            qg[x] = tile(qn_ref, *x).astype(F32) * egc[x]
        yield

    s_cur = {x: s_scr[x] for x in seqs}
    o_rows = {x: [] for x in seqs}

    def recur(p):
        for cc in range(2):
            c0 = cc * c
            ws = {}
            for b, h in seqs:
                wq = jnp.concatenate([sol[b, h, p][c0:c0 + c, dk:], qg[b, h, p][c0:c0 + c]], axis=0)
                ws[b, h] = _dot(wq.astype(BF16), s_cur[b, h].astype(BF16))
            yield
            for b, h in seqs:
                g_c0 = gcp[b, h, p][c0:c0 + c]
                glast = g_c0[c - 1:c, :]
                kd = kpb[b, h, p][c0:c0 + c].astype(F32) * jnp.exp(glast - g_c0)
                v_new = (sol[b, h, p][c0:c0 + c, :dk] - ws[b, h][:c]).astype(BF16)
                s_cur[b, h] = s_cur[b, h] * jnp.exp(glast) + _dot_tn(kd.astype(BF16), v_new)
                zc = jnp.zeros_like(v_new)
                vn_pad = jnp.concatenate([v_new, zc] if cc == 0 else [zc, v_new], axis=0)
                o_rows[b, h].append(ws[b, h][c:] + _dot(qk[b, h, p][c0:c0 + c], vn_pad))
            yield

    def interleave(*stages):
        live = list(stages)
        while live:
            for g in list(live):
                if next(g, live) is live:
                    live.remove(g)

    interleave(prepare(0))
    for p in range(npair):
        interleave(*([prepare(p + 1)] if p + 1 < npair else []), recur(p))
    for b, h in seqs:
        s_scr[b, h] = s_cur[b, h]
        o = jnp.concatenate(o_rows[b, h], axis=0)
        o = o * lax.rsqrt(jnp.mean(o * o, axis=-1, keepdims=True) + EPS) * gn_ref[...]
        og_ref[b, :, h * dk:(h + 1) * dk] = (o * _silu(z_ref[b, :, h * dk:(h + 1) * dk])).astype(og_ref.dtype)

    @pl.when(i == pl.num_programs(0) - 1)
    def _():
        sout_ref[...] = s_scr[...]


def _gdn_prompt(qn, kn, kb, vb, gcol, grow, z, gn, *, nb, seq, tl):
    nh, dk = GDN_HEADS, GDN_DK
    gw = nh * dk
    nt = seq // tl
    tok = lambda i: (0, i, 0)
    wide = pl.BlockSpec((nb, tl, gw), tok)
    per_seq = lambda a: a.reshape(nb, seq, a.shape[-1])
    in_specs = [
        wide, wide, wide, wide, pl.BlockSpec((nb, tl, LANES), tok),
        pl.BlockSpec((nb, SUBLANES, tl), lambda i: (0, 0, i)),
        wide, _const_spec((1, dk)),
    ]
    out_shape = (jax.ShapeDtypeStruct((nb, seq, gw), BF16),
                 jax.ShapeDtypeStruct((nb, nh, dk, dk), F32))
    out_specs = (wide, pl.BlockSpec((nb, nh, dk, dk), lambda i: (0, 0, 0, 0)))
    og, s_out = pl.pallas_call(
        functools.partial(_gdn_prompt_kernel, tl=tl),
        grid=(nt,), in_specs=in_specs, out_specs=out_specs, out_shape=out_shape,
        scratch_shapes=[pltpu.VMEM((nb, nh, dk, dk), F32)],
        compiler_params=_cparams(("arbitrary",)), name="gdn_prompt",
    )(per_seq(qn), per_seq(kn), per_seq(kb), per_seq(vb), per_seq(gcol), grow, per_seq(z), gn)
    return og.reshape(nb * seq, gw), s_out


def _gdn_sample_pre_kernel(qkv_ref, hist_ref, ba_ref, cw_ref, alr_ref, dtr_ref, r_ref, intra_ref, *, nt):
    nh, dk = GDN_HEADS, GDN_DK
    gw = nh * dk
    nb = qkv_ref.shape[1]
    cw = cw_ref[...]
    nhist = GDN_CONV - 1
    xp = [hist_ref[j] for j in range(nhist)] + [qkv_ref[t] for t in range(nt)]
    ys = []
    for t in range(nt):
        y = cw[0:1, :] * xp[t]
        for j in range(1, GDN_CONV):
            y = y + cw[j:j + 1, :] * xp[t + j]
        ys.append(_silu(y))
    beta, gc = [], []
    for t in range(nt):
        ba = ba_ref[t]
        beta_c = jax.nn.sigmoid(ba)
        g_c = -jnp.exp(alr_ref[...]) * _softplus(ba + dtr_ref[...])
        beta.append([jnp.broadcast_to(beta_c[:, h:h + 1], (nb, dk)) for h in range(nh)])
        g_t = [jnp.broadcast_to(g_c[:, nh + h:nh + h + 1], (nb, dk)) for h in range(nh)]
        gc.append(g_t if t == 0 else [gc[t - 1][h] + g_t[h] for h in range(nh)])
    zrow = jnp.zeros((nb, dk), F32)
    for h in range(nh):
        q, k, v = [], [], []
        for t in range(nt):
            qt = ys[t][:, h * dk:(h + 1) * dk]
            kt = ys[t][:, gw + h * dk:gw + (h + 1) * dk]
            q.append(qt * lax.rsqrt(jnp.sum(qt * qt, axis=-1, keepdims=True) + EPS) * (dk ** -0.5))
            k.append(kt * lax.rsqrt(jnp.sum(kt * kt, axis=-1, keepdims=True) + EPS))
            v.append(ys[t][:, 2 * gw + h * dk:2 * gw + (h + 1) * dk])
        g = [gc[t][h] for t in range(nt)]
        us, ws = [], []
        for t in range(nt):
            b_t = beta[t][h]
            u_t = v[t] * b_t
            w_t = k[t] * b_t * jnp.exp(g[t])
            for s in range(t):
                m_ts = b_t * jnp.sum(k[t] * k[s], axis=-1, keepdims=True) * jnp.exp(g[t] - g[s])
                u_t = u_t - m_ts * us[s]
                w_t = w_t - m_ts * ws[s]
            us.append(u_t)
            ws.append(w_t)
        lane = slice(h * dk, (h + 1) * dk)
        for t in range(nt):
            r_ref[t, :, lane] = ws[t]
            r_ref[nt + t, :, lane] = q[t] * jnp.exp(g[t])
            r_ref[2 * nt + t, :, lane] = us[t]
            r_ref[3 * nt + t, :, lane] = k[t] * jnp.exp(g[nt - 1] - g[t])
            for s in range(nt):
                if s <= t:
                    intra_ref[t * nt + s, :, lane] = (jnp.sum(q[t] * k[s], axis=-1, keepdims=True)
                                                       * jnp.exp(g[t] - g[s]))
                else:
                    intra_ref[t * nt + s, :, lane] = zrow
        r_ref[4 * nt, :, lane] = jnp.exp(g[nt - 1])
        for r in range(4 * nt + 1, r_ref.shape[0]):
            r_ref[r, :, lane] = zrow


def _gdn_sample_state_kernel(r_ref, s_ref, ws_ref, sout_ref, *, nt, bb):
    nh = GDN_HEADS
    rows = lax.broadcasted_iota(jnp.int32, (2 * nt, GDN_DK), 0)
    inst = [(bi, h) for bi in range(bb) for h in range(nh)]
    ws = {x: _dot(r_ref[x[0], x[1], 0:2 * nt, :].astype(BF16), s_ref[x].astype(BF16)) for x in inst}
    for x in inst:
        ud = r_ref[x[0], x[1], 2 * nt:4 * nt, :]
        vn = jnp.where(rows < nt, ud - ws[x], 0.0)
        kd = jnp.where(rows < nt, pltpu.roll(ud, nt, axis=0), 0.0)
        ws_ref[x] = ws[x]
        sout_ref[x] = (s_ref[x] * r_ref[x[0], x[1], 4 * nt:4 * nt + 1, :]
                       + _dot_tn(kd.astype(BF16), vn.astype(BF16)))


def _gdn_sample_post_kernel(ws_ref, r_ref, intra_ref, z_ref, gn_ref, og_ref, *, nt):
    nh, dk = GDN_HEADS, GDN_DK
    for h in range(nh):
        lane = slice(h * dk, (h + 1) * dk)
        vn = [r_ref[2 * nt + t, :, lane] - ws_ref[t, :, lane] for t in range(nt)]
        for t in range(nt):
            o = ws_ref[nt + t, :, lane]
            for s in range(t + 1):
                o = o + intra_ref[t * nt + s, :, lane] * vn[s]
            o = o * lax.rsqrt(jnp.mean(o * o, axis=-1, keepdims=True) + EPS) * gn_ref[...]
            og_ref[t, :, lane] = (o * _silu(z_ref[t, :, lane])).astype(og_ref.dtype)


def _gdn_sample(qkv_tm, hist_tm, ba_tm, z_tm, state_s, cw, alr, dtr, gn, *, bb):
    nt, nb, _ = qkv_tm.shape
    nh, dk = GDN_HEADS, GDN_DK
    gw = nh * dk
    nr = 6 * nt
    full = lambda shape: pl.BlockSpec(shape, lambda *_: (0,) * len(shape))
    r_tm, intra = pl.pallas_call(
        functools.partial(_gdn_sample_pre_kernel, nt=nt),
        grid=(1,),
        in_specs=[full(qkv_tm.shape), full(hist_tm.shape), full(ba_tm.shape), full(cw.shape),
                  full(alr.shape), full(dtr.shape)],
        out_specs=(full((nr, nb, gw)), full((nt * nt, nb, gw))),
        out_shape=(jax.ShapeDtypeStruct((nr, nb, gw), F32), jax.ShapeDtypeStruct((nt * nt, nb, gw), F32)),
        compiler_params=_cparams(("arbitrary",)), name="gdn_sample_pre",
    )(qkv_tm, hist_tm, ba_tm, cw, alr, dtr)
    r_bm = jnp.transpose(r_tm.reshape(nr, nb, nh, dk), (1, 2, 0, 3))
    blk = lambda b: (b, 0, 0, 0)
    ws_bm, s_new = pl.pallas_call(
        functools.partial(_gdn_sample_state_kernel, nt=nt, bb=bb),
        grid=(nb // bb,),
        in_specs=[pl.BlockSpec((bb, nh, nr, dk), blk), pl.BlockSpec((bb, nh, dk, dk), blk)],
        out_specs=(pl.BlockSpec((bb, nh, 2 * nt, dk), blk), pl.BlockSpec((bb, nh, dk, dk), blk)),
        out_shape=(jax.ShapeDtypeStruct((nb, nh, 2 * nt, dk), F32), jax.ShapeDtypeStruct((nb, nh, dk, dk), F32)),
        compiler_params=_cparams(("parallel",)), name="gdn_sample_state",
    )(r_bm, state_s)
    ws_tm = jnp.transpose(ws_bm, (2, 0, 1, 3)).reshape(2 * nt, nb, gw)
    og = pl.pallas_call(
        functools.partial(_gdn_sample_post_kernel, nt=nt),
        grid=(1,),
        in_specs=[full(ws_tm.shape), full(r_tm.shape), full(intra.shape), full(z_tm.shape), full(gn.shape)],
        out_specs=full((nt, nb, gw)),
        out_shape=jax.ShapeDtypeStruct((nt, nb, gw), BF16),
        compiler_params=_cparams(("arbitrary",)), name="gdn_sample_post",
    )(ws_tm, r_tm, intra, z_tm, gn)
    return og, s_new


def _lambda(lq1_ref, lk1_ref, lq2_ref, lk2_ref, lam_init):
    s1 = jnp.sum(lq1_ref[...] * lk1_ref[...], axis=-1, keepdims=True)
    s2 = jnp.sum(lq2_ref[...] * lk2_ref[...], axis=-1, keepdims=True)
    return jnp.exp(s1) - jnp.exp(s2) + lam_init


def _subln(o, g_ref, lam_init):
    return o * lax.rsqrt(jnp.mean(o * o, axis=-1, keepdims=True) + EPS) * g_ref[...] * (1.0 - lam_init)


def _attn_prompt_kernel(q_ref, k_ref, vt_ref, lq1_ref, lk1_ref, lq2_ref, lk2_ref, sg_ref, o_ref,
                        sa_scr, sb_scr, m_scr, acc_scr, *, tq, lam_init):
    qi = pl.program_id(1)
    nh, dv, dqk = DIFF_HEADS, DIFF_DV, DIFF_DQK
    lam = _lambda(lq1_ref, lk1_ref, lq2_ref, lk2_ref, lam_init)
    lane = lax.broadcasted_iota(jnp.int32, (tq, dv), 1)
    kidx = lax.broadcasted_iota(jnp.int32, (tq, tq), 0)
    qidx = lax.broadcasted_iota(jnp.int32, (tq, tq), 1)
    causal = kidx <= qidx
    nsum = 2 * SUBLANES
    ones_rows = jnp.ones((nsum, tq), BF16)
    lanes = [slice(h * dv, (h + 1) * dv) for h in range(nh)]
    qs = []
    for h in range(nh):
        q = q_ref[:, lanes[h]]
        zero = jnp.zeros_like(q)
        qs.append((jnp.where(lane < dqk, q, zero), jnp.where(lane >= dqk, q, zero)))

    nvt = tq // vt_ref.shape[-1]
    nchain = 2 * nh

    def scores(dst, j):
        for h in range(nh):
            kb = k_ref[pl.ds(pl.multiple_of(j * tq, tq), tq), lanes[h]]
            for c in range(2):
                dst[2 * h + c] = _dot_nt(kb, qs[h][c])

    def softmax_pv(src, j, masked):
        ps, alphas = [], []
        for x in range(nchain):
            s_x = jnp.where(causal, src[x], NEG) if masked else src[x]
            m_i = m_scr[x]
            m_new = jnp.maximum(m_i, jnp.max(s_x, axis=0, keepdims=True))
            m_scr[x] = m_new
            alphas.append(jnp.exp2(m_i - m_new))
            ps.append(jnp.exp2(s_x - m_new).astype(BF16))
        for h in range(nh):
            vt = jnp.concatenate([vt_ref[j * nvt + t, lanes[h], :] for t in range(nvt)], axis=1)
            vt = jnp.concatenate([vt, ones_rows], axis=0)
            for c in range(2):
                x = 2 * h + c
                acc_scr[x] = alphas[x] * acc_scr[x] + _dot(vt, ps[x])

    m_scr[...] = jnp.full(m_scr.shape, NEG, F32)
    acc_scr[...] = jnp.zeros(acc_scr.shape, F32)
    scores(sa_scr, 0)

    def two_blocks(i, _):
        scores(sb_scr, 2 * i + 1)
        softmax_pv(sa_scr, 2 * i, False)
        scores(sa_scr, 2 * i + 2)
        softmax_pv(sb_scr, 2 * i + 1, False)
        return 0

    lax.fori_loop(0, lax.shift_right_logical(qi, 1), two_blocks, 0)
    odd = lax.rem(qi, 2) == 1

    @pl.when(odd)
    def _():
        scores(sb_scr, qi)
        softmax_pv(sa_scr, qi - 1, False)
        softmax_pv(sb_scr, qi, True)

    @pl.when(jnp.logical_not(odd))
    def _():
        softmax_pv(sa_scr, qi, True)

    for h in range(nh):
        ls = lanes[h]
        a1, a2 = acc_scr[2 * h], acc_scr[2 * h + 1]
        ot = a1[:dv] / a1[dv:dv + 1] - lam * (a2[:dv] / a2[dv:dv + 1])
        ot = ot * lax.rsqrt(jnp.mean(ot * ot, axis=0, keepdims=True) + EPS) * (1.0 - lam_init)
        o_ref[:, ls] = (jnp.transpose(ot) * sg_ref[...]).astype(o_ref.dtype)


def _attn_prompt(qa, ka, vt, lq1, lk1, lq2, lk2, sg, *, nb, seq, tq, lam_init):
    nh, dv = DIFF_HEADS, DIFF_DV
    wd = nh * dv
    nq = seq // tq
    qa3, ka3 = qa.reshape(nb, seq, wd), ka.reshape(nb, seq, wd)
    tv = vt.shape[-1]
    nv = seq // tv
    vt4 = vt.reshape(nb, nv, wd, tv)
    vec = _const_spec((1, DIFF_DQK))
    out = pl.pallas_call(
        functools.partial(_attn_prompt_kernel, tq=tq, lam_init=lam_init),
        grid=(nb, nq),
        in_specs=[pl.BlockSpec((None, tq, wd), lambda b, i: (b, i, 0)),
                  pl.BlockSpec((None, seq, wd), lambda b, i: (b, 0, 0)),
                  pl.BlockSpec((None, nv, wd, tv), lambda b, i: (b, 0, 0, 0)),
                  vec, vec, vec, vec, _const_spec((1, dv))],
        out_specs=pl.BlockSpec((None, tq, wd), lambda b, i: (b, i, 0)),
        out_shape=jax.ShapeDtypeStruct((nb, seq, wd), BF16),
        scratch_shapes=[pltpu.VMEM((2 * nh, tq, tq), F32), pltpu.VMEM((2 * nh, tq, tq), F32),
                        pltpu.VMEM((2 * nh, 1, tq), F32), pltpu.VMEM((2 * nh, dv + 2 * SUBLANES, tq), F32)],
        compiler_params=_cparams(("parallel", "arbitrary")), name="attn_prompt",
    )(qa3, ka3, vt4, lq1, lk1, lq2, lk2, sg)
    return out.reshape(nb * seq, wd)


def _attn_sample_kernel(pt_ref, q_ref, kn_ref, vn_ref, *rest, n_pages, nt, bs, lam_init):
    k_refs = rest[:bs * n_pages]
    v_refs = rest[bs * n_pages:2 * bs * n_pages]
    lq1_ref, lk1_ref, lq2_ref, lk2_ref, sg_ref, o_ref = rest[2 * bs * n_pages:]
    del pt_ref
    nh, dv, dqk = DIFF_HEADS, DIFF_DV, DIFF_DQK
    page = k_refs[0].shape[0] // nh
    nr = 2 * nt
    lam = _lambda(lq1_ref, lk1_ref, lq2_ref, lk2_ref, lam_init)
    row = lax.broadcasted_iota(jnp.int32, (nr, dv), 0)
    lane = lax.broadcasted_iota(jnp.int32, (nr, dv), 1)
    first = row < nt
    keep = jnp.logical_xor(lane >= dqk, first)
    tpos = jnp.where(first, row, row - nt)
    new_ok = (lane < nt) & (lane <= tpos)
    zpad = jnp.zeros((page - nr, dv), F32)
    lanes = [slice(h * dv, (h + 1) * dv) for h in range(nh)]
    head_rows = [pl.ds(h, page, stride=nh) for h in range(nh)]
    groups = [list(range(j, min(j + 2, n_pages))) for j in range(0, n_pages, 2)]

    chains = [(i, h) for i in range(bs) for h in range(nh)]

    def past(refs, i, h, grp):
        return jnp.concatenate([refs[i * n_pages + j][head_rows[h], :] for j in grp], axis=0).astype(BF16)

    s_all = {}
    for i, h in chains:
        qh = q_ref[i, :, lanes[h]]
        qz = jnp.where(keep, qh, jnp.zeros_like(qh))
        s_ih = [_dot_nt(qz, past(k_refs, i, h, grp)) for grp in groups]
        knew = jnp.concatenate([kn_ref[i, :, lanes[h]], zpad], axis=0)
        s_ih.append(jnp.where(new_ok, _dot_nt(qz, knew.astype(BF16)), NEG))
        s_all[i, h] = s_ih
    p_all, l_all = {}, {}
    for x in chains:
        m = jnp.max(s_all[x][0], axis=-1, keepdims=True)
        for s in s_all[x][1:]:
            m = jnp.maximum(m, jnp.max(s, axis=-1, keepdims=True))
        ps = [jnp.exp2(s - m) for s in s_all[x]]
        l = jnp.sum(ps[0], axis=-1, keepdims=True)
        for p in ps[1:]:
            l = l + jnp.sum(p, axis=-1, keepdims=True)
        p_all[x] = [p.astype(BF16) for p in ps]
        l_all[x] = l
    for i, h in chains:
        acc = _dot(p_all[i, h][-1], jnp.concatenate([vn_ref[i, :, lanes[h]], zpad], axis=0).astype(BF16))
        for gi, grp in enumerate(groups):
            acc = acc + _dot(p_all[i, h][gi], past(v_refs, i, h, grp))
        o2 = acc / l_all[i, h]
        o = o2 - lam * pltpu.roll(o2, nt, axis=0)
        o_ref[i, :, lanes[h]] = _subln(o, sg_ref, lam_init)


def _attn_sample(page_table, q2, kn8, vn8, cache_k, cache_v, lq1, lk1, lq2, lk2, sg, *, nt, bs, lam_init):
    nb, n_pages = page_table.shape
    nh, dv = DIFF_HEADS, DIFF_DV
    page = cache_k.shape[1]
    wd = nh * dv
    ck = cache_k.reshape(cache_k.shape[0], page * nh, dv)
    cv = cache_v.reshape(cache_v.shape[0], page * nh, dv)
    nr = 2 * nt
    small = pl.BlockSpec((bs, nr, wd), lambda b, pt: (b, 0, 0))

    def page_spec(i, j):
        return pl.BlockSpec((None, page * nh, dv), lambda b, pt: (pt[b * bs + i, j], 0, 0))

    pages = [page_spec(i, j) for i in range(bs) for j in range(n_pages)]
    vec = pl.BlockSpec((1, DIFF_DQK), lambda b, pt: (0, 0))
    grid_spec = pltpu.PrefetchScalarGridSpec(
        num_scalar_prefetch=1, grid=(nb // bs,),
        in_specs=[small, small, small] + pages * 2
        + [vec, vec, vec, vec, pl.BlockSpec((1, dv), lambda b, pt: (0, 0))],
        out_specs=small)
    return pl.pallas_call(
        functools.partial(_attn_sample_kernel, n_pages=n_pages, nt=nt, bs=bs, lam_init=lam_init),
        grid_spec=grid_spec, out_shape=jax.ShapeDtypeStruct((nb, nr, wd), F32),
        compiler_params=_cparams(("parallel",)), name="attn_sample",
    )(page_table, q2, kn8, vn8, *([ck] * (bs * n_pages)), *([cv] * (bs * n_pages)), lq1, lk1, lq2, lk2, sg)


def _post_kernel(x_ref, og_ref, od_ref, hist_ref, wo_ref, gf_ref, wu_ref, fcw_ref, fcb_ref, wd_ref,
                 y_ref, tail_ref, ubuf, *, tm, shift, hr, dff, ncol):
    i = pl.program_id(1)
    gw = og_ref.shape[1]

    @pl.when(i == 0)
    def _():
        ubuf[0:hr, :] = hist_ref[...]

    hres = x_ref[...] + _dot(og_ref[...], wo_ref[0:gw, :]) + _dot(od_ref[...], wo_ref[gw:, :])
    hn = (hres * lax.rsqrt(jnp.mean(hres * hres, axis=-1, keepdims=True) + EPS) * gf_ref[...]).astype(BF16)
    fcw = fcw_ref[...]
    fcb = fcb_ref[...]

    def conv(cols):
        ubuf[hr:hr + tm, cols] = _dot(hn, wu_ref[:, cols])
        u = fcw[FFN_CONV - 1:FFN_CONV, cols] * ubuf[hr:hr + tm, cols] + fcb[:, cols]
        for j in range(1, FFN_CONV):
            u = u + fcw[FFN_CONV - 1 - j:FFN_CONV - j, cols] * ubuf[hr - j * shift:hr - j * shift + tm, cols]
        return u

    y = hres
    wc = dff // ncol
    for c in range(ncol):
        gate = conv(slice(c * wc, (c + 1) * wc))
        val = conv(slice(dff + c * wc, dff + (c + 1) * wc))
        act = (_silu(gate) * val).astype(BF16)
        y = y + _dot(act, wd_ref[c * wc:(c + 1) * wc, :])
    y_ref[...] = y
    tail = ubuf[tm:tm + hr, :]
    tail_ref[...] = tail
    ubuf[0:hr, :] = tail


def _post(x2d, og, od, hist, w_out, g_ffn, w_up, fcw, fcb, w_down, *, nseq, tm, shift, hr, ncol):
    n, d = x2d.shape
    gw = og.shape[1]
    dff2 = w_up.shape[1]
    nt = n // (nseq * tm)
    row = lambda b, i: (b * nt + i, 0)
    in_specs = [
        pl.BlockSpec((tm, d), row), pl.BlockSpec((tm, gw), row), pl.BlockSpec((tm, od.shape[1]), row),
        pl.BlockSpec((None, hr, dff2), lambda b, i: (b, 0, 0)),
        _const_spec(w_out.shape), _const_spec((1, d)), _const_spec(w_up.shape),
        _const_spec((FFN_CONV, dff2)), _const_spec((1, dff2)), _const_spec(w_down.shape),
    ]
    out_shape = (jax.ShapeDtypeStruct((n, d), F32), jax.ShapeDtypeStruct((nseq, hr, dff2), F32))
    out_specs = (pl.BlockSpec((tm, d), row), pl.BlockSpec((None, hr, dff2), lambda b, i: (b, 0, 0)))
    return pl.pallas_call(
        functools.partial(_post_kernel, tm=tm, shift=shift, hr=hr, dff=dff2 // 2, ncol=ncol),
        grid=(nseq, nt), in_specs=in_specs, out_specs=out_specs, out_shape=out_shape,
        scratch_shapes=[pltpu.VMEM((hr + tm, dff2), F32)],
        compiler_params=_cparams(("parallel", "arbitrary")), name="post",
    )(x2d, og, od, hist, w_out, g_ffn, w_up, fcw, fcb, w_down)


def _lane_pad(vec, offset):
    out = jnp.zeros((1, LANES), F32)
    return lax.dynamic_update_slice(out, vec.reshape(1, -1).astype(F32), (0, offset))


def _layer(l, x_prompt, x_sample, state_gdn_conv, state_gdn_s, cache_k, cache_v, page_table, state_ffn_conv, wl):
    (attn_norm_g, w_in, gdn_conv_w, gdn_a_log, gdn_dt_bias, gdn_out_norm_g, diff_q_norm_g, diff_k_norm_g,
     lq1, lk1, lq2, lk2, diff_subln_g, w_out, ffn_norm_g, w_up, ffn_conv_w, ffn_conv_b, w_down) = wl
    nbp, seq, d = x_prompt.shape
    nbs, nts, _ = x_sample.shape
    nh, dk = GDN_HEADS, GDN_DK
    gw = nh * dk
    dw = DIFF_HEADS * DIFF_DV
    lam_init = 0.8 - 0.6 * math.exp(-0.3 * l)

    c_b = 4 * gw
    c_d = c_b + 2 * nh
    w_g = w_in[:, :c_b].astype(BF16)
    w_d = w_in[:, c_d:].astype(BF16)
    w_t = jnp.transpose(jnp.concatenate(
        [w_in[:, c_b:c_d], jnp.zeros((d, LANES - 2 * nh), w_in.dtype)], axis=1)).astype(BF16)
    reps = dw // DIFF_DQK
    gq_t = jnp.tile(diff_q_norm_g.reshape(1, -1), (1, reps))
    gk_t = jnp.tile(diff_k_norm_g.reshape(1, -1), (1, reps))
    g_attn = attn_norm_g.reshape(1, d)
    alr, dtr = _lane_pad(gdn_a_log, nh), _lane_pad(gdn_dt_bias, nh)
    alc = jnp.transpose(alr[:, :SUBLANES])
    dtc = jnp.transpose(dtr[:, :SUBLANES])
    gn = gdn_out_norm_g.reshape(1, dk)
    vecs = [v.reshape(1, -1) for v in (lq1, lk1, lq2, lk2)]
    sg = diff_subln_g.reshape(1, -1)
    w_out_b, w_up_b, w_down_b = w_out.astype(BF16), w_up.astype(BF16), w_down.astype(BF16)
    g_ffn = ffn_norm_g.reshape(1, d)
    fcb = ffn_conv_b.reshape(1, -1)
    dff2 = w_up.shape[1]

    xp2 = x_prompt.reshape(nbp * seq, d)
    (z, gcol, grow, qn, kn, kb, vb, tail_g, knew, vnew, qa, ka, vt) = _inproj_gdn(
        xp2, g_attn, w_g, w_d, w_t, gq_t, gk_t, gdn_conv_w, alr, dtr, alc, dtc,
        nb=nbp, tm=min(_Tiles.inproj_rows, seq), gw=gw, dw=dw)
    og, s_prompt = _gdn_prompt(qn, kn, kb, vb, gcol, grow, z, gn, nb=nbp, seq=seq,
                               tl=min(_Tiles.gdn_tokens, seq))
    od = _attn_prompt(qa, ka, vt, *vecs, sg, nb=nbp, seq=seq, tq=min(_Tiles.attn_tile, seq), lam_init=lam_init)
    hr_p = SUBLANES
    y_p, tail_p = _post(xp2, og, od, jnp.zeros((nbp, hr_p, dff2), F32), w_out_b, g_ffn, w_up_b,
                        ffn_conv_w, fcb, w_down_b, nseq=nbp, tm=min(_Tiles.post_rows, seq), shift=1, hr=hr_p,
                        ncol=_Tiles.post_col_groups)
    out_p = (y_p.reshape(nbp, seq, d),
             tail_g[:, SUBLANES - (GDN_CONV - 1):, :],
             s_prompt,
             knew.reshape(nbp, seq, DIFF_HEADS, DIFF_DV),
             vnew.reshape(nbp, seq, DIFF_HEADS, DIFF_DV),
             tail_p[:, hr_p - (FFN_CONV - 1):, :])

    xs2 = jnp.transpose(x_sample, (1, 0, 2)).reshape(nts * nbs, d)
    tm_s = min(_Tiles.sample_rows, nts * nbs)
    assert tm_s % nbs == 0 and nbs % _Tiles.paged_seqs == 0
    qkv, z, ba, knew, vnew, qa, _, _ = _inproj(xs2, g_attn, w_g, w_d, w_t, gq_t, gk_t, tm=tm_s, gw=gw, dw=dw)
    qkv_tm = qkv.reshape(nts, nbs, 3 * gw)
    hist_tm = jnp.transpose(state_gdn_conv, (1, 0, 2))
    og_tm, s_sample = _gdn_sample(qkv_tm, hist_tm, ba.reshape(nts, nbs, LANES), z.reshape(nts, nbs, gw),
                                  state_gdn_s, gdn_conv_w, alr, dtr, gn, bb=min(_Tiles.state_seqs, nbs))
    conv_all = jnp.concatenate([hist_tm, qkv_tm], axis=0)
    conv_s = jnp.transpose(conv_all[nts:], (1, 0, 2))
    to_bm = lambda a: jnp.transpose(a.reshape(nts, nbs, -1), (1, 0, 2))
    q_bm = to_bm(qa)
    q2 = jnp.concatenate([q_bm, q_bm], axis=1)
    pad = jnp.zeros((nbs, nts, dw), F32)
    kn_bm, vn_bm = to_bm(knew), to_bm(vnew)
    od_bm = _attn_sample(page_table, q2, jnp.concatenate([kn_bm, pad], axis=1),
                         jnp.concatenate([vn_bm, pad], axis=1), cache_k, cache_v, *vecs, sg,
                         nt=nts, bs=_Tiles.paged_seqs, lam_init=lam_init)
    od_tm = jnp.transpose(od_bm[:, :nts, :], (1, 0, 2)).reshape(nts * nbs, dw).astype(BF16)
    hr_s = (FFN_CONV - 1) * nbs
    hist_f = jnp.transpose(state_ffn_conv, (1, 0, 2)).reshape(1, hr_s, dff2)
    y_s, tail_s = _post(xs2, og_tm.reshape(nts * nbs, gw), od_tm, hist_f, w_out_b, g_ffn, w_up_b,
                        ffn_conv_w, fcb, w_down_b, nseq=1, tm=tm_s, shift=nbs, hr=hr_s,
                        ncol=_Tiles.post_col_groups)
    out_s = (jnp.transpose(y_s.reshape(nts, nbs, d), (1, 0, 2)),
             conv_s,
             s_sample,
             kn_bm.reshape(nbs, nts, DIFF_HEADS, DIFF_DV),
             vn_bm.reshape(nbs, nts, DIFF_HEADS, DIFF_DV),
             jnp.transpose(tail_s.reshape(FFN_CONV - 1, nbs, dff2), (1, 0, 2)))
    return out_p, out_s


def kernel(x_prompt, x_sample, state_gdn_conv, state_gdn_S, cache_k, cache_v, page_table, state_ffn_conv, attn_norm_g, w_in, gdn_conv_w, gdn_A_log, gdn_dt_bias, gdn_out_norm_g, diff_q_norm_g, diff_k_norm_g, diff_lambda_q1, diff_lambda_k1, diff_lambda_q2, diff_lambda_k2, diff_subln_g, w_out, ffn_norm_g, w_up, ffn_conv_w, ffn_conv_b, w_down):
    depth = w_in.shape[0]
    hp, hs = x_prompt, x_sample
    outs_p, outs_s = [], []
    for l in range(depth):
        wl = (attn_norm_g[l], w_in[l], gdn_conv_w[l], gdn_A_log[l], gdn_dt_bias[l], gdn_out_norm_g[l],
              diff_q_norm_g[l], diff_k_norm_g[l], diff_lambda_q1[l], diff_lambda_k1[l], diff_lambda_q2[l],
              diff_lambda_k2[l], diff_subln_g[l], w_out[l], ffn_norm_g[l], w_up[l], ffn_conv_w[l],
              ffn_conv_b[l], w_down[l])
        out_p, out_s = _layer(l, hp, hs, state_gdn_conv[l], state_gdn_S[l], cache_k[l], cache_v[l],
                              page_table, state_ffn_conv[l], wl)
        hp, hs = out_p[0], out_s[0]
        outs_p.append(out_p[1:])
        outs_s.append(out_s[1:])
    stack = lambda outs, i: jnp.stack([o[i] for o in outs])
    return (hp, hs) + tuple(stack(outs_p, i) for i in range(5)) + tuple(stack(outs_s, i) for i in range(5))
```
